```python
import math
import jax, jax.numpy as jnp
from jax import lax
import numpy as np

D_MODEL = 1024
BATCH = 8
SEQ = 2048
DEPTH = 2

HEAD_DIM = 64
N_HEADS = D_MODEL // HEAD_DIM
D_FF = 4 * D_MODEL
DECAY_LORA = 64
AAA_LORA = 64
GATE_LORA = 160
N_SHIFT_MIX = 6
BLOCK_Q = 128
N_A_LAYERS = DEPTH // 2
N_B_LAYERS = DEPTH - N_A_LAYERS
NORM_EPS = 1e-6
GN_EPS = 64e-5

kernel_name = "rwkv7_fox_yoco_hybrid"


def rmsnorm(x, g, eps=NORM_EPS):
    xf = x.astype(jnp.float32)
    y = xf * lax.rsqrt(jnp.mean(xf * xf, axis=-1, keepdims=True) + eps)
    return (y * g.astype(jnp.float32)).astype(x.dtype)


def to_heads(t):
    b, s, _ = t.shape
    return t.reshape(b, s, N_HEADS, HEAD_DIM)


def rwkv7_time_mix(h, mu, w_rkv, w0, w1, w2, a0, a1, a2, g1, g2,
                   k_k, k_a, r_k, lnx_w, lnx_b, w_o):
    B, T, D = h.shape
    f32 = jnp.float32
    x_prev = jnp.pad(h[:, :-1], ((0, 0), (1, 0), (0, 0)))
    xx = x_prev - h
    xs = h[None] + xx[None] * mu[:, None, None, :]
    x_r, x_w, x_k, x_v, x_a, x_g = xs[0], xs[1], xs[2], xs[3], xs[4], xs[5]
    rkv = jnp.einsum('pbtd,pde->pbte', jnp.stack([x_r, x_k, x_v]), w_rkv)
    r, k, v = rkv[0], rkv[1], rkv[2]
    w = -jax.nn.softplus(-(w0 + jnp.tanh(x_w @ w1) @ w2)) - 0.5
    decay = jnp.exp(-jnp.exp(w.astype(f32)))
    a = jax.nn.sigmoid(a0 + (x_a @ a1) @ a2)
    g = jax.nn.sigmoid(x_g @ g1) @ g2
    kk = to_heads((k * k_k).astype(f32))
    kk = kk / jnp.maximum(jnp.linalg.norm(kk, axis=-1, keepdims=True), 1e-12)
    k = k * (1.0 + (a - 1.0) * k_a)

    r_h = to_heads(r).astype(f32)
    k_h = to_heads(k).astype(f32)
    v_h = to_heads(v).astype(f32)
    a_h = to_heads(a).astype(f32)
    w_h = to_heads(decay)
    b_h = kk * a_h

    def step(S, inp):
        r_t, w_t, k_t, v_t, kk_t, b_t = inp
        sa = jnp.einsum('bhvk,bhk->bhv', S, -kk_t)
        S = (S * w_t[:, :, None, :] + sa[..., None] * b_t[:, :, None, :]
             + v_t[..., None] * k_t[:, :, None, :])
        y_t = jnp.einsum('bhvk,bhk->bhv', S, r_t)
        return S, y_t

    tm = lambda t: jnp.moveaxis(t, 1, 0)
    S0 = jnp.zeros((B, N_HEADS, HEAD_DIM, HEAD_DIM), f32)
    _, y = lax.scan(step, S0, (tm(r_h), tm(w_h), tm(k_h), tm(v_h), tm(kk), tm(b_h)))
    y = jnp.moveaxis(y, 0, 1)

    mean = jnp.mean(y, axis=-1, keepdims=True)
    var = jnp.mean(jnp.square(y - mean), axis=-1, keepdims=True)
    y = (y - mean) * lax.rsqrt(var + GN_EPS)
    y = y.reshape(B, T, D) * lnx_w.astype(f32) + lnx_b.astype(f32)
    bonus = jnp.sum(r_h * k_h * r_k.astype(f32), axis=-1, keepdims=True) * v_h
    y = (y + bonus.reshape(B, T, D)).astype(h.dtype) * g
    return y @ w_o


def shared_kv(x, kv_norm_g, kv_w, kv_f_bias, k_norm_g):
    D = x.shape[-1]
    h = rmsnorm(x, kv_norm_g)
    proj = h @ kv_w
    k = to_heads(proj[..., :D])
    v = to_heads(proj[..., D:2 * D])
    f_logit = (proj[..., 2 * D:] + kv_f_bias).astype(jnp.float32)
    k = rmsnorm(k, k_norm_g)
    log_f = jax.nn.log_sigmoid(f_logit)
    c = jnp.cumsum(jnp.transpose(log_f, (0, 2, 1)), axis=-1)
    return jnp.transpose(k, (0, 2, 1, 3)), jnp.transpose(v, (0, 2, 1, 3)), c


def forgetting_attention(q, k, v, c):
    T = q.shape[2]
    scale = 1.0 / math.sqrt(HEAD_DIM)
    outs = []
    for s in range(0, T, BLOCK_Q):
        e = s + BLOCK_Q
        logits = jnp.einsum('bhqd,bhkd->bhqk', q[:, :, s:e], k[:, :, :e]).astype(jnp.float32) * scale
        logits = logits + (c[:, :, s:e, None] - c[:, :, None, :e])
        causal = jnp.arange(s, e)[:, None] >= jnp.arange(e)[None, :]
        logits = jnp.where(causal, logits, -jnp.inf)
        p = jax.nn.softmax(logits, axis=-1).astype(v.dtype)
        outs.append(jnp.einsum('bhqk,bhkd->bhqd', p, v[:, :, :e]))
    return jnp.concatenate(outs, axis=2)


def fox_layer(h, w_q, q_norm_g, w_o, k_sh, v_sh, c_sh):
    B, T, D = h.shape
    q = rmsnorm(to_heads(h @ w_q), q_norm_g)
    q = jnp.transpose(q, (0, 2, 1, 3))
    o = forgetting_attention(q, k_sh, v_sh, c_sh)
    o = jnp.transpose(o, (0, 2, 1, 3)).reshape(B, T, D)
    return o @ w_o


def sq_relu_mlp(h, w_in, w_out):
    return jnp.square(jax.nn.relu(h @ w_in)) @ w_out


def setup_inputs(seed: int = 0) -> dict:
    key = jax.random.key(seed)
    ks = iter(jax.random.split(key, 40))
    D, H, N, F = D_MODEL, N_HEADS, HEAD_DIM, D_FF
    nA, nB = N_A_LAYERS, N_B_LAYERS
    nrm = lambda shape, s: s * jax.random.normal(next(ks), shape, jnp.float32)
    gain = lambda shape: 1.0 + nrm(shape, 0.05)
    decay_base = jnp.linspace(-6.5, -1.5, D, dtype=jnp.float32)
    inp = {}
    inp["x"] = jax.random.normal(next(ks), (BATCH, SEQ, D), jnp.float32)
    inp["rwkv_norm_g"] = gain((nA, D))
    inp["rwkv_mu"] = jax.random.uniform(next(ks), (nA, N_SHIFT_MIX, D), jnp.float32)
    inp["rwkv_w_rkv"] = nrm((nA, 3, D, D), D ** -0.5)
    inp["rwkv_w0"] = decay_base[None] + nrm((nA, D), 0.1)
    inp["rwkv_w1"] = nrm((nA, D, DECAY_LORA), D ** -0.5)
    inp["rwkv_w2"] = nrm((nA, DECAY_LORA, D), 0.1 * DECAY_LORA ** -0.5)
    inp["rwkv_a0"] = nrm((nA, D), 0.1)
    inp["rwkv_a1"] = nrm((nA, D, AAA_LORA), D ** -0.5)
    inp["rwkv_a2"] = nrm((nA, AAA_LORA, D), 0.1 * AAA_LORA ** -0.5)
    inp["rwkv_g1"] = nrm((nA, D, GATE_LORA), D ** -0.5)
    inp["rwkv_g2"] = nrm((nA, GATE_LORA, D), GATE_LORA ** -0.5)
    inp["rwkv_k_k"] = 0.85 + nrm((nA, D), 0.05)
    inp["rwkv_k_a"] = 1.0 + nrm((nA, D), 0.05)
    inp["rwkv_r_k"] = nrm((nA, H, N), 0.1)
    inp["rwkv_lnx_w"] = gain((nA, D))
    inp["rwkv_lnx_b"] = nrm((nA, D), 0.02)
    inp["rwkv_w_o"] = nrm((nA, D, D), 0.5 * D ** -0.5)
    inp["kv_norm_g"] = gain((D,))
    inp["kv_w"] = nrm((D, 2 * D + H), D ** -0.5)
    inp["kv_f_bias"] = jax.random.uniform(next(ks), (H,), jnp.float32, 0.5, 3.0)
    inp["k_norm_g"] = gain((H, N))
    inp["attn_norm_g"] = gain((nB, D))
    inp["attn_w_q"] = nrm((nB, D, D), D ** -0.5)
    inp["q_norm_g"] = gain((nB, H, N))
    inp["attn_w_o"] = nrm((nB, D, D), 0.5 * D ** -0.5)
    inp["mlp_norm_g"] = gain((DEPTH, D))
    inp["mlp_w_in"] = nrm((DEPTH, D, F), D ** -0.5)
    inp["mlp_w_out"] = nrm((DEPTH, F, D), 0.5 * F ** -0.5)
    return inp


def reference(x, rwkv_norm_g, rwkv_mu, rwkv_w_rkv, rwkv_w0, rwkv_w1, rwkv_w2,
              rwkv_a0, rwkv_a1, rwkv_a2, rwkv_g1, rwkv_g2, rwkv_k_k, rwkv_k_a,
              rwkv_r_k, rwkv_lnx_w, rwkv_lnx_b, rwkv_w_o,
              kv_norm_g, kv_w, kv_f_bias, k_norm_g,
              attn_norm_g, attn_w_q, q_norm_g, attn_w_o,
              mlp_norm_g, mlp_w_in, mlp_w_out):
    k_sh = v_sh = c_sh = None
    for layer in range(DEPTH):
        if layer < N_A_LAYERS:
            i = layer
            h = rmsnorm(x, rwkv_norm_g[i])
            x = x + rwkv7_time_mix(h, rwkv_mu[i], rwkv_w_rkv[i], rwkv_w0[i], rwkv_w1[i],
                                   rwkv_w2[i], rwkv_a0[i], rwkv_a1[i], rwkv_a2[i],
                                   rwkv_g1[i], rwkv_g2[i], rwkv_k_k[i], rwkv_k_a[i],
                                   rwkv_r_k[i], rwkv_lnx_w[i], rwkv_lnx_b[i], rwkv_w_o[i])
        else:
            j = layer - N_A_LAYERS
            h = rmsnorm(x, attn_norm_g[j])
            x = x + fox_layer(h, attn_w_q[j], q_norm_g[j], attn_w_o[j], k_sh, v_sh, c_sh)
        x = x + sq_relu_mlp(rmsnorm(x, mlp_norm_g[layer]), mlp_w_in[layer], mlp_w_out[layer])
        if layer == N_A_LAYERS - 1:
            k_sh, v_sh, c_sh = shared_kv(x, kv_norm_g, kv_w, kv_f_bias, k_norm_g)
    return x
```

```python
import functools

import jax
import jax.numpy as jnp
from jax import lax
from jax.experimental import pallas as pl
from jax.experimental.pallas import tpu as pltpu

HEAD_DIM = 64
LANES = 128
NORM_EPS = 1e-6
GN_EPS = 64e-5
CHUNK = 64
NEG_BIG = -1e30
VMEM_LIMIT = 56 * 1024 * 1024

BF16 = jnp.bfloat16
F32 = jnp.float32
HI = lax.Precision.HIGHEST

_NT = (((1,), (1,)), ((), ()))
_TN = (((0,), (0,)), ((), ()))


def _dot(a, b, precision=None):
    return jnp.dot(a, b, preferred_element_type=F32, precision=precision)


def _dot_nt(a, b, precision=None):
    return lax.dot_general(a, b, _NT, preferred_element_type=F32, precision=precision)


def _dot_tn(a, b, precision=None):
    return lax.dot_general(a, b, _TN, preferred_element_type=F32, precision=precision)


def _rms(x, g):
    return x * lax.rsqrt(jnp.mean(x * x, axis=-1, keepdims=True) + NORM_EPS) * g


def _head_sum(x):
    outs = []
    for c in range(x.shape[1] // LANES):
        xc = x[:, c * LANES:(c + 1) * LANES]
        lo = lax.broadcasted_iota(jnp.int32, xc.shape, 1) < HEAD_DIM
        s0 = jnp.sum(jnp.where(lo, xc, 0.0), axis=1, keepdims=True)
        s1 = jnp.sum(jnp.where(lo, 0.0, xc), axis=1, keepdims=True)
        outs.append(jnp.where(lo, s0, s1))
    return outs[0] if len(outs) == 1 else jnp.concatenate(outs, axis=1)


def _sigmoid(z):
    return 1.0 / (1.0 + jnp.exp(-z))


def _softplus(z):
    return jnp.maximum(z, 0.0) + jnp.log(1.0 + jnp.exp(-jnp.abs(z)))


def _const_spec(shape):
    nd = len(shape)
    return pl.BlockSpec(shape, lambda *_: (0,) * nd)


def _params(sem):
    return pltpu.CompilerParams(dimension_semantics=sem, vmem_limit_bytes=VMEM_LIMIT)


def _rwkv_prep_kernel(x_ref, xp_ref, ng_ref, mu_ref, wr_ref, wk_ref, wv_ref,
                      w0_ref, w1_ref, w2_ref, a0_ref, a1_ref, a2_ref, g1_ref, g2_ref,
                      kkw_ref, kaw_ref,
                      r_out, lw_out, k_out, v_out, kk_out, b_out, g_out):
    i = pl.program_id(1)
    ng = ng_ref[...]
    h = _rms(x_ref[0], ng)
    hp = _rms(xp_ref[0][7:8, :], ng)
    hp = jnp.where(i > 0, hp, 0.0)
    row = lax.broadcasted_iota(jnp.int32, h.shape, 0)
    hs = jnp.where(row == 0, hp, pltpu.roll(h, 1, 0))
    xx = hs - h
    mu = mu_ref[...]

    def mix(j):
        return (h + xx * mu[j:j + 1, :]).astype(BF16)

    r = _dot(mix(0), wr_ref[...])
    wl = _dot(jnp.tanh(_dot(mix(1), w1_ref[...])).astype(BF16), w2_ref[...])
    k = _dot(mix(2), wk_ref[...])
    v = _dot(mix(3), wv_ref[...])
    al = _dot(_dot(mix(4), a1_ref[...]).astype(BF16), a2_ref[...])
    g = _dot(_sigmoid(_dot(mix(5), g1_ref[...])).astype(BF16), g2_ref[...])

    w = -_softplus(-(w0_ref[...] + wl)) - 0.5
    a = _sigmoid(a0_ref[...] + al)
    kk = k * kkw_ref[...]
    nrm = jnp.sqrt(_head_sum(kk * kk))
    kk = kk / jnp.maximum(nrm, 1e-12)

    r_out[0] = r
    lw_out[0] = -jnp.exp(w)
    k_out[0] = k * (1.0 + (a - 1.0) * kaw_ref[...])
    v_out[0] = v
    kk_out[0] = kk
    b_out[0] = kk * a
    g_out[0] = g


def _rwkv_prep(x, ng, mu, wr, wk, wv, w0, w1, w2, a0, a1, a2, g1, g2, kkw, kaw, *, tm):
    B, T, D = x.shape
    row = lambda a: a.reshape(1, D)
    consts = [row(ng), mu, wr, wk, wv, row(w0), w1, w2, row(a0), a1, a2, g1, g2, row(kkw), row(kaw)]
    tile = pl.BlockSpec((1, tm, D), lambda b, i: (b, i, 0))
    prev = pl.BlockSpec((1, 8, D), lambda b, i: (b, jnp.maximum(i * (tm // 8) - 1, 0), 0))
    out = jax.ShapeDtypeStruct((B, T, D), F32)
    return pl.pallas_call(
        _rwkv_prep_kernel,
        grid=(B, T // tm),
        in_specs=[tile, prev] + [_const_spec(c.shape) for c in consts],
        out_specs=[tile] * 7,
        out_shape=[out] * 7,
        compiler_params=_params(("parallel", "parallel")),
        name="rwkv_prep",
    )(x, x, *consts)


def _blockdiag(z, lo):
    return jnp.concatenate([jnp.where(lo, z, 0.0), jnp.where(lo, 0.0, z)], axis=0)


def _rwkv_scan_kernel(r_ref, lw_ref, k_ref, v_ref, kk_ref, b_ref, g_ref,
                      rk_ref, lnw_ref, lnb_ref, y_out, s_scr, *, npairs, prec):
    c = pl.program_id(2)

    @pl.when(c == 0)
    def _():
        s_scr[...] = jnp.zeros_like(s_scr)

    C = CHUNK
    t_i = lax.broadcasted_iota(jnp.int32, (C, C), 0)
    j_i = lax.broadcasted_iota(jnp.int32, (C, C), 1)
    ltri = (j_i <= t_i).astype(F32)
    row = lax.broadcasted_iota(jnp.int32, (C, LANES), 0)
    lane = lax.broadcasted_iota(jnp.int32, (C, LANES), 1)
    lo = lane < HEAD_DIM
    col = jnp.bitwise_and(lane, HEAD_DIM - 1)
    strict = col < row
    incl = col <= row
    rr = lax.broadcasted_iota(jnp.int32, (LANES, LANES), 0)
    cc = lax.broadcasted_iota(jnp.int32, (LANES, LANES), 1)
    same_head = (rr < HEAD_DIM) == (cc < HEAD_DIM)
    bd = functools.partial(_blockdiag, lo=lo)

    for p in range(npairs):
        sl = slice(p * LANES, (p + 1) * LANES)
        r = r_ref[0][:, sl]
        lw = lw_ref[0][:, sl]
        k = k_ref[0][:, sl]
        v = v_ref[0][:, sl]
        kk = kk_ref[0][:, sl]
        b = b_ref[0][:, sl]

        cw = _dot(ltri, lw, HI)
        cwl = cw[C - 1:C, :]
        at = -kk * jnp.exp(cw - lw)
        dinv = jnp.exp(-cw)
        bt = b * dinv
        kt = k * dinv
        rt = r * jnp.exp(cw)
        dend = jnp.exp(cwl - cw)
        bh = b * dend
        kh = k * dend

        lhs = jnp.concatenate([at, rt], axis=0)
        xb = _dot_nt(lhs, bd(bt), prec)
        xk = _dot_nt(lhs, bd(kt), prec)
        aab = jnp.where(strict, xb[:C], 0.0)
        arb = jnp.where(incl, xb[C:], 0.0)
        aak = jnp.where(strict, xk[:C], 0.0)
        ark = jnp.where(incl, xk[C:], 0.0)

        z1 = at
        z2 = _dot(aak, bd(v), prec)
        n = aab
        for it in range(6):
            parts = [bd(z1), bd(z2)] + ([bd(n)] if it < 5 else [])
            res = _dot(n, jnp.concatenate(parts, axis=1), prec)
            z1 = z1 + res[:, :LANES]
            z2 = z2 + res[:, LANES:2 * LANES]
            if it < 5:
                n = res[:, 2 * LANES:]

        s = s_scr[p]
        ws = _dot_nt(jnp.concatenate([z1, rt], axis=0), s, prec)
        u = ws[:C] + z2
        y = ws[C:] + _dot(jnp.concatenate([arb, ark], axis=1),
                          jnp.concatenate([bd(u), bd(v)], axis=0), prec)
        upd = _dot_tn(jnp.concatenate([u, v], axis=0), jnp.concatenate([bh, kh], axis=0), prec)
        s_scr[p] = s * jnp.exp(cwl) + jnp.where(same_head, upd, 0.0)

        mean = _head_sum(y) * (1.0 / HEAD_DIM)
        d = y - mean
        var = _head_sum(d * d) * (1.0 / HEAD_DIM)
        yn = d * lax.rsqrt(var + GN_EPS)
        bonus = _head_sum(r * k * rk_ref[:, sl]) * v
        out = (yn * lnw_ref[:, sl] + lnb_ref[:, sl] + bonus) * g_ref[0][:, sl]
        y_out[0, :, sl] = out.astype(y_out.dtype)


def _rwkv_scan(r, lw, k, v, kk, b, g, rk, lnw, lnb, *, npairs, prec):
    B, T, D = r.shape
    W = npairs * LANES
    tile = pl.BlockSpec((1, CHUNK, W), lambda bi, p, c: (bi, c, p))
    vec = pl.BlockSpec((1, W), lambda bi, p, c: (0, p))
    return pl.pallas_call(
        functools.partial(_rwkv_scan_kernel, npairs=npairs, prec=prec),
        grid=(B, D // W, T // CHUNK),
        in_specs=[tile] * 7 + [vec] * 3,
        out_specs=tile,
        out_shape=jax.ShapeDtypeStruct((B, T, D), BF16),
        scratch_shapes=[pltpu.VMEM((npairs, LANES, LANES), F32)],
        compiler_params=_params(("parallel", "parallel", "arbitrary")),
        name="rwkv_scan",
    )(r, lw, k, v, kk, b, g, rk.reshape(1, D), lnw.reshape(1, D), lnb.reshape(1, D))


def _proj_res_kernel(res_ref, a_ref, w_ref, o_ref):
    o_ref[...] = res_ref[...] + _dot(a_ref[...], w_ref[...])


def _proj_res(res, a, w, *, tm):
    M, D = res.shape
    tile = pl.BlockSpec((tm, D), lambda i: (i, 0))
    return pl.pallas_call(
        _proj_res_kernel,
        grid=(M // tm,),
        in_specs=[tile, tile, _const_spec(w.shape)],
        out_specs=tile,
        out_shape=jax.ShapeDtypeStruct((M, D), F32),
        compiler_params=_params(("parallel",)),
        name="proj_res",
    )(res, a, w)


def _mlp_kernel(x_ref, g_ref, win_ref, wout_ref, o_ref, *, tf):
    x = x_ref[...]
    xn = _rms(x, g_ref[...]).astype(BF16)
    acc = x
    for f in range(win_ref.shape[1] // tf):
        hid = jnp.maximum(_dot(xn, win_ref[:, f * tf:(f + 1) * tf]), 0.0)
        acc = acc + _dot((hid * hid).astype(BF16), wout_ref[f * tf:(f + 1) * tf, :])
    o_ref[...] = acc


def _mlp(x, g, w_in, w_out, *, tm, tf):
    M, D = x.shape
    tile = pl.BlockSpec((tm, D), lambda i: (i, 0))
    return pl.pallas_call(
        functools.partial(_mlp_kernel, tf=tf),
        grid=(M // tm,),
        in_specs=[tile, _const_spec((1, D)), _const_spec(w_in.shape), _const_spec(w_out.shape)],
        out_specs=tile,
        out_shape=jax.ShapeDtypeStruct((M, D), F32),
        compiler_params=_params(("parallel",)),
        name="sq_relu_mlp",
    )(x, g.reshape(1, D), w_in, w_out)


def _head_rms(t, g):
    ms = _head_sum(t * t) * (1.0 / HEAD_DIM)
    return t * lax.rsqrt(ms + NORM_EPS) * g


def _q_proj_kernel(x_ref, g_ref, w_ref, qg_ref, q_out, *, scale):
    xn = _rms(x_ref[...], g_ref[...]).astype(BF16)
    q = _head_rms(_dot(xn, w_ref[...]), qg_ref[...])
    q_out[...] = (q * scale).astype(q_out.dtype)


def _q_proj(x, g, w, qg, *, tm, scale):
    M, D = x.shape
    tile = pl.BlockSpec((tm, D), lambda i: (i, 0))
    return pl.pallas_call(
        functools.partial(_q_proj_kernel, scale=scale),
        grid=(M // tm,),
        in_specs=[tile, _const_spec((1, D)), _const_spec(w.shape), _const_spec((1, D))],
        out_specs=tile,
        out_shape=jax.ShapeDtypeStruct((M, D), BF16),
        compiler_params=_params(("parallel",)),
        name="q_proj",
    )(x, g.reshape(1, D), w, qg.reshape(1, D))


def _shared_kv_kernel(x_ref, g_ref, wk_ref, wv_ref, wft_ref, fb_ref, kg_ref,
                      k_out, v_out, c_out, carry_scr):
    i = pl.program_id(1)

    @pl.when(i == 0)
    def _():
        carry_scr[...] = jnp.zeros_like(carry_scr)

    hn = _rms(x_ref[0], g_ref[...])
    hb = hn.astype(BF16)
    k_out[0] = _head_rms(_dot(hb, wk_ref[...]), kg_ref[...]).astype(k_out.dtype)
    v_out[0] = _dot(hb, wv_ref[...]).astype(v_out.dtype)

    tm = hn.shape[0]
    f = _dot_nt(wft_ref[...], hn, HI) + fb_ref[...]
    logf = jnp.minimum(f, 0.0) - jnp.log(1.0 + jnp.exp(-jnp.abs(f)))
    j_i = lax.broadcasted_iota(jnp.int32, (tm, tm), 0)
    t_i = lax.broadcasted_iota(jnp.int32, (tm, tm), 1)
    c = _dot(logf, (j_i <= t_i).astype(F32), HI) + carry_scr[:, 0:1]
    c_out[0] = c
    carry_scr[...] = jnp.broadcast_to(c[:, tm - 1:tm], carry_scr.shape)


def _shared_kv(x, g, wk, wv, wft, fb, kg, *, tm):
    B, T, D = x.shape
    H = wft.shape[0]
    tile = pl.BlockSpec((1, tm, D), lambda b, i: (b, i, 0))
    return pl.pallas_call(
        _shared_kv_kernel,
        grid=(B, T // tm),
        in_specs=[tile, _const_spec((1, D)), _const_spec(wk.shape), _const_spec(wv.shape),
                  _const_spec(wft.shape), _const_spec((H, 1)), _const_spec((1, D))],
        out_specs=[tile, tile, pl.BlockSpec((1, H, tm), lambda b, i: (b, 0, i))],
        out_shape=[jax.ShapeDtypeStruct((B, T, D), BF16), jax.ShapeDtypeStruct((B, T, D), BF16),
                   jax.ShapeDtypeStruct((B, H, T), F32)],
        scratch_shapes=[pltpu.VMEM((H, LANES), F32)],
        compiler_params=_params(("parallel", "arbitrary")),
        name="shared_kv",
    )(x, g.reshape(1, D), wk, wv, wft, fb.reshape(H, 1), kg.reshape(1, D))


def _fox_attn_kernel(q_ref, k_ref, v_ref, c_ref, o_ref, *, tq):
    i = pl.program_id(2)
    q = q_ref[0]
    lo = lax.broadcasted_iota(jnp.int32, q.shape, 1) < HEAD_DIM
    zero = jnp.zeros_like(q)
    qs = (jnp.where(lo, q, zero), jnp.where(lo, zero, q))
    r_i = lax.broadcasted_iota(jnp.int32, (tq, tq), 0)
    c_i = lax.broadcasted_iota(jnp.int32, (tq, tq), 1)
    cq_rows = c_ref[0, 0, i]
    cq = [jnp.sum(jnp.where(r_i == c_i, cq_rows[h:h + 1, :], 0.0), axis=1, keepdims=True)
          for h in range(2)]
    causal = c_i <= r_i

    def step(j, carry, masked):
        off = pl.multiple_of(j * tq, tq)
        kb = k_ref[0, pl.ds(off, tq), :]
        vb = v_ref[0, pl.ds(off, tq), :]
        ck = c_ref[0, 0, j]
        new = []
        for h in range(2):
            m, l, acc = carry[h]
            s = _dot_nt(qs[h], kb) + (cq[h] - ck[h:h + 1, :])
            if masked:
                s = jnp.where(causal, s, NEG_BIG)
            m_new = jnp.maximum(m, jnp.max(s, axis=1, keepdims=True))
            p = jnp.exp(s - m_new)
            alpha = jnp.exp(m - m_new)
            l = alpha * l + jnp.sum(p, axis=1, keepdims=True)
            acc = alpha * acc + _dot(p.astype(BF16), vb)
            new.append((m_new, l, acc))
        return tuple(new)

    init = tuple((jnp.full((tq, 1), NEG_BIG, F32), jnp.zeros((tq, 1), F32),
                  jnp.zeros((tq, LANES), F32)) for _ in range(2))
    carry = lax.fori_loop(0, i, functools.partial(step, masked=False), init)
    (_, l0, acc0), (_, l1, acc1) = step(i, carry, True)
    o_ref[0] = jnp.where(lo, acc0 / l0, acc1 / l1).astype(o_ref.dtype)


def _fox_attn(q, k, v, c, *, tq):
    B, T, D = q.shape
    nb = T // tq
    c5 = c.reshape(B, D // LANES, 2, nb, tq).transpose(0, 1, 3, 2, 4)
    qt = pl.BlockSpec((1, tq, LANES), lambda b, p, i: (b, i, p))
    kv = pl.BlockSpec((1, T, LANES), lambda b, p, i: (b, 0, p))
    cs = pl.BlockSpec((1, 1, nb, 2, tq), lambda b, p, i: (b, p, 0, 0, 0))
    return pl.pallas_call(
        functools.partial(_fox_attn_kernel, tq=tq),
        grid=(B, D // LANES, nb),
        in_specs=[qt, kv, kv, cs],
        out_specs=qt,
        out_shape=jax.ShapeDtypeStruct((B, T, D), BF16),
        compiler_params=_params(("parallel", "parallel", "arbitrary")),
        name="fox_attn",
    )(q, k, v, c5)


def kernel(x, rwkv_norm_g, rwkv_mu, rwkv_w_rkv, rwkv_w0, rwkv_w1, rwkv_w2, rwkv_a0, rwkv_a1, rwkv_a2, rwkv_g1, rwkv_g2, rwkv_k_k, rwkv_k_a, rwkv_r_k, rwkv_lnx_w, rwkv_lnx_b, rwkv_w_o, kv_norm_g, kv_w, kv_f_bias, k_norm_g, attn_norm_g, attn_w_q, q_norm_g, attn_w_o, mlp_norm_g, mlp_w_in, mlp_w_out):
    B, T, D = x.shape
    M = B * T
    n_a = rwkv_norm_g.shape[0]
    depth = mlp_norm_g.shape[0]
    bf = lambda w: w.astype(BF16)
    tm_prep = min(256, T)
    tm = min(512, T)
    tq = min(256, T)
    tf = min(1024, mlp_w_in.shape[-1])
    npairs = D // LANES

    k_sh = v_sh = c_sh = None
    for layer in range(depth):
        if layer < n_a:
            i = layer
            r, lw, k, v, kk, b, g = _rwkv_prep(
                x, rwkv_norm_g[i], rwkv_mu[i], bf(rwkv_w_rkv[i, 0]), bf(rwkv_w_rkv[i, 1]),
                bf(rwkv_w_rkv[i, 2]), rwkv_w0[i], bf(rwkv_w1[i]), bf(rwkv_w2[i]), rwkv_a0[i],
                bf(rwkv_a1[i]), bf(rwkv_a2[i]), bf(rwkv_g1[i]), bf(rwkv_g2[i]),
                rwkv_k_k[i], rwkv_k_a[i], tm=tm_prep)
            y = _rwkv_scan(r, lw, k, v, kk, b, g, rwkv_r_k[i], rwkv_lnx_w[i], rwkv_lnx_b[i],
                           npairs=npairs, prec=HI)
            x = _proj_res(x.reshape(M, D), y.reshape(M, D), bf(rwkv_w_o[i]), tm=tm).reshape(B, T, D)
        else:
            j = layer - n_a
            q = _q_proj(x.reshape(M, D), attn_norm_g[j], bf(attn_w_q[j]), q_norm_g[j],
                        tm=tm, scale=HEAD_DIM ** -0.5).reshape(B, T, D)
            o = _fox_attn(q, k_sh, v_sh, c_sh, tq=tq)
            x = _proj_res(x.reshape(M, D), o.reshape(M, D), bf(attn_w_o[j]), tm=tm).reshape(B, T, D)
        x = _mlp(x.reshape(M, D), mlp_norm_g[layer], bf(mlp_w_in[layer]), bf(mlp_w_out[layer]),
                 tm=tm, tf=tf).reshape(B, T, D)
        if layer == n_a - 1:
            k_sh, v_sh, c_sh = _shared_kv(
                x, kv_norm_g, bf(kv_w[:, :D]), bf(kv_w[:, D:2 * D]), kv_w[:, 2 * D:].T,
                kv_f_bias, k_norm_g, tm=tm)
    return x
```

```python
import functools

import jax
import jax.numpy as jnp
from jax import lax
from jax.experimental import pallas as pl
from jax.experimental.pallas import tpu as pltpu

HEAD_DIM = 64
LANES = 128
NORM_EPS = 1e-6
GN_EPS = 64e-5
CHUNK = 64
NEG_BIG = -1e30
VMEM_LIMIT = 56 * 1024 * 1024

BF16 = jnp.bfloat16
F32 = jnp.float32
HI = lax.Precision.HIGHEST

_NT = (((1,), (1,)), ((), ()))
_TN = (((0,), (0,)), ((), ()))


def _dot(a, b, precision=None):
    return jnp.dot(a, b, preferred_element_type=F32, precision=precision)


def _dot_nt(a, b, precision=None):
    return lax.dot_general(a, b, _NT, preferred_element_type=F32, precision=precision)


def _dot_tn(a, b, precision=None):
    return lax.dot_general(a, b, _TN, preferred_element_type=F32, precision=precision)


def _split(a):
    hi = a.astype(BF16)
    return hi, (a - hi.astype(F32)).astype(BF16)


def _rms(x, g):
    return x * lax.rsqrt(jnp.mean(x * x, axis=-1, keepdims=True) + NORM_EPS) * g


def _head_sum(x):
    outs = []
    for c in range(x.shape[1] // LANES):
        xc = x[:, c * LANES:(c + 1) * LANES]
        lo = lax.broadcasted_iota(jnp.int32, xc.shape, 1) < HEAD_DIM
        s0 = jnp.sum(jnp.where(lo, xc, 0.0), axis=1, keepdims=True)
        s1 = jnp.sum(jnp.where(lo, 0.0, xc), axis=1, keepdims=True)
        outs.append(jnp.where(lo, s0, s1))
    return outs[0] if len(outs) == 1 else jnp.concatenate(outs, axis=1)


def _sigmoid(z):
    return 1.0 / (1.0 + jnp.exp(-z))


def _softplus(z):
    return jnp.maximum(z, 0.0) + jnp.log(1.0 + jnp.exp(-jnp.abs(z)))


def _const_spec(shape):
    nd = len(shape)
    return pl.BlockSpec(shape, lambda *_: (0,) * nd)


def _params(sem):
    return pltpu.CompilerParams(dimension_semantics=sem, vmem_limit_bytes=VMEM_LIMIT)


def _rwkv_prep_kernel(x_ref, xp_ref, ng_ref, mu_ref, wr_ref, wk_ref, wv_ref,
                      w0_ref, w1_ref, w2_ref, a0_ref, a1_ref, a2_ref, g1_ref, g2_ref,
                      kkw_ref, kaw_ref,
                      r_out, lw_out, k_out, v_out, kk_out, b_out, g_out):
    i = pl.program_id(1)
    ng = ng_ref[...]
    h = _rms(x_ref[0], ng)
    hp = _rms(xp_ref[0][7:8, :], ng)
    hp = jnp.where(i > 0, hp, 0.0)
    row = lax.broadcasted_iota(jnp.int32, h.shape, 0)
    hs = jnp.where(row == 0, hp, pltpu.roll(h, 1, 0))
    xx = hs - h
    mu = mu_ref[...]

    def mix(j):
        return (h + xx * mu[j:j + 1, :]).astype(BF16)

    r = _dot(mix(0), wr_ref[...])
    wl = _dot(jnp.tanh(_dot(mix(1), w1_ref[...])).astype(BF16), w2_ref[...])
    k = _dot(mix(2), wk_ref[...])
    v = _dot(mix(3), wv_ref[...])
    al = _dot(_dot(mix(4), a1_ref[...]).astype(BF16), a2_ref[...])
    g = _dot(_sigmoid(_dot(mix(5), g1_ref[...])).astype(BF16), g2_ref[...])

    w = -_softplus(-(w0_ref[...] + wl)) - 0.5
    a = _sigmoid(a0_ref[...] + al)
    kk = k * kkw_ref[...]
    nrm = jnp.sqrt(_head_sum(kk * kk))
    kk = kk / jnp.maximum(nrm, 1e-12)

    r_out[0] = r
    lw_out[0] = -jnp.exp(w)
    k_out[0] = k * (1.0 + (a - 1.0) * kaw_ref[...])
    v_out[0] = v
    kk_out[0] = kk
    b_out[0] = kk * a
    g_out[0] = g


def _rwkv_prep(x, ng, mu, wr, wk, wv, w0, w1, w2, a0, a1, a2, g1, g2, kkw, kaw, *, tm):
    B, T, D = x.shape
    row = lambda a: a.reshape(1, D)
    consts = [row(ng), mu, wr, wk, wv, row(w0), w1, w2, row(a0), a1, a2, g1, g2, row(kkw), row(kaw)]
    tile = pl.BlockSpec((1, tm, D), lambda b, i: (b, i, 0))
    prev = pl.BlockSpec((1, 8, D), lambda b, i: (b, jnp.maximum(i * (tm // 8) - 1, 0), 0))
    out = jax.ShapeDtypeStruct((B, T, D), F32)
    return pl.pallas_call(
        _rwkv_prep_kernel,
        grid=(B, T // tm),
        in_specs=[tile, prev] + [_const_spec(c.shape) for c in consts],
        out_specs=[tile] * 7,
        out_shape=[out] * 7,
        compiler_params=_params(("parallel", "parallel")),
        name="rwkv_prep",
    )(x, x, *consts)


def _blockdiag(z, lo):
    z = z.astype(BF16)
    zero = jnp.zeros_like(z)
    return jnp.concatenate([jnp.where(lo, z, zero), jnp.where(lo, zero, z)], axis=0)


def _rwkv_scan_kernel(r_ref, lw_ref, k_ref, v_ref, kk_ref, b_ref, g_ref,
                      rk_ref, lnw_ref, lnb_ref, y_out, s_scr, *, npairs):
    c = pl.program_id(2)

    @pl.when(c == 0)
    def _():
        s_scr[...] = jnp.zeros_like(s_scr)

    C = CHUNK
    t_i = lax.broadcasted_iota(jnp.int32, (C, C), 0)
    j_i = lax.broadcasted_iota(jnp.int32, (C, C), 1)
    ltri = (j_i <= t_i).astype(BF16)
    row = lax.broadcasted_iota(jnp.int32, (C, LANES), 0)
    lane = lax.broadcasted_iota(jnp.int32, (C, LANES), 1)
    lo = lane < HEAD_DIM
    col = jnp.bitwise_and(lane, HEAD_DIM - 1)
    strict = col < row
    incl = col <= row
    rr = lax.broadcasted_iota(jnp.int32, (LANES, LANES), 0)
    cc = lax.broadcasted_iota(jnp.int32, (LANES, LANES), 1)
    same_head = (rr < HEAD_DIM) == (cc < HEAD_DIM)
    bd = functools.partial(_blockdiag, lo=lo)
    cat0 = lambda *xs: jnp.concatenate([x.astype(BF16) for x in xs], axis=0)
    cat1 = lambda *xs: jnp.concatenate([x.astype(BF16) for x in xs], axis=1)

    P = range(npairs)
    sls = [slice(p * LANES, (p + 1) * LANES) for p in P]
    r = [r_ref[0, :, sl] for sl in sls]
    lw = [lw_ref[0, :, sl] for sl in sls]
    k = [k_ref[0, :, sl] for sl in sls]
    v = [v_ref[0, :, sl] for sl in sls]
    kk = [kk_ref[0, :, sl] for sl in sls]
    b = [b_ref[0, :, sl] for sl in sls]

    cw2 = [_dot(ltri, cat1(*_split(lw[p]))) for p in P]
    cw = [cw2[p][:, :LANES] + cw2[p][:, LANES:] for p in P]
    cwl = [cw[p][C - 1:C, :] for p in P]
    at = [-kk[p] * jnp.exp(cw[p] - lw[p]) for p in P]
    dinv = [jnp.exp(-cw[p]) for p in P]
    rt = [r[p] * jnp.exp(cw[p]) for p in P]
    dend = [jnp.exp(cwl[p] - cw[p]) for p in P]

    x = [_dot_nt(cat0(at[p], rt[p]), cat0(bd(b[p] * dinv[p]), bd(k[p] * dinv[p]))) for p in P]
    aab = [jnp.where(strict, x[p][:C, :LANES], 0.0) for p in P]
    arb = [jnp.where(incl, x[p][C:, :LANES], 0.0) for p in P]
    aak = [jnp.where(strict, x[p][:C, LANES:], 0.0) for p in P]
    ark = [jnp.where(incl, x[p][C:, LANES:], 0.0) for p in P]

    bdv = [bd(v[p]) for p in P]
    z1 = at
    z2 = [_dot(aak[p].astype(BF16), bdv[p]) for p in P]
    n = aab
    for it in range(6):
        last = it == 5
        res = [_dot(n[p].astype(BF16),
                    jnp.concatenate([bd(z1[p]), bd(z2[p])] + ([] if last else [bd(n[p])]), axis=1))
               for p in P]
        z1 = [z1[p] + res[p][:, :LANES] for p in P]
        z2 = [z2[p] + res[p][:, LANES:2 * LANES] for p in P]
        if not last:
            n = [res[p][:, 2 * LANES:] for p in P]

    s = [s_scr[p] for p in P]
    ws = [_dot_nt(cat0(z1[p], rt[p]), s[p].astype(BF16)) for p in P]
    u = [ws[p][:C] + z2[p] for p in P]
    y = [ws[p][C:] + _dot(cat1(arb[p], ark[p]), cat0(bd(u[p]), bdv[p])) for p in P]
    upd = [_dot_tn(cat0(u[p], v[p]), cat0(b[p] * dend[p], k[p] * dend[p])) for p in P]
    for p in P:
        s_scr[p] = s[p] * jnp.exp(cwl[p]) + jnp.where(same_head, upd[p], 0.0)

    for p in P:
        sl = sls[p]
        mean = _head_sum(y[p]) * (1.0 / HEAD_DIM)
        d = y[p] - mean
        var = _head_sum(d * d) * (1.0 / HEAD_DIM)
        yn = d * lax.rsqrt(var + GN_EPS)
        bonus = _head_sum(r[p] * k[p] * rk_ref[:, sl]) * v[p]
        out = (yn * lnw_ref[:, sl] + lnb_ref[:, sl] + bonus) * g_ref[0, :, sl]
        y_out[0, :, sl] = out.astype(y_out.dtype)


def _rwkv_scan(r, lw, k, v, kk, b, g, rk, lnw, lnb, *, npairs):
    B, T, D = r.shape
    W = npairs * LANES
    tile = pl.BlockSpec((1, CHUNK, W), lambda bi, p, c: (bi, c, p))
    vec = pl.BlockSpec((1, W), lambda bi, p, c: (0, p))
    return pl.pallas_call(
        functools.partial(_rwkv_scan_kernel, npairs=npairs),
        grid=(B, D // W, T // CHUNK),
        in_specs=[tile] * 7 + [vec] * 3,
        out_specs=tile,
        out_shape=jax.ShapeDtypeStruct((B, T, D), BF16),
        scratch_shapes=[pltpu.VMEM((npairs, LANES, LANES), F32)],
        compiler_params=_params(("parallel", "parallel", "arbitrary")),
        name="rwkv_scan",
    )(r, lw, k, v, kk, b, g, rk.reshape(1, D), lnw.reshape(1, D), lnb.reshape(1, D))


def _proj_res_kernel(res_ref, a_ref, w_ref, o_ref):
    o_ref[...] = res_ref[...] + _dot(a_ref[...], w_ref[...])


def _proj_res(res, a, w, *, tm):
    M, D = res.shape
    tile = pl.BlockSpec((tm, D), lambda i: (i, 0))
    return pl.pallas_call(
        _proj_res_kernel,
        grid=(M // tm,),
        in_specs=[tile, tile, _const_spec(w.shape)],
        out_specs=tile,
        out_shape=jax.ShapeDtypeStruct((M, D), F32),
        compiler_params=_params(("parallel",)),
        name="proj_res",
    )(res, a, w)


def _mlp_kernel(x_ref, g_ref, win_ref, wout_ref, o_ref, *, tf):
    x = x_ref[...]
    xn = _rms(x, g_ref[...]).astype(BF16)
    acc = x
    for f in range(win_ref.shape[1] // tf):
        hid = jnp.maximum(_dot(xn, win_ref[:, f * tf:(f + 1) * tf]), 0.0)
        acc = acc + _dot((hid * hid).astype(BF16), wout_ref[f * tf:(f + 1) * tf, :])
    o_ref[...] = acc


def _mlp(x, g, w_in, w_out, *, tm, tf):
    M, D = x.shape
    tile = pl.BlockSpec((tm, D), lambda i: (i, 0))
    return pl.pallas_call(
        functools.partial(_mlp_kernel, tf=tf),
        grid=(M // tm,),
        in_specs=[tile, _const_spec((1, D)), _const_spec(w_in.shape), _const_spec(w_out.shape)],
        out_specs=tile,
        out_shape=jax.ShapeDtypeStruct((M, D), F32),
        compiler_params=_params(("parallel",)),
        name="sq_relu_mlp",
    )(x, g.reshape(1, D), w_in, w_out)


def _head_rms(t, g):
    ms = _head_sum(t * t) * (1.0 / HEAD_DIM)
    return t * lax.rsqrt(ms + NORM_EPS) * g


def _q_proj_kernel(x_ref, g_ref, w_ref, qg_ref, q_out, *, scale):
    xn = _rms(x_ref[...], g_ref[...]).astype(BF16)
    q = _head_rms(_dot(xn, w_ref[...]), qg_ref[...])
    q_out[...] = (q * scale).astype(q_out.dtype)


def _q_proj(x, g, w, qg, *, tm, scale):
    M, D = x.shape
    tile = pl.BlockSpec((tm, D), lambda i: (i, 0))
    return pl.pallas_call(
        functools.partial(_q_proj_kernel, scale=scale),
        grid=(M // tm,),
        in_specs=[tile, _const_spec((1, D)), _const_spec(w.shape), _const_spec((1, D))],
        out_specs=tile,
        out_shape=jax.ShapeDtypeStruct((M, D), BF16),
        compiler_params=_params(("parallel",)),
        name="q_proj",
    )(x, g.reshape(1, D), w, qg.reshape(1, D))


def _shared_kv_kernel(x_ref, g_ref, wk_ref, wv_ref, wft_ref, fb_ref, kg_ref,
                      k_out, v_out, c_out, carry_scr):
    i = pl.program_id(1)

    @pl.when(i == 0)
    def _():
        carry_scr[...] = jnp.zeros_like(carry_scr)

    hn = _rms(x_ref[0], g_ref[...])
    hb = hn.astype(BF16)
    k_out[0] = _head_rms(_dot(hb, wk_ref[...]), kg_ref[...]).astype(k_out.dtype)
    v_out[0] = _dot(hb, wv_ref[...]).astype(v_out.dtype)

    tm = hn.shape[0]
    f = _dot_nt(wft_ref[...], hn, HI) + fb_ref[...]
    logf = jnp.minimum(f, 0.0) - jnp.log(1.0 + jnp.exp(-jnp.abs(f)))
    j_i = lax.broadcasted_iota(jnp.int32, (tm, tm), 0)
    t_i = lax.broadcasted_iota(jnp.int32, (tm, tm), 1)
    c = _dot(logf, (j_i <= t_i).astype(F32), HI) + carry_scr[:, 0:1]
    c_out[0] = c
    carry_scr[...] = jnp.broadcast_to(c[:, tm - 1:tm], carry_scr.shape)


def _shared_kv(x, g, wk, wv, wft, fb, kg, *, tm):
    B, T, D = x.shape
    H = wft.shape[0]
    tile = pl.BlockSpec((1, tm, D), lambda b, i: (b, i, 0))
    return pl.pallas_call(
        _shared_kv_kernel,
        grid=(B, T // tm),
        in_specs=[tile, _const_spec((1, D)), _const_spec(wk.shape), _const_spec(wv.shape),
                  _const_spec(wft.shape), _const_spec((H, 1)), _const_spec((1, D))],
        out_specs=[tile, tile, pl.BlockSpec((1, H, tm), lambda b, i: (b, 0, i))],
        out_shape=[jax.ShapeDtypeStruct((B, T, D), BF16), jax.ShapeDtypeStruct((B, T, D), BF16),
                   jax.ShapeDtypeStruct((B, H, T), F32)],
        scratch_shapes=[pltpu.VMEM((H, LANES), F32)],
        compiler_params=_params(("parallel", "arbitrary")),
        name="shared_kv",
    )(x, g.reshape(1, D), wk, wv, wft, fb.reshape(H, 1), kg.reshape(1, D))


def _fox_attn_kernel(q_ref, k_ref, v_ref, c_ref, o_ref, *, tq):
    i = pl.program_id(2)
    q = q_ref[0]
    lo = lax.broadcasted_iota(jnp.int32, q.shape, 1) < HEAD_DIM
    zero = jnp.zeros_like(q)
    qs = (jnp.where(lo, q, zero), jnp.where(lo, zero, q))
    r_i = lax.broadcasted_iota(jnp.int32, (tq, tq), 0)
    c_i = lax.broadcasted_iota(jnp.int32, (tq, tq), 1)
    cq_rows = c_ref[0, 0, i]
    cq = [jnp.sum(jnp.where(r_i == c_i, cq_rows[h:h + 1, :], 0.0), axis=1, keepdims=True)
          for h in range(2)]
    causal = c_i <= r_i

    def step(j, carry, masked):
        off = pl.multiple_of(j * tq, tq)
        kb = k_ref[0, pl.ds(off, tq), :]
        vb = v_ref[0, pl.ds(off, tq), :]
        ck = c_ref[0, 0, j]
        new = []
        for h in range(2):
            m, l, acc = carry[h]
            s = _dot_nt(qs[h], kb) + (cq[h] - ck[h:h + 1, :])
            if masked:
                s = jnp.where(causal, s, NEG_BIG)
            m_new = jnp.maximum(m, jnp.max(s, axis=1, keepdims=True))
            p = jnp.exp(s - m_new)
            alpha = jnp.exp(m - m_new)
            l = alpha * l + jnp.sum(p, axis=1, keepdims=True)
            acc = alpha * acc + _dot(p.astype(BF16), vb)
            new.append((m_new, l, acc))
        return tuple(new)

    init = tuple((jnp.full((tq, 1), NEG_BIG, F32), jnp.zeros((tq, 1), F32),
                  jnp.zeros((tq, LANES), F32)) for _ in range(2))
    carry = lax.fori_loop(0, i, functools.partial(step, masked=False), init)
    (_, l0, acc0), (_, l1, acc1) = step(i, carry, True)
    o_ref[0] = jnp.where(lo, acc0 / l0, acc1 / l1).astype(o_ref.dtype)


def _fox_attn(q, k, v, c, *, tq):
    B, T, D = q.shape
    nb = T // tq
    c5 = c.reshape(B, D // LANES, 2, nb, tq).transpose(0, 1, 3, 2, 4)
    qt = pl.BlockSpec((1, tq, LANES), lambda b, p, i: (b, i, p))
    kv = pl.BlockSpec((1, T, LANES), lambda b, p, i: (b, 0, p))
    cs = pl.BlockSpec((1, 1, nb, 2, tq), lambda b, p, i: (b, p, 0, 0, 0))
    return pl.pallas_call(
        functools.partial(_fox_attn_kernel, tq=tq),
        grid=(B, D // LANES, nb),
        in_specs=[qt, kv, kv, cs],
        out_specs=qt,
        out_shape=jax.ShapeDtypeStruct((B, T, D), BF16),
        compiler_params=_params(("parallel", "parallel", "arbitrary")),
        name="fox_attn",
    )(q, k, v, c5)


def kernel(x, rwkv_norm_g, rwkv_mu, rwkv_w_rkv, rwkv_w0, rwkv_w1, rwkv_w2, rwkv_a0, rwkv_a1, rwkv_a2, rwkv_g1, rwkv_g2, rwkv_k_k, rwkv_k_a, rwkv_r_k, rwkv_lnx_w, rwkv_lnx_b, rwkv_w_o, kv_norm_g, kv_w, kv_f_bias, k_norm_g, attn_norm_g, attn_w_q, q_norm_g, attn_w_o, mlp_norm_g, mlp_w_in, mlp_w_out):
    B, T, D = x.shape
    M = B * T
    n_a = rwkv_norm_g.shape[0]
    depth = mlp_norm_g.shape[0]
    bf = lambda w: w.astype(BF16)
    tm_prep = min(256, T)
    tm = min(512, T)
    tq = min(256, T)
    tf = min(1024, mlp_w_in.shape[-1])
    npairs = D // LANES

    k_sh = v_sh = c_sh = None
    for layer in range(depth):
        if layer < n_a:
            i = layer
            r, lw, k, v, kk, b, g = _rwkv_prep(
                x, rwkv_norm_g[i], rwkv_mu[i], bf(rwkv_w_rkv[i, 0]), bf(rwkv_w_rkv[i, 1]),
                bf(rwkv_w_rkv[i, 2]), rwkv_w0[i], bf(rwkv_w1[i]), bf(rwkv_w2[i]), rwkv_a0[i],
                bf(rwkv_a1[i]), bf(rwkv_a2[i]), bf(rwkv_g1[i]), bf(rwkv_g2[i]),
                rwkv_k_k[i], rwkv_k_a[i], tm=tm_prep)
            y = _rwkv_scan(r, lw, k, v, kk, b, g, rwkv_r_k[i], rwkv_lnx_w[i], rwkv_lnx_b[i],
                           npairs=npairs)
            x = _proj_res(x.reshape(M, D), y.reshape(M, D), bf(rwkv_w_o[i]), tm=tm).reshape(B, T, D)
        else:
            j = layer - n_a
            q = _q_proj(x.reshape(M, D), attn_norm_g[j], bf(attn_w_q[j]), q_norm_g[j],
                        tm=tm, scale=HEAD_DIM ** -0.5).reshape(B, T, D)
            o = _fox_attn(q, k_sh, v_sh, c_sh, tq=tq)
            x = _proj_res(x.reshape(M, D), o.reshape(M, D), bf(attn_w_o[j]), tm=tm).reshape(B, T, D)
        x = _mlp(x.reshape(M, D), mlp_norm_g[layer], bf(mlp_w_in[layer]), bf(mlp_w_out[layer]),
                 tm=tm, tf=tf).reshape(B, T, D)
        if layer == n_a - 1:
            k_sh, v_sh, c_sh = _shared_kv(
                x, kv_norm_g, bf(kv_w[:, :D]), bf(kv_w[:, D:2 * D]), kv_w[:, 2 * D:].T,
                kv_f_bias, k_norm_g, tm=tm)
    return x
```

```python
import functools

import jax
import jax.numpy as jnp
from jax import lax
from jax.experimental import pallas as pl
from jax.experimental.pallas import tpu as pltpu

HEAD_DIM = 64
LANES = 128
NORM_EPS = 1e-6
GN_EPS = 64e-5
CHUNK = 64
NEG_BIG = -1e30
LOG2E = 1.4426950408889634
VMEM_LIMIT = 56 * 1024 * 1024

BF16 = jnp.bfloat16
F32 = jnp.float32

_NT = (((1,), (1,)), ((), ()))
_TN = (((0,), (0,)), ((), ()))


def _dot(a, b):
    return jnp.dot(a, b, preferred_element_type=F32)


def _dot_nt(a, b):
    return lax.dot_general(a, b, _NT, preferred_element_type=F32)


def _dot_tn(a, b):
    return lax.dot_general(a, b, _TN, preferred_element_type=F32)


def _split(a):
    hi = a.astype(BF16)
    return hi, (a - hi.astype(F32)).astype(BF16)


def _split3(a):
    hi = a.astype(BF16).astype(F32)
    r1 = a - hi
    mid = r1.astype(BF16).astype(F32)
    return hi, mid, r1 - mid


def _rms(x, g):
    return x * lax.rsqrt(jnp.mean(x * x, axis=-1, keepdims=True) + NORM_EPS) * g


def _head_sum(x):
    outs = []
    for c in range(x.shape[1] // LANES):
        xc = x[:, c * LANES:(c + 1) * LANES]
        lo = lax.broadcasted_iota(jnp.int32, xc.shape, 1) < HEAD_DIM
        s0 = jnp.sum(jnp.where(lo, xc, 0.0), axis=1, keepdims=True)
        s1 = jnp.sum(jnp.where(lo, 0.0, xc), axis=1, keepdims=True)
        outs.append(jnp.where(lo, s0, s1))
    return outs[0] if len(outs) == 1 else jnp.concatenate(outs, axis=1)


def _head_rms(t, g):
    ms = _head_sum(t * t) * (1.0 / HEAD_DIM)
    return t * lax.rsqrt(ms + NORM_EPS) * g


def _sigmoid(z):
    return 1.0 / (1.0 + jnp.exp(-z))


def _softplus(z):
    return jnp.maximum(z, 0.0) + jnp.log(1.0 + jnp.exp(-jnp.abs(z)))


def _const_spec(shape):
    nd = len(shape)
    return pl.BlockSpec(shape, lambda *_: (0,) * nd)


def _params(sem):
    return pltpu.CompilerParams(dimension_semantics=sem, vmem_limit_bytes=VMEM_LIMIT)


def _rwkv_prep_kernel(x_ref, xp_ref, ng_ref, mu_ref, wr_ref, wk_ref, wv_ref,
                      w0_ref, w1_ref, w2_ref, a0_ref, a1_ref, a2_ref, g1_ref, g2_ref,
                      kkw_ref, kaw_ref,
                      r_out, lw_out, k_out, v_out, kk_out, b_out, g_out):
    i = pl.program_id(1)
    ng = ng_ref[...]
    h = _rms(x_ref[0], ng)
    hp = _rms(xp_ref[0][7:8, :], ng)
    hp = jnp.where(i > 0, hp, 0.0)
    row = lax.broadcasted_iota(jnp.int32, h.shape, 0)
    hs = jnp.where(row == 0, hp, pltpu.roll(h, 1, 0))
    xx = hs - h
    mu = mu_ref[...]

    def mix(j):
        return (h + xx * mu[j:j + 1, :]).astype(BF16)

    r = _dot(mix(0), wr_ref[...])
    wl = _dot(jnp.tanh(_dot(mix(1), w1_ref[...])).astype(BF16), w2_ref[...])
    k = _dot(mix(2), wk_ref[...])
    v = _dot(mix(3), wv_ref[...])
    al = _dot(_dot(mix(4), a1_ref[...]).astype(BF16), a2_ref[...])
    g = _dot(_sigmoid(_dot(mix(5), g1_ref[...])).astype(BF16), g2_ref[...])

    w = -_softplus(-(w0_ref[...] + wl)) - 0.5
    a = _sigmoid(a0_ref[...] + al)
    kk = k * kkw_ref[...]
    nrm = jnp.sqrt(_head_sum(kk * kk))
    kk = kk / jnp.maximum(nrm, 1e-12)

    r_out[0] = r
    lw_out[0] = -jnp.exp(w)
    k_out[0] = k * (1.0 + (a - 1.0) * kaw_ref[...])
    v_out[0] = v
    kk_out[0] = kk
    b_out[0] = kk * a
    g_out[0] = g


def _rwkv_prep(x, ng, mu, wr, wk, wv, w0, w1, w2, a0, a1, a2, g1, g2, kkw, kaw, *, tm):
    B, T, D = x.shape
    row = lambda a: a.reshape(1, D)
    consts = [row(ng), mu, wr, wk, wv, row(w0), w1, w2, row(a0), a1, a2, g1, g2, row(kkw), row(kaw)]
    tile = pl.BlockSpec((1, tm, D), lambda b, i: (b, i, 0))
    prev = pl.BlockSpec((1, 8, D), lambda b, i: (b, jnp.maximum(i * (tm // 8) - 1, 0), 0))
    out = jax.ShapeDtypeStruct((B, T, D), F32)
    return pl.pallas_call(
        _rwkv_prep_kernel,
        grid=(B, T // tm),
        in_specs=[tile, prev] + [_const_spec(c.shape) for c in consts],
        out_specs=[tile] * 7,
        out_shape=[out] * 7,
        compiler_params=_params(("parallel", "parallel")),
        name="rwkv_prep",
    )(x, x, *consts)


def _blockdiag(z, lo):
    z = z.astype(BF16)
    zero = jnp.zeros_like(z)
    return jnp.concatenate([jnp.where(lo, z, zero), jnp.where(lo, zero, z)], axis=0)


def _rwkv_scan_kernel(r_ref, lw_ref, k_ref, v_ref, kk_ref, b_ref, g_ref,
                      rk_ref, lnw_ref, lnb_ref, y_out, s_scr, *, npairs):
    c = pl.program_id(2)

    @pl.when(c == 0)
    def _():
        s_scr[...] = jnp.zeros_like(s_scr)

    C = CHUNK
    t_i = lax.broadcasted_iota(jnp.int32, (C, C), 0)
    j_i = lax.broadcasted_iota(jnp.int32, (C, C), 1)
    ltri = (j_i <= t_i).astype(BF16)
    row = lax.broadcasted_iota(jnp.int32, (C, LANES), 0)
    lane = lax.broadcasted_iota(jnp.int32, (C, LANES), 1)
    lo = lane < HEAD_DIM
    col = jnp.bitwise_and(lane, HEAD_DIM - 1)
    strict = col < row
    incl = col <= row
    rr = lax.broadcasted_iota(jnp.int32, (LANES, LANES), 0)
    cc = lax.broadcasted_iota(jnp.int32, (LANES, LANES), 1)
    same_head = (rr < HEAD_DIM) == (cc < HEAD_DIM)
    bd = functools.partial(_blockdiag, lo=lo)
    cat0 = lambda *xs: jnp.concatenate([x.astype(BF16) for x in xs], axis=0)
    cat1 = lambda *xs: jnp.concatenate([x.astype(BF16) for x in xs], axis=1)

    P = range(npairs)
    sls = [slice(p * LANES, (p + 1) * LANES) for p in P]
    r = [r_ref[0, :, sl] for sl in sls]
    lw = [lw_ref[0, :, sl] for sl in sls]
    k = [k_ref[0, :, sl] for sl in sls]
    v = [v_ref[0, :, sl] for sl in sls]
    kk = [kk_ref[0, :, sl] for sl in sls]
    b = [b_ref[0, :, sl] for sl in sls]

    cw2 = [_dot(ltri, cat1(*_split(lw[p]))) for p in P]
    cw = [cw2[p][:, :LANES] + cw2[p][:, LANES:] for p in P]
    cwl = [cw[p][C - 1:C, :] for p in P]
    at = [-kk[p] * jnp.exp(cw[p] - lw[p]) for p in P]
    dinv = [jnp.exp(-cw[p]) for p in P]
    rt = [r[p] * jnp.exp(cw[p]) for p in P]
    dend = [jnp.exp(cwl[p] - cw[p]) for p in P]

    x = [_dot_nt(cat0(at[p], rt[p]), cat0(bd(b[p] * dinv[p]), bd(k[p] * dinv[p]))) for p in P]
    aab = [jnp.where(strict, x[p][:C, :LANES], 0.0) for p in P]
    arb = [jnp.where(incl, x[p][C:, :LANES], 0.0) for p in P]
    aak = [jnp.where(strict, x[p][:C, LANES:], 0.0) for p in P]
    ark = [jnp.where(incl, x[p][C:, LANES:], 0.0) for p in P]

    bdv = [bd(v[p]) for p in P]
    z1 = at
    z2 = [_dot(aak[p].astype(BF16), bdv[p]) for p in P]
    n = aab
    for it in range(6):
        last = it == 5
        res = [_dot(n[p].astype(BF16),
                    jnp.concatenate([bd(z1[p]), bd(z2[p])] + ([] if last else [bd(n[p])]), axis=1))
               for p in P]
        z1 = [z1[p] + res[p][:, :LANES] for p in P]
        z2 = [z2[p] + res[p][:, LANES:2 * LANES] for p in P]
        if not last:
            n = [res[p][:, 2 * LANES:] for p in P]

    s = [s_scr[p] for p in P]
    ws = [_dot_nt(cat0(z1[p], rt[p]), s[p].astype(BF16)) for p in P]
    u = [ws[p][:C] + z2[p] for p in P]
    y = [ws[p][C:] + _dot(cat1(arb[p], ark[p]), cat0(bd(u[p]), bdv[p])) for p in P]
    upd = [_dot_tn(cat0(u[p], v[p]), cat0(b[p] * dend[p], k[p] * dend[p])) for p in P]
    for p in P:
        s_scr[p] = s[p] * jnp.exp(cwl[p]) + jnp.where(same_head, upd[p], 0.0)

    for p in P:
        sl = sls[p]
        mean = _head_sum(y[p]) * (1.0 / HEAD_DIM)
        d = y[p] - mean
        var = _head_sum(d * d) * (1.0 / HEAD_DIM)
        yn = d * lax.rsqrt(var + GN_EPS)
        bonus = _head_sum(r[p] * k[p] * rk_ref[:, sl]) * v[p]
        out = (yn * lnw_ref[:, sl] + lnb_ref[:, sl] + bonus) * g_ref[0, :, sl]
        y_out[0, :, sl] = out.astype(y_out.dtype)


def _rwkv_scan(r, lw, k, v, kk, b, g, rk, lnw, lnb, *, npairs):
    B, T, D = r.shape
    W = npairs * LANES
    tile = pl.BlockSpec((1, CHUNK, W), lambda bi, p, c: (bi, c, p))
    vec = pl.BlockSpec((1, W), lambda bi, p, c: (0, p))
    return pl.pallas_call(
        functools.partial(_rwkv_scan_kernel, npairs=npairs),
        grid=(B, D // W, T // CHUNK),
        in_specs=[tile] * 7 + [vec] * 3,
        out_specs=tile,
        out_shape=jax.ShapeDtypeStruct((B, T, D), BF16),
        scratch_shapes=[pltpu.VMEM((npairs, LANES, LANES), F32)],
        compiler_params=_params(("parallel", "parallel", "arbitrary")),
        name="rwkv_scan",
    )(r, lw, k, v, kk, b, g, rk.reshape(1, D), lnw.reshape(1, D), lnb.reshape(1, D))


def _proj_res_kernel(res_ref, a_ref, w_ref, o_ref):
    o_ref[...] = res_ref[...] + _dot(a_ref[...], w_ref[...])


def _proj_res(res, a, w, *, tm):
    M, D = res.shape
    tile = pl.BlockSpec((tm, D), lambda i: (i, 0))
    return pl.pallas_call(
        _proj_res_kernel,
        grid=(M // tm,),
        in_specs=[tile, tile, _const_spec(w.shape)],
        out_specs=tile,
        out_shape=jax.ShapeDtypeStruct((M, D), F32),
        compiler_params=_params(("parallel",)),
        name="proj_res",
    )(res, a, w)


def _mlp_kernel(x_ref, g_ref, win_ref, wout_ref, o_ref, *, tf):
    x = x_ref[...]
    xn = _rms(x, g_ref[...]).astype(BF16)
    acc = x
    for f in range(win_ref.shape[1] // tf):
        hid = jnp.maximum(_dot(xn, win_ref[:, f * tf:(f + 1) * tf]), 0.0)
        acc = acc + _dot((hid * hid).astype(BF16), wout_ref[f * tf:(f + 1) * tf, :])
    o_ref[...] = acc


def _mlp(x, g, w_in, w_out, *, tm, tf):
    M, D = x.shape
    tile = pl.BlockSpec((tm, D), lambda i: (i, 0))
    return pl.pallas_call(
        functools.partial(_mlp_kernel, tf=tf),
        grid=(M // tm,),
        in_specs=[tile, _const_spec((1, D)), _const_spec(w_in.shape), _const_spec(w_out.shape)],
        out_specs=tile,
        out_shape=jax.ShapeDtypeStruct((M, D), F32),
        compiler_params=_params(("parallel",)),
        name="sq_relu_mlp",
    )(x, g.reshape(1, D), w_in, w_out)


def _q_proj_kernel(x_ref, g_ref, wt_ref, qg_ref, c_ref, q_out, *, scale):
    hn = _rms(x_ref[0], g_ref[...])
    qt = _dot(wt_ref[...], hn.T.astype(BF16))
    tm = hn.shape[0]
    row = lax.broadcasted_iota(jnp.int32, (HEAD_DIM, tm), 0)
    for h in range(qt.shape[0] // HEAD_DIM):
        hs = slice(h * HEAD_DIM, (h + 1) * HEAD_DIM)
        qh = qt[hs, :]
        ms = jnp.mean(qh * qh, axis=0, keepdims=True)
        qn = qh * lax.rsqrt(ms + NORM_EPS) * (qg_ref[hs, :] * scale)
        hi, mid, lo = _split3(c_ref[0, h:h + 1, :] * LOG2E)
        aug = jnp.where(row == 0, hi, jnp.where(row == 1, mid, jnp.where(
            row == 2, lo, jnp.where(row < 6, 1.0, 0.0))))
        q_out[0, h] = jnp.concatenate([qn, aug], axis=0).astype(q_out.dtype)


def _q_proj(x, g, wt, qg, c_row, *, tm, scale):
    B, T, D = x.shape
    H = D // HEAD_DIM
    return pl.pallas_call(
        functools.partial(_q_proj_kernel, scale=scale),
        grid=(B, T // tm),
        in_specs=[pl.BlockSpec((1, tm, D), lambda b, i: (b, i, 0)), _const_spec((1, D)),
                  _const_spec(wt.shape), _const_spec((D, 1)),
                  pl.BlockSpec((1, H, tm), lambda b, i: (b, 0, i))],
        out_specs=pl.BlockSpec((1, H, LANES, tm), lambda b, i: (b, 0, 0, i)),
        out_shape=jax.ShapeDtypeStruct((B, H, LANES, T), BF16),
        compiler_params=_params(("parallel", "parallel")),
        name="q_proj",
    )(x, g.reshape(1, D), wt, qg.reshape(D, 1), c_row)


def _shared_kv_kernel(x_ref, g_ref, wk_ref, wvt_ref, wf_ref, fb_ref, kg_ref,
                      k_out, vt_out, c_out, carry_scr):
    i = pl.program_id(1)

    @pl.when(i == 0)
    def _():
        carry_scr[...] = jnp.zeros_like(carry_scr)

    hn = _rms(x_ref[0], g_ref[...])
    hb = hn.astype(BF16)
    tm, D = hn.shape
    H = D // HEAD_DIM
    k = _head_rms(_dot(hb, wk_ref[...]), kg_ref[...])
    vt = _dot(wvt_ref[...], hn.T.astype(BF16))

    f = _dot(hb, wf_ref[...]) + fb_ref[...]
    logf = jnp.minimum(f, 0.0) - jnp.log(1.0 + jnp.exp(-jnp.abs(f)))
    t_i = lax.broadcasted_iota(jnp.int32, (tm, tm), 0)
    j_i = lax.broadcasted_iota(jnp.int32, (tm, tm), 1)
    ltri = (j_i <= t_i).astype(BF16)
    c3 = _dot(ltri, jnp.concatenate([t.astype(BF16) for t in _split3(logf)], axis=1))
    c = (c3[:, :LANES] + c3[:, LANES:2 * LANES]) + c3[:, 2 * LANES:] + carry_scr[0:1, :]
    carry_scr[...] = jnp.broadcast_to(c[tm - 1:tm, :], carry_scr.shape)
    c_out[0] = c.T[:H, :]

    nck = c * (-LOG2E)
    lane = lax.broadcasted_iota(jnp.int32, (tm, LANES), 1)
    vrow = lax.broadcasted_iota(jnp.int32, (HEAD_DIM, tm), 0)
    ones_row = jnp.where(vrow == 0, 1.0, 0.0)
    for h in range(H):
        base = k[:, (h // 2) * LANES:(h // 2 + 1) * LANES]
        if h % 2:
            base = pltpu.roll(base, HEAD_DIM, 1)
        hi, mid, lo = _split3(nck[:, h:h + 1])
        tile = jnp.where(lane < HEAD_DIM, base, jnp.where(lane < HEAD_DIM + 3, 1.0, jnp.where(
            lane == HEAD_DIM + 3, hi, jnp.where(lane == HEAD_DIM + 4, mid, jnp.where(
                lane == HEAD_DIM + 5, lo, 0.0)))))
        k_out[0, h] = tile.astype(k_out.dtype)
        vt_out[0, h, 0] = jnp.concatenate(
            [vt[h * HEAD_DIM:(h + 1) * HEAD_DIM, :], ones_row], axis=0).astype(vt_out.dtype)


def _shared_kv(x, g, wk, wvt, wf, fb, kg, *, tm):
    B, T, D = x.shape
    H = D // HEAD_DIM
    return pl.pallas_call(
        _shared_kv_kernel,
        grid=(B, T // tm),
        in_specs=[pl.BlockSpec((1, tm, D), lambda b, i: (b, i, 0)), _const_spec((1, D)),
                  _const_spec(wk.shape), _const_spec(wvt.shape), _const_spec(wf.shape),
                  _const_spec((1, LANES)), _const_spec((1, D))],
        out_specs=[pl.BlockSpec((1, H, tm, LANES), lambda b, i: (b, 0, i, 0)),
                   pl.BlockSpec((1, H, 1, LANES, tm), lambda b, i: (b, 0, i, 0, 0)),
                   pl.BlockSpec((1, H, tm), lambda b, i: (b, 0, i))],
        out_shape=[jax.ShapeDtypeStruct((B, H, T, LANES), BF16),
                   jax.ShapeDtypeStruct((B, H, T // tm, LANES, tm), BF16),
                   jax.ShapeDtypeStruct((B, H, T), F32)],
        scratch_shapes=[pltpu.VMEM((8, LANES), F32)],
        compiler_params=_params(("parallel", "arbitrary")),
        name="shared_kv",
    )(x, g.reshape(1, D), wk, wvt, wf, fb, kg.reshape(1, D))


def _fox_attn_kernel(q_ref, k_ref, vt_ref, o_ref, acc_scr, *, tq, nh):
    i = pl.program_id(2)
    qt = [q_ref[0, h] for h in range(nh)]
    acc_scr[...] = jnp.zeros_like(acc_scr)
    key_i = lax.broadcasted_iota(jnp.int32, (tq, tq), 0)
    qry_i = lax.broadcasted_iota(jnp.int32, (tq, tq), 1)
    causal = key_i <= qry_i

    def step(j, m, masked):
        off = pl.multiple_of(j * tq, tq)
        s = [_dot(k_ref[0, h, pl.ds(off, tq), :], qt[h]) for h in range(nh)]
        if masked:
            s = [jnp.where(causal, s[h], NEG_BIG) for h in range(nh)]
        m_new = [jnp.maximum(m[h], jnp.max(s[h], axis=0, keepdims=True)) for h in range(nh)]
        p = [jnp.exp2(s[h] - m_new[h]).astype(BF16) for h in range(nh)]
        alpha = [jnp.exp2(m[h] - m_new[h]) for h in range(nh)]
        pv = [_dot(vt_ref[0, h, j], p[h]) for h in range(nh)]
        for h in range(nh):
            acc_scr[h] = alpha[h] * acc_scr[h] + pv[h]
        return tuple(m_new)

    m0 = tuple(jnp.full((1, tq), NEG_BIG, F32) for _ in range(nh))
    m = lax.fori_loop(0, i, functools.partial(step, masked=False), m0)
    step(i, m, True)
    ot = [acc_scr[h, :HEAD_DIM, :] * (1.0 / acc_scr[h, HEAD_DIM:HEAD_DIM + 1, :]) for h in range(nh)]
    o_ref[0] = jnp.concatenate(ot, axis=0).T.astype(o_ref.dtype)


def _fox_attn(qt, ka, vt, *, tq, nh):
    B, H, _, T = qt.shape
    nb = T // tq
    return pl.pallas_call(
        functools.partial(_fox_attn_kernel, tq=tq, nh=nh),
        grid=(B, H // nh, nb),
        in_specs=[pl.BlockSpec((1, nh, LANES, tq), lambda b, p, i: (b, p, 0, i)),
                  pl.BlockSpec((1, nh, T, LANES), lambda b, p, i: (b, p, 0, 0)),
                  pl.BlockSpec((1, nh, nb, LANES, tq), lambda b, p, i: (b, p, 0, 0, 0))],
        out_specs=pl.BlockSpec((1, tq, nh * HEAD_DIM), lambda b, p, i: (b, i, p)),
        out_shape=jax.ShapeDtypeStruct((B, T, H * HEAD_DIM), BF16),
        scratch_shapes=[pltpu.VMEM((nh, LANES, tq), F32)],
        compiler_params=_params(("parallel", "parallel", "arbitrary")),
        name="fox_attn",
    )(qt, ka, vt)


def kernel(x, rwkv_norm_g, rwkv_mu, rwkv_w_rkv, rwkv_w0, rwkv_w1, rwkv_w2, rwkv_a0, rwkv_a1, rwkv_a2, rwkv_g1, rwkv_g2, rwkv_k_k, rwkv_k_a, rwkv_r_k, rwkv_lnx_w, rwkv_lnx_b, rwkv_w_o, kv_norm_g, kv_w, kv_f_bias, k_norm_g, attn_norm_g, attn_w_q, q_norm_g, attn_w_o, mlp_norm_g, mlp_w_in, mlp_w_out):
    B, T, D = x.shape
    M = B * T
    n_a = rwkv_norm_g.shape[0]
    depth = mlp_norm_g.shape[0]
    bf = lambda w: w.astype(BF16)
    tm_prep = min(256, T)
    tm = min(512, T)
    tq = min(256, T)
    tf = min(1024, mlp_w_in.shape[-1])
    npairs = D // LANES

    k_sh = v_sh = c_sh = None
    for layer in range(depth):
        if layer < n_a:
            i = layer
            r, lw, k, v, kk, b, g = _rwkv_prep(
                x, rwkv_norm_g[i], rwkv_mu[i], bf(rwkv_w_rkv[i, 0]), bf(rwkv_w_rkv[i, 1]),
                bf(rwkv_w_rkv[i, 2]), rwkv_w0[i], bf(rwkv_w1[i]), bf(rwkv_w2[i]), rwkv_a0[i],
                bf(rwkv_a1[i]), bf(rwkv_a2[i]), bf(rwkv_g1[i]), bf(rwkv_g2[i]),
                rwkv_k_k[i], rwkv_k_a[i], tm=tm_prep)
            y = _rwkv_scan(r, lw, k, v, kk, b, g, rwkv_r_k[i], rwkv_lnx_w[i], rwkv_lnx_b[i],
                           npairs=npairs)
            x = _proj_res(x.reshape(M, D), y.reshape(M, D), bf(rwkv_w_o[i]), tm=tm).reshape(B, T, D)
        else:
            j = layer - n_a
            qt = _q_proj(x, attn_norm_g[j], bf(attn_w_q[j].T), q_norm_g[j], c_sh,
                         tm=tq, scale=HEAD_DIM ** -0.5 * LOG2E)
            o = _fox_attn(qt, k_sh, v_sh, tq=tq, nh=min(8, D // HEAD_DIM))
            x = _proj_res(x.reshape(M, D), o.reshape(M, D), bf(attn_w_o[j]), tm=tm).reshape(B, T, D)
        x = _mlp(x.reshape(M, D), mlp_norm_g[layer], bf(mlp_w_in[layer]), bf(mlp_w_out[layer]),
                 tm=tm, tf=tf).reshape(B, T, D)
        if layer == n_a - 1:
            wf = jnp.pad(kv_w[:, 2 * D:], ((0, 0), (0, LANES - (kv_w.shape[1] - 2 * D))))
            fb = jnp.pad(kv_f_bias, (0, LANES - kv_f_bias.shape[0])).reshape(1, LANES)
            k_sh, v_sh, c_sh = _shared_kv(
                x, kv_norm_g, bf(kv_w[:, :D]), bf(kv_w[:, D:2 * D].T), bf(wf), fb, k_norm_g, tm=tq)
    return x
```

```python
import functools

import jax
import jax.numpy as jnp
from jax import lax
from jax.experimental import pallas as pl
from jax.experimental.pallas import tpu as pltpu

HEAD_DIM = 64
LANES = 128
NORM_EPS = 1e-6
GN_EPS = 64e-5
CHUNK = 64
NEG_BIG = -1e30
LOG2E = 1.4426950408889634
VMEM_LIMIT = 56 * 1024 * 1024

BF16 = jnp.bfloat16
F32 = jnp.float32

_NT = (((1,), (1,)), ((), ()))
_TN = (((0,), (0,)), ((), ()))


def _dot(a, b):
    return jnp.dot(a, b, preferred_element_type=F32)


def _dot_nt(a, b):
    return lax.dot_general(a, b, _NT, preferred_element_type=F32)


def _dot_tn(a, b):
    return lax.dot_general(a, b, _TN, preferred_element_type=F32)


def _split(a):
    hi = a.astype(BF16)
    return hi, (a - hi.astype(F32)).astype(BF16)


def _split3(a):
    hi = a.astype(BF16).astype(F32)
    r1 = a - hi
    mid = r1.astype(BF16).astype(F32)
    return hi, mid, r1 - mid


def _rms(x, g):
    return x * lax.rsqrt(jnp.mean(x * x, axis=-1, keepdims=True) + NORM_EPS) * g


def _head_sum(x):
    outs = []
    for c in range(x.shape[1] // LANES):
        xc = x[:, c * LANES:(c + 1) * LANES]
        lo = lax.broadcasted_iota(jnp.int32, xc.shape, 1) < HEAD_DIM
        s0 = jnp.sum(jnp.where(lo, xc, 0.0), axis=1, keepdims=True)
        s1 = jnp.sum(jnp.where(lo, 0.0, xc), axis=1, keepdims=True)
        outs.append(jnp.where(lo, s0, s1))
    return outs[0] if len(outs) == 1 else jnp.concatenate(outs, axis=1)


def _head_rms(t, g):
    ms = _head_sum(t * t) * (1.0 / HEAD_DIM)
    return t * lax.rsqrt(ms + NORM_EPS) * g


def _sigmoid(z):
    return 1.0 / (1.0 + jnp.exp(-z))


def _softplus(z):
    return jnp.maximum(z, 0.0) + jnp.log(1.0 + jnp.exp(-jnp.abs(z)))


def _const_spec(shape):
    nd = len(shape)
    return pl.BlockSpec(shape, lambda *_: (0,) * nd)


def _params(sem):
    return pltpu.CompilerParams(dimension_semantics=sem, vmem_limit_bytes=VMEM_LIMIT)


def _rwkv_prep_kernel(x_ref, xp_ref, ng_ref, mu_ref, wr_ref, wk_ref, wv_ref,
                      w0_ref, w1_ref, w2_ref, a0_ref, a1_ref, a2_ref, g1_ref, g2_ref,
                      kkw_ref, kaw_ref,
                      r_out, lw_out, k_out, v_out, kk_out, b_out, g_out):
    i = pl.program_id(1)
    ng = ng_ref[...]
    h = _rms(x_ref[0], ng)
    hp = _rms(xp_ref[0][7:8, :], ng)
    hp = jnp.where(i > 0, hp, 0.0)
    row = lax.broadcasted_iota(jnp.int32, h.shape, 0)
    hs = jnp.where(row == 0, hp, pltpu.roll(h, 1, 0))
    xx = hs - h
    mu = mu_ref[...]

    def mix(j):
        return (h + xx * mu[j:j + 1, :]).astype(BF16)

    r = _dot(mix(0), wr_ref[...])
    wl = _dot(jnp.tanh(_dot(mix(1), w1_ref[...])).astype(BF16), w2_ref[...])
    k = _dot(mix(2), wk_ref[...])
    v = _dot(mix(3), wv_ref[...])
    al = _dot(_dot(mix(4), a1_ref[...]).astype(BF16), a2_ref[...])
    g = _dot(_sigmoid(_dot(mix(5), g1_ref[...])).astype(BF16), g2_ref[...])

    w = -_softplus(-(w0_ref[...] + wl)) - 0.5
    a = _sigmoid(a0_ref[...] + al)
    kk = k * kkw_ref[...]
    nrm = jnp.sqrt(_head_sum(kk * kk))
    kk = kk / jnp.maximum(nrm, 1e-12)

    r_out[0] = r
    lw_out[0] = -jnp.exp(w)
    k_out[0] = k * (1.0 + (a - 1.0) * kaw_ref[...])
    v_out[0] = v
    kk_out[0] = kk
    b_out[0] = kk * a
    g_out[0] = g


def _rwkv_prep(x, ng, mu, wr, wk, wv, w0, w1, w2, a0, a1, a2, g1, g2, kkw, kaw, *, tm):
    B, T, D = x.shape
    row = lambda a: a.reshape(1, D)
    consts = [row(ng), mu, wr, wk, wv, row(w0), w1, w2, row(a0), a1, a2, g1, g2, row(kkw), row(kaw)]
    tile = pl.BlockSpec((1, tm, D), lambda b, i: (b, i, 0))
    prev = pl.BlockSpec((1, 8, D), lambda b, i: (b, jnp.maximum(i * (tm // 8) - 1, 0), 0))
    out = jax.ShapeDtypeStruct((B, T, D), F32)
    return pl.pallas_call(
        _rwkv_prep_kernel,
        grid=(B, T // tm),
        in_specs=[tile, prev] + [_const_spec(c.shape) for c in consts],
        out_specs=[tile] * 7,
        out_shape=[out] * 7,
        compiler_params=_params(("parallel", "parallel")),
        name="rwkv_prep",
    )(x, x, *consts)


def _blockdiag(z, lo):
    z = z.astype(BF16)
    zero = jnp.zeros_like(z)
    return jnp.concatenate([jnp.where(lo, z, zero), jnp.where(lo, zero, z)], axis=0)


def _rwkv_scan_kernel(r_ref, lw_ref, k_ref, v_ref, kk_ref, b_ref, g_ref,
                      rk_ref, lnw_ref, lnb_ref, y_out, s_scr, *, nbatch, npairs):
    c = pl.program_id(2)

    @pl.when(c == 0)
    def _():
        s_scr[...] = jnp.zeros_like(s_scr)

    C = CHUNK
    t_i = lax.broadcasted_iota(jnp.int32, (C, C), 0)
    j_i = lax.broadcasted_iota(jnp.int32, (C, C), 1)
    ltri = (j_i <= t_i).astype(BF16)
    row = lax.broadcasted_iota(jnp.int32, (C, LANES), 0)
    lane = lax.broadcasted_iota(jnp.int32, (C, LANES), 1)
    lo = lane < HEAD_DIM
    col = jnp.bitwise_and(lane, HEAD_DIM - 1)
    strict = col < row
    incl = col <= row
    rr = lax.broadcasted_iota(jnp.int32, (LANES, LANES), 0)
    cc = lax.broadcasted_iota(jnp.int32, (LANES, LANES), 1)
    same_head = (rr < HEAD_DIM) == (cc < HEAD_DIM)
    bd = functools.partial(_blockdiag, lo=lo)
    cat0 = lambda *xs: jnp.concatenate([x.astype(BF16) for x in xs], axis=0)
    cat1 = lambda *xs: jnp.concatenate([x.astype(BF16) for x in xs], axis=1)

    units = [(bi, slice(p * LANES, (p + 1) * LANES)) for bi in range(nbatch) for p in range(npairs)]
    P = range(len(units))
    r = [r_ref[bi, :, sl] for bi, sl in units]
    lw = [lw_ref[bi, :, sl] for bi, sl in units]
    k = [k_ref[bi, :, sl] for bi, sl in units]
    v = [v_ref[bi, :, sl] for bi, sl in units]
    kk = [kk_ref[bi, :, sl] for bi, sl in units]
    b = [b_ref[bi, :, sl] for bi, sl in units]

    cw2 = [_dot(ltri, cat1(*_split(lw[p]))) for p in P]
    cw = [cw2[p][:, :LANES] + cw2[p][:, LANES:] for p in P]
    cwl = [cw[p][C - 1:C, :] for p in P]
    at = [-kk[p] * jnp.exp(cw[p] - lw[p]) for p in P]
    dinv = [jnp.exp(-cw[p]) for p in P]
    rt = [r[p] * jnp.exp(cw[p]) for p in P]
    dend = [jnp.exp(cwl[p] - cw[p]) for p in P]

    x = [_dot_nt(cat0(at[p], rt[p]), cat0(bd(b[p] * dinv[p]), bd(k[p] * dinv[p]))) for p in P]
    aab = [jnp.where(strict, x[p][:C, :LANES], 0.0) for p in P]
    arb = [jnp.where(incl, x[p][C:, :LANES], 0.0) for p in P]
    aak = [jnp.where(strict, x[p][:C, LANES:], 0.0) for p in P]
    ark = [jnp.where(incl, x[p][C:, LANES:], 0.0) for p in P]

    bdv = [bd(v[p]) for p in P]
    av = [_dot(aak[p].astype(BF16), bdv[p]) for p in P]
    eye = jnp.where(col == row, 1.0, 0.0)
    tinv = [eye + aab[p] for p in P]
    n = [_dot(aab[p].astype(BF16), bd(aab[p])) for p in P]
    for it in range(5):
        last = it == 4
        res = [_dot(n[p].astype(BF16),
                    jnp.concatenate([bd(tinv[p])] + ([] if last else [bd(n[p])]), axis=1)) for p in P]
        tinv = [tinv[p] + res[p][:, :LANES] for p in P]
        if not last:
            n = [res[p][:, LANES:] for p in P]
    z = [_dot(tinv[p].astype(BF16), jnp.concatenate([bd(at[p]), bd(av[p])], axis=1)) for p in P]
    z1 = [z[p][:, :LANES] for p in P]
    z2 = [z[p][:, LANES:] for p in P]

    s = [s_scr[p] for p in P]
    ws = [_dot_nt(cat0(z1[p], rt[p]), s[p].astype(BF16)) for p in P]
    u = [ws[p][:C] + z2[p] for p in P]
    y = [ws[p][C:] + _dot(cat1(arb[p], ark[p]), cat0(bd(u[p]), bdv[p])) for p in P]
    upd = [_dot_tn(cat0(u[p], v[p]), cat0(b[p] * dend[p], k[p] * dend[p])) for p in P]
    for p in P:
        s_scr[p] = s[p] * jnp.exp(cwl[p]) + jnp.where(same_head, upd[p], 0.0)

    for p in P:
        bi, sl = units[p]
        mean = _head_sum(y[p]) * (1.0 / HEAD_DIM)
        d = y[p] - mean
        var = _head_sum(d * d) * (1.0 / HEAD_DIM)
        yn = d * lax.rsqrt(var + GN_EPS)
        bonus = _head_sum(r[p] * k[p] * rk_ref[:, sl]) * v[p]
        out = (yn * lnw_ref[:, sl] + lnb_ref[:, sl] + bonus) * g_ref[bi, :, sl]
        y_out[bi, :, sl] = out.astype(y_out.dtype)


def _rwkv_scan(r, lw, k, v, kk, b, g, rk, lnw, lnb, *, nbatch, npairs):
    B, T, D = r.shape
    W = npairs * LANES
    tile = pl.BlockSpec((nbatch, CHUNK, W), lambda bi, p, c: (bi, c, p))
    vec = pl.BlockSpec((1, W), lambda bi, p, c: (0, p))
    return pl.pallas_call(
        functools.partial(_rwkv_scan_kernel, nbatch=nbatch, npairs=npairs),
        grid=(B // nbatch, D // W, T // CHUNK),
        in_specs=[tile] * 7 + [vec] * 3,
        out_specs=tile,
        out_shape=jax.ShapeDtypeStruct((B, T, D), BF16),
        scratch_shapes=[pltpu.VMEM((nbatch * npairs, LANES, LANES), F32)],
        compiler_params=_params(("parallel", "parallel", "arbitrary")),
        name="rwkv_scan",
    )(r, lw, k, v, kk, b, g, rk.reshape(1, D), lnw.reshape(1, D), lnb.reshape(1, D))


def _proj_mlp_kernel(res_ref, a_ref, wo_ref, g_ref, win_ref, wout_ref, o_ref, *, tf):
    x = res_ref[...] + _dot(a_ref[...], wo_ref[...])
    xn = _rms(x, g_ref[...]).astype(BF16)
    acc = x
    for f in range(win_ref.shape[1] // tf):
        hid = jnp.maximum(_dot(xn, win_ref[:, f * tf:(f + 1) * tf]), 0.0)
        acc = acc + _dot((hid * hid).astype(BF16), wout_ref[f * tf:(f + 1) * tf, :])
    o_ref[...] = acc


def _proj_mlp(res, a, w_o, g, w_in, w_out, *, tm, tf):
    M, D = res.shape
    tile = pl.BlockSpec((tm, D), lambda i: (i, 0))
    return pl.pallas_call(
        functools.partial(_proj_mlp_kernel, tf=tf),
        grid=(M // tm,),
        in_specs=[tile, tile, _const_spec(w_o.shape), _const_spec((1, D)),
                  _const_spec(w_in.shape), _const_spec(w_out.shape)],
        out_specs=tile,
        out_shape=jax.ShapeDtypeStruct((M, D), F32),
        compiler_params=_params(("parallel",)),
        name="proj_mlp",
    )(res, a, w_o, g.reshape(1, D), w_in, w_out)


def _q_proj_kernel(x_ref, g_ref, wt_ref, qg_ref, c_ref, q_out, *, scale):
    hn = _rms(x_ref[0], g_ref[...])
    qt = _dot(wt_ref[...], hn.T.astype(BF16))
    tm = hn.shape[0]
    row = lax.broadcasted_iota(jnp.int32, (HEAD_DIM, tm), 0)
    for h in range(qt.shape[0] // HEAD_DIM):
        hs = slice(h * HEAD_DIM, (h + 1) * HEAD_DIM)
        qh = qt[hs, :]
        ms = jnp.mean(qh * qh, axis=0, keepdims=True)
        qn = qh * lax.rsqrt(ms + NORM_EPS) * (qg_ref[hs, :] * scale)
        hi, mid, lo = _split3(c_ref[0, h:h + 1, :] * LOG2E)
        aug = jnp.where(row == 0, hi, jnp.where(row == 1, mid, jnp.where(
            row == 2, lo, jnp.where(row < 6, 1.0, 0.0))))
        q_out[0, h] = jnp.concatenate([qn, aug], axis=0).astype(q_out.dtype)


def _q_proj(x, g, wt, qg, c_row, *, tm, scale):
    B, T, D = x.shape
    H = D // HEAD_DIM
    return pl.pallas_call(
        functools.partial(_q_proj_kernel, scale=scale),
        grid=(B, T // tm),
        in_specs=[pl.BlockSpec((1, tm, D), lambda b, i: (b, i, 0)), _const_spec((1, D)),
                  _const_spec(wt.shape), _const_spec((D, 1)),
                  pl.BlockSpec((1, H, tm), lambda b, i: (b, 0, i))],
        out_specs=pl.BlockSpec((1, H, LANES, tm), lambda b, i: (b, 0, 0, i)),
        out_shape=jax.ShapeDtypeStruct((B, H, LANES, T), BF16),
        compiler_params=_params(("parallel", "parallel")),
        name="q_proj",
    )(x, g.reshape(1, D), wt, qg.reshape(D, 1), c_row)


def _shared_kv_kernel(x_ref, g_ref, wk_ref, wvt_ref, wf_ref, fb_ref, kg_ref,
                      k_out, vt_out, c_out, carry_scr):
    i = pl.program_id(1)

    @pl.when(i == 0)
    def _():
        carry_scr[...] = jnp.zeros_like(carry_scr)

    hn = _rms(x_ref[0], g_ref[...])
    hb = hn.astype(BF16)
    tm, D = hn.shape
    H = D // HEAD_DIM
    k = _head_rms(_dot(hb, wk_ref[...]), kg_ref[...])
    vt = _dot(wvt_ref[...], hn.T.astype(BF16))

    f = _dot(hb, wf_ref[...]) + fb_ref[...]
    logf = jnp.minimum(f, 0.0) - jnp.log(1.0 + jnp.exp(-jnp.abs(f)))
    t_i = lax.broadcasted_iota(jnp.int32, (tm, tm), 0)
    j_i = lax.broadcasted_iota(jnp.int32, (tm, tm), 1)
    ltri = (j_i <= t_i).astype(BF16)
    c3 = _dot(ltri, jnp.concatenate([t.astype(BF16) for t in _split3(logf)], axis=1))
    c = (c3[:, :LANES] + c3[:, LANES:2 * LANES]) + c3[:, 2 * LANES:] + carry_scr[0:1, :]
    carry_scr[...] = jnp.broadcast_to(c[tm - 1:tm, :], carry_scr.shape)
    c_out[0] = c.T[:H, :]

    nck = c * (-LOG2E)
    lane = lax.broadcasted_iota(jnp.int32, (tm, LANES), 1)
    vrow = lax.broadcasted_iota(jnp.int32, (HEAD_DIM, tm), 0)
    ones_row = jnp.where(vrow == 0, 1.0, 0.0)
    for h in range(H):
        base = k[:, (h // 2) * LANES:(h // 2 + 1) * LANES]
        if h % 2:
            base = pltpu.roll(base, HEAD_DIM, 1)
        hi, mid, lo = _split3(nck[:, h:h + 1])
        tile = jnp.where(lane < HEAD_DIM, base, jnp.where(lane < HEAD_DIM + 3, 1.0, jnp.where(
            lane == HEAD_DIM + 3, hi, jnp.where(lane == HEAD_DIM + 4, mid, jnp.where(
                lane == HEAD_DIM + 5, lo, 0.0)))))
        k_out[0, h] = tile.astype(k_out.dtype)
        vt_out[0, h, 0] = jnp.concatenate(
            [vt[h * HEAD_DIM:(h + 1) * HEAD_DIM, :], ones_row], axis=0).astype(vt_out.dtype)


def _shared_kv(x, g, wk, wvt, wf, fb, kg, *, tm):
    B, T, D = x.shape
    H = D // HEAD_DIM
    return pl.pallas_call(
        _shared_kv_kernel,
        grid=(B, T // tm),
        in_specs=[pl.BlockSpec((1, tm, D), lambda b, i: (b, i, 0)), _const_spec((1, D)),
                  _const_spec(wk.shape), _const_spec(wvt.shape), _const_spec(wf.shape),
                  _const_spec((1, LANES)), _const_spec((1, D))],
        out_specs=[pl.BlockSpec((1, H, tm, LANES), lambda b, i: (b, 0, i, 0)),
                   pl.BlockSpec((1, H, 1, LANES, tm), lambda b, i: (b, 0, i, 0, 0)),
                   pl.BlockSpec((1, H, tm), lambda b, i: (b, 0, i))],
        out_shape=[jax.ShapeDtypeStruct((B, H, T, LANES), BF16),
                   jax.ShapeDtypeStruct((B, H, T // tm, LANES, tm), BF16),
                   jax.ShapeDtypeStruct((B, H, T), F32)],
        scratch_shapes=[pltpu.VMEM((8, LANES), F32)],
        compiler_params=_params(("parallel", "arbitrary")),
        name="shared_kv",
    )(x, g.reshape(1, D), wk, wvt, wf, fb, kg.reshape(1, D))


def _fox_attn_kernel(q_ref, k_ref, vt_ref, o_ref, acc_scr, *, tq, nh):
    i = pl.program_id(2)
    qt = [q_ref[0, h] for h in range(nh)]
    acc_scr[...] = jnp.zeros_like(acc_scr)
    key_i = lax.broadcasted_iota(jnp.int32, (tq, tq), 0)
    qry_i = lax.broadcasted_iota(jnp.int32, (tq, tq), 1)
    causal = key_i <= qry_i

    def step(j, m, masked):
        off = pl.multiple_of(j * tq, tq)
        s = [_dot(k_ref[0, h, pl.ds(off, tq), :], qt[h]) for h in range(nh)]
        if masked:
            s = [jnp.where(causal, s[h], NEG_BIG) for h in range(nh)]
        m_new = [jnp.maximum(m[h], jnp.max(s[h], axis=0, keepdims=True)) for h in range(nh)]
        p = [jnp.exp2(s[h] - m_new[h]).astype(BF16) for h in range(nh)]
        alpha = [jnp.exp2(m[h] - m_new[h]) for h in range(nh)]
        pv = [_dot(vt_ref[0, h, j], p[h]) for h in range(nh)]
        for h in range(nh):
            acc_scr[h] = alpha[h] * acc_scr[h] + pv[h]
        return tuple(m_new)

    m0 = tuple(jnp.full((1, tq), NEG_BIG, F32) for _ in range(nh))
    m = lax.fori_loop(0, i, functools.partial(step, masked=False), m0)
    step(i, m, True)
    ot = [acc_scr[h, :HEAD_DIM, :] * (1.0 / acc_scr[h, HEAD_DIM:HEAD_DIM + 1, :]) for h in range(nh)]
    o_ref[0] = jnp.concatenate(ot, axis=0).T.astype(o_ref.dtype)


def _fox_attn(qt, ka, vt, *, tq, nh):
    B, H, _, T = qt.shape
    nb = T // tq
    return pl.pallas_call(
        functools.partial(_fox_attn_kernel, tq=tq, nh=nh),
        grid=(B, H // nh, nb),
        in_specs=[pl.BlockSpec((1, nh, LANES, tq), lambda b, p, i: (b, p, 0, i)),
                  pl.BlockSpec((1, nh, T, LANES), lambda b, p, i: (b, p, 0, 0)),
                  pl.BlockSpec((1, nh, nb, LANES, tq), lambda b, p, i: (b, p, 0, 0, 0))],
        out_specs=pl.BlockSpec((1, tq, nh * HEAD_DIM), lambda b, p, i: (b, i, p)),
        out_shape=jax.ShapeDtypeStruct((B, T, H * HEAD_DIM), BF16),
        scratch_shapes=[pltpu.VMEM((nh, LANES, tq), F32)],
        compiler_params=_params(("parallel", "parallel", "arbitrary")),
        name="fox_attn",
    )(qt, ka, vt)


def kernel(x, rwkv_norm_g, rwkv_mu, rwkv_w_rkv, rwkv_w0, rwkv_w1, rwkv_w2, rwkv_a0, rwkv_a1, rwkv_a2, rwkv_g1, rwkv_g2, rwkv_k_k, rwkv_k_a, rwkv_r_k, rwkv_lnx_w, rwkv_lnx_b, rwkv_w_o, kv_norm_g, kv_w, kv_f_bias, k_norm_g, attn_norm_g, attn_w_q, q_norm_g, attn_w_o, mlp_norm_g, mlp_w_in, mlp_w_out):
    B, T, D = x.shape
    M = B * T
    n_a = rwkv_norm_g.shape[0]
    depth = mlp_norm_g.shape[0]
    bf = lambda w: w.astype(BF16)
    tm_prep = min(256, T)
    tm = min(512, T)
    tq = min(256, T)
    tf = min(1024, mlp_w_in.shape[-1])
    npairs = D // LANES

    k_sh = v_sh = c_sh = None
    for layer in range(depth):
        if layer < n_a:
            i = layer
            r, lw, k, v, kk, b, g = _rwkv_prep(
                x, rwkv_norm_g[i], rwkv_mu[i], bf(rwkv_w_rkv[i, 0]), bf(rwkv_w_rkv[i, 1]),
                bf(rwkv_w_rkv[i, 2]), rwkv_w0[i], bf(rwkv_w1[i]), bf(rwkv_w2[i]), rwkv_a0[i],
                bf(rwkv_a1[i]), bf(rwkv_a2[i]), bf(rwkv_g1[i]), bf(rwkv_g2[i]),
                rwkv_k_k[i], rwkv_k_a[i], tm=tm_prep)
            mix = _rwkv_scan(r, lw, k, v, kk, b, g, rwkv_r_k[i], rwkv_lnx_w[i], rwkv_lnx_b[i],
                             nbatch=2 if B % 2 == 0 else 1, npairs=npairs)
            w_o = rwkv_w_o[i]
        else:
            j = layer - n_a
            qt = _q_proj(x, attn_norm_g[j], bf(attn_w_q[j].T), q_norm_g[j], c_sh,
                         tm=tq, scale=HEAD_DIM ** -0.5 * LOG2E)
            mix = _fox_attn(qt, k_sh, v_sh, tq=tq, nh=min(8, D // HEAD_DIM))
            w_o = attn_w_o[j]
        x = _proj_mlp(x.reshape(M, D), mix.reshape(M, D), bf(w_o), mlp_norm_g[layer],
                      bf(mlp_w_in[layer]), bf(mlp_w_out[layer]), tm=tm, tf=tf).reshape(B, T, D)
        if layer == n_a - 1:
            wf = jnp.pad(kv_w[:, 2 * D:], ((0, 0), (0, LANES - (kv_w.shape[1] - 2 * D))))
            fb = jnp.pad(kv_f_bias, (0, LANES - kv_f_bias.shape[0])).reshape(1, LANES)
            k_sh, v_sh, c_sh = _shared_kv(
                x, kv_norm_g, bf(kv_w[:, :D]), bf(kv_w[:, D:2 * D].T), bf(wf), fb, k_norm_g, tm=tq)
    return x
```

```python
import functools

import jax
import jax.numpy as jnp
from jax import lax
from jax.experimental import pallas as pl
from jax.experimental.pallas import tpu as pltpu

HEAD_DIM = 64
LANES = 128
NORM_EPS = 1e-6
GN_EPS = 64e-5
CHUNK = 64
NEG_BIG = -1e30
LOG2E = 1.4426950408889634
EXP_M_HALF = 0.6065306597126334
VMEM_LIMIT = 56 * 1024 * 1024

BF16 = jnp.bfloat16
F32 = jnp.float32

_NT = (((1,), (1,)), ((), ()))
_TN = (((0,), (0,)), ((), ()))


def _dot(a, b):
    return jnp.dot(a, b, preferred_element_type=F32)


def _dot_nt(a, b):
    return lax.dot_general(a, b, _NT, preferred_element_type=F32)


def _dot_tn(a, b):
    return lax.dot_general(a, b, _TN, preferred_element_type=F32)


def _split(a):
    hi = a.astype(BF16)
    return hi, (a - hi.astype(F32)).astype(BF16)


def _split3(a):
    hi = a.astype(BF16).astype(F32)
    r1 = a - hi
    mid = r1.astype(BF16).astype(F32)
    return hi, mid, r1 - mid


def _rms(x, g):
    return x * lax.rsqrt(jnp.mean(x * x, axis=-1, keepdims=True) + NORM_EPS) * g


def _head_sum(x):
    outs = []
    for c in range(x.shape[1] // LANES):
        xc = x[:, c * LANES:(c + 1) * LANES]
        lo = lax.broadcasted_iota(jnp.int32, xc.shape, 1) < HEAD_DIM
        s0 = jnp.sum(jnp.where(lo, xc, 0.0), axis=1, keepdims=True)
        s1 = jnp.sum(jnp.where(lo, 0.0, xc), axis=1, keepdims=True)
        outs.append(jnp.where(lo, s0, s1))
    return outs[0] if len(outs) == 1 else jnp.concatenate(outs, axis=1)


def _head_rms(t, g):
    ms = _head_sum(t * t) * (1.0 / HEAD_DIM)
    return t * lax.rsqrt(ms + NORM_EPS) * g


def _sigmoid(z):
    return 1.0 / (1.0 + jnp.exp(-z))


def _const_spec(shape):
    nd = len(shape)
    return pl.BlockSpec(shape, lambda *_: (0,) * nd)


def _params(sem):
    return pltpu.CompilerParams(dimension_semantics=sem, vmem_limit_bytes=VMEM_LIMIT)


def _rwkv_prep_kernel(x_ref, xp_ref, ng_ref, mu_ref, wr_ref, wk_ref, wv_ref,
                      w0_ref, w1_ref, w2_ref, a0_ref, a1_ref, a2_ref, g1_ref, g2_ref,
                      kkw_ref, kaw_ref,
                      r_out, lw_out, k_out, v_out, kk_out, b_out, g_out):
    i = pl.program_id(1)
    ng = ng_ref[...]
    h = _rms(x_ref[0], ng)
    hp = _rms(xp_ref[0][7:8, :], ng)
    hp = jnp.where(i > 0, hp, 0.0)
    rolled = pltpu.roll(h, 1, 0)
    first = jnp.where(lax.broadcasted_iota(jnp.int32, (8, h.shape[1]), 0) == 0, hp, rolled[:8])
    hs = jnp.concatenate([first, rolled[8:]], axis=0)
    hb = h.astype(BF16)
    xxb = (hs - h).astype(BF16)
    mub = mu_ref[...].astype(BF16)

    def mix(j):
        return hb + xxb * mub[j:j + 1, :]

    wl = _dot(jnp.tanh(_dot(mix(1), w1_ref[...])).astype(BF16), w2_ref[...])
    al = _dot(_dot(mix(4), a1_ref[...]).astype(BF16), a2_ref[...])
    k = _dot(mix(2), wk_ref[...])

    lw_out[0] = -EXP_M_HALF * _sigmoid(w0_ref[...] + wl)
    a = _sigmoid(a0_ref[...] + al)
    kk = k * kkw_ref[...]
    kk = kk * lax.rsqrt(jnp.maximum(_head_sum(kk * kk), 1e-24))
    k_out[0] = k * (1.0 + (a - 1.0) * kaw_ref[...])
    kk_out[0] = kk
    b_out[0] = kk * a

    g_out[0] = _dot(_sigmoid(_dot(mix(5), g1_ref[...])).astype(BF16), g2_ref[...])
    r_out[0] = _dot(mix(0), wr_ref[...])
    v_out[0] = _dot(mix(3), wv_ref[...])


def _rwkv_prep(x, ng, mu, wr, wk, wv, w0, w1, w2, a0, a1, a2, g1, g2, kkw, kaw, *, tm):
    B, T, D = x.shape
    row = lambda a: a.reshape(1, D)
    consts = [row(ng), mu, wr, wk, wv, row(w0), w1, w2, row(a0), a1, a2, g1, g2, row(kkw), row(kaw)]
    tile = pl.BlockSpec((1, tm, D), lambda b, i: (b, i, 0))
    prev = pl.BlockSpec((1, 8, D), lambda b, i: (b, jnp.maximum(i * (tm // 8) - 1, 0), 0))
    out = jax.ShapeDtypeStruct((B, T, D), F32)
    return pl.pallas_call(
        _rwkv_prep_kernel,
        grid=(B, T // tm),
        in_specs=[tile, prev] + [_const_spec(c.shape) for c in consts],
        out_specs=[tile] * 7,
        out_shape=[out] * 7,
        compiler_params=_params(("parallel", "parallel")),
        name="rwkv_prep",
    )(x, x, *consts)


def _blockdiag(z, lo):
    z = z.astype(BF16)
    zero = jnp.zeros_like(z)
    return jnp.concatenate([jnp.where(lo, z, zero), jnp.where(lo, zero, z)], axis=0)


def _rwkv_scan_kernel(r_ref, lw_ref, k_ref, v_ref, kk_ref, b_ref, g_ref,
                      rk_ref, lnw_ref, lnb_ref, y_out, s_scr, *, nbatch, npairs):
    c = pl.program_id(2)

    @pl.when(c == 0)
    def _():
        s_scr[...] = jnp.zeros_like(s_scr)

    C = CHUNK
    t_i = lax.broadcasted_iota(jnp.int32, (C, C), 0)
    j_i = lax.broadcasted_iota(jnp.int32, (C, C), 1)
    ltri = (j_i <= t_i).astype(BF16)
    row = lax.broadcasted_iota(jnp.int32, (C, LANES), 0)
    lane = lax.broadcasted_iota(jnp.int32, (C, LANES), 1)
    lo = lane < HEAD_DIM
    col = jnp.bitwise_and(lane, HEAD_DIM - 1)
    strict = col < row
    incl = col <= row
    rr = lax.broadcasted_iota(jnp.int32, (LANES, LANES), 0)
    cc = lax.broadcasted_iota(jnp.int32, (LANES, LANES), 1)
    same_head = (rr < HEAD_DIM) == (cc < HEAD_DIM)
    bd = functools.partial(_blockdiag, lo=lo)
    cat0 = lambda *xs: jnp.concatenate([x.astype(BF16) for x in xs], axis=0)
    cat1 = lambda *xs: jnp.concatenate([x.astype(BF16) for x in xs], axis=1)

    units = [(bi, slice(p * LANES, (p + 1) * LANES)) for bi in range(nbatch) for p in range(npairs)]
    P = range(len(units))
    r = [r_ref[bi, :, sl] for bi, sl in units]
    lw = [lw_ref[bi, :, sl] for bi, sl in units]
    k = [k_ref[bi, :, sl] for bi, sl in units]
    v = [v_ref[bi, :, sl] for bi, sl in units]
    kk = [kk_ref[bi, :, sl] for bi, sl in units]
    b = [b_ref[bi, :, sl] for bi, sl in units]

    cw2 = [_dot(ltri, cat1(*_split(lw[p]))) for p in P]
    cw = [cw2[p][:, :LANES] + cw2[p][:, LANES:] for p in P]
    cwl = [cw[p][C - 1:C, :] for p in P]
    at = [-kk[p] * jnp.exp(cw[p] - lw[p]) for p in P]
    dinv = [jnp.exp(-cw[p]) for p in P]
    rt = [r[p] * jnp.exp(cw[p]) for p in P]
    dend = [jnp.exp(cwl[p] - cw[p]) for p in P]

    x = [_dot_nt(cat0(at[p], rt[p]), cat0(bd(b[p] * dinv[p]), bd(k[p] * dinv[p]))) for p in P]
    aab = [jnp.where(strict, x[p][:C, :LANES], 0.0) for p in P]
    arb = [jnp.where(incl, x[p][C:, :LANES], 0.0) for p in P]
    aak = [jnp.where(strict, x[p][:C, LANES:], 0.0) for p in P]
    ark = [jnp.where(incl, x[p][C:, LANES:], 0.0) for p in P]

    bdv = [bd(v[p]) for p in P]
    av = [_dot(aak[p].astype(BF16), bdv[p]) for p in P]
    eye = jnp.where(col == row, 1.0, 0.0)
    tinv = [eye + aab[p] for p in P]
    n = [_dot(aab[p].astype(BF16), bd(aab[p])) for p in P]
    for it in range(5):
        last = it == 4
        res = [_dot(n[p].astype(BF16),
                    jnp.concatenate([bd(tinv[p])] + ([] if last else [bd(n[p])]), axis=1)) for p in P]
        tinv = [tinv[p] + res[p][:, :LANES] for p in P]
        if not last:
            n = [res[p][:, LANES:] for p in P]
    z = [_dot(tinv[p].astype(BF16), jnp.concatenate([bd(at[p]), bd(av[p])], axis=1)) for p in P]
    z1 = [z[p][:, :LANES] for p in P]
    z2 = [z[p][:, LANES:] for p in P]

    s = [s_scr[p] for p in P]
    ws = [_dot_nt(cat0(z1[p], rt[p]), s[p].astype(BF16)) for p in P]
    u = [ws[p][:C] + z2[p] for p in P]
    y = [ws[p][C:] + _dot(cat1(arb[p], ark[p]), cat0(bd(u[p]), bdv[p])) for p in P]
    upd = [_dot_tn(cat0(u[p], v[p]), cat0(b[p] * dend[p], k[p] * dend[p])) for p in P]
    for p in P:
        s_scr[p] = s[p] * jnp.exp(cwl[p]) + jnp.where(same_head, upd[p], 0.0)

    for p in P:
        bi, sl = units[p]
        mean = _head_sum(y[p]) * (1.0 / HEAD_DIM)
        d = y[p] - mean
        var = _head_sum(d * d) * (1.0 / HEAD_DIM)
        yn = d * lax.rsqrt(var + GN_EPS)
        bonus = _head_sum(r[p] * k[p] * rk_ref[:, sl]) * v[p]
        out = (yn * lnw_ref[:, sl] + lnb_ref[:, sl] + bonus) * g_ref[bi, :, sl]
        y_out[bi, :, sl] = out.astype(y_out.dtype)


def _rwkv_scan(r, lw, k, v, kk, b, g, rk, lnw, lnb, *, nbatch, npairs):
    B, T, D = r.shape
    W = npairs * LANES
    tile = pl.BlockSpec((nbatch, CHUNK, W), lambda bi, p, c: (bi, c, p))
    vec = pl.BlockSpec((1, W), lambda bi, p, c: (0, p))
    return pl.pallas_call(
        functools.partial(_rwkv_scan_kernel, nbatch=nbatch, npairs=npairs),
        grid=(B // nbatch, D // W, T // CHUNK),
        in_specs=[tile] * 7 + [vec] * 3,
        out_specs=tile,
        out_shape=jax.ShapeDtypeStruct((B, T, D), BF16),
        scratch_shapes=[pltpu.VMEM((nbatch * npairs, LANES, LANES), F32)],
        compiler_params=_params(("parallel", "parallel", "arbitrary")),
        name="rwkv_scan",
    )(r, lw, k, v, kk, b, g, rk.reshape(1, D), lnw.reshape(1, D), lnb.reshape(1, D))


def _proj_mlp_kernel(res_ref, a_ref, wo_ref, g_ref, win_ref, wout_ref, o_ref, *, tf):
    x = res_ref[...] + _dot(a_ref[...], wo_ref[...])
    xn = _rms(x, g_ref[...]).astype(BF16)
    acc = x
    for f in range(win_ref.shape[1] // tf):
        hid = jnp.maximum(_dot(xn, win_ref[:, f * tf:(f + 1) * tf]), 0.0)
        acc = acc + _dot((hid * hid).astype(BF16), wout_ref[f * tf:(f + 1) * tf, :])
    o_ref[...] = acc


def _proj_mlp(res, a, w_o, g, w_in, w_out, *, tm, tf):
    M, D = res.shape
    tile = pl.BlockSpec((tm, D), lambda i: (i, 0))
    return pl.pallas_call(
        functools.partial(_proj_mlp_kernel, tf=tf),
        grid=(M // tm,),
        in_specs=[tile, tile, _const_spec(w_o.shape), _const_spec((1, D)),
                  _const_spec(w_in.shape), _const_spec(w_out.shape)],
        out_specs=tile,
        out_shape=jax.ShapeDtypeStruct((M, D), F32),
        compiler_params=_params(("parallel",)),
        name="proj_mlp",
    )(res, a, w_o, g.reshape(1, D), w_in, w_out)


def _q_proj_kernel(x_ref, g_ref, wt_ref, qg_ref, c_ref, q_out, *, scale):
    hn = _rms(x_ref[0], g_ref[...])
    qt = _dot_nt(wt_ref[...], hn.astype(BF16))
    tm = hn.shape[0]
    row = lax.broadcasted_iota(jnp.int32, (HEAD_DIM, tm), 0)
    for h in range(qt.shape[0] // HEAD_DIM):
        hs = slice(h * HEAD_DIM, (h + 1) * HEAD_DIM)
        qh = qt[hs, :]
        ms = jnp.mean(qh * qh, axis=0, keepdims=True)
        qn = qh * lax.rsqrt(ms + NORM_EPS) * (qg_ref[hs, :] * scale)
        hi, mid, lo = _split3(c_ref[0, h:h + 1, :] * LOG2E)
        aug = jnp.where(row == 0, hi, jnp.where(row == 1, mid, jnp.where(
            row == 2, lo, jnp.where(row < 6, 1.0, 0.0))))
        q_out[0, h] = jnp.concatenate([qn, aug], axis=0).astype(q_out.dtype)


def _q_proj(x, g, wt, qg, c_row, *, tm, scale):
    B, T, D = x.shape
    H = D // HEAD_DIM
    return pl.pallas_call(
        functools.partial(_q_proj_kernel, scale=scale),
        grid=(B, T // tm),
        in_specs=[pl.BlockSpec((1, tm, D), lambda b, i: (b, i, 0)), _const_spec((1, D)),
                  _const_spec(wt.shape), _const_spec((D, 1)),
                  pl.BlockSpec((1, H, tm), lambda b, i: (b, 0, i))],
        out_specs=pl.BlockSpec((1, H, LANES, tm), lambda b, i: (b, 0, 0, i)),
        out_shape=jax.ShapeDtypeStruct((B, H, LANES, T), BF16),
        compiler_params=_params(("parallel", "parallel")),
        name="q_proj",
    )(x, g.reshape(1, D), wt, qg.reshape(D, 1), c_row)


def _shared_kv_kernel(x_ref, g_ref, wk_ref, wvt_ref, wf_ref, fb_ref, kg_ref,
                      k_out, vt_out, c_out, carry_scr):
    i = pl.program_id(1)

    @pl.when(i == 0)
    def _():
        carry_scr[...] = jnp.zeros_like(carry_scr)

    hn = _rms(x_ref[0], g_ref[...])
    hb = hn.astype(BF16)
    tm, D = hn.shape
    H = D // HEAD_DIM
    k = _head_rms(_dot(hb, wk_ref[...]), kg_ref[...])
    vt = _dot_nt(wvt_ref[...], hb)

    f = _dot(hb, wf_ref[...]) + fb_ref[...]
    logf = jnp.minimum(f, 0.0) - jnp.log(1.0 + jnp.exp(-jnp.abs(f)))
    t_i = lax.broadcasted_iota(jnp.int32, (tm, tm), 0)
    j_i = lax.broadcasted_iota(jnp.int32, (tm, tm), 1)
    ltri = (j_i <= t_i).astype(BF16)
    c3 = _dot(ltri, jnp.concatenate([t.astype(BF16) for t in _split3(logf)], axis=1))
    c = (c3[:, :LANES] + c3[:, LANES:2 * LANES]) + c3[:, 2 * LANES:] + carry_scr[0:1, :]
    carry_scr[...] = jnp.broadcast_to(c[tm - 1:tm, :], carry_scr.shape)
    c_out[0] = c.T[:H, :]

    nck = c * (-LOG2E)
    lane = lax.broadcasted_iota(jnp.int32, (tm, LANES), 1)
    vrow = lax.broadcasted_iota(jnp.int32, (HEAD_DIM, tm), 0)
    ones_row = jnp.where(vrow == 0, 1.0, 0.0)
    for h in range(H):
        base = k[:, (h // 2) * LANES:(h // 2 + 1) * LANES]
        if h % 2:
            base = pltpu.roll(base, HEAD_DIM, 1)
        hi, mid, lo = _split3(nck[:, h:h + 1])
        tile = jnp.where(lane < HEAD_DIM, base, jnp.where(lane < HEAD_DIM + 3, 1.0, jnp.where(
            lane == HEAD_DIM + 3, hi, jnp.where(lane == HEAD_DIM + 4, mid, jnp.where(
                lane == HEAD_DIM + 5, lo, 0.0)))))
        k_out[0, h] = tile.astype(k_out.dtype)
        vt_out[0, h, 0] = jnp.concatenate(
            [vt[h * HEAD_DIM:(h + 1) * HEAD_DIM, :], ones_row], axis=0).astype(vt_out.dtype)


def _shared_kv(x, g, wk, wvt, wf, fb, kg, *, tm):
    B, T, D = x.shape
    H = D // HEAD_DIM
    return pl.pallas_call(
        _shared_kv_kernel,
        grid=(B, T // tm),
        in_specs=[pl.BlockSpec((1, tm, D), lambda b, i: (b, i, 0)), _const_spec((1, D)),
                  _const_spec(wk.shape), _const_spec(wvt.shape), _const_spec(wf.shape),
                  _const_spec((1, LANES)), _const_spec((1, D))],
        out_specs=[pl.BlockSpec((1, H, tm, LANES), lambda b, i: (b, 0, i, 0)),
                   pl.BlockSpec((1, H, 1, LANES, tm), lambda b, i: (b, 0, i, 0, 0)),
                   pl.BlockSpec((1, H, tm), lambda b, i: (b, 0, i))],
        out_shape=[jax.ShapeDtypeStruct((B, H, T, LANES), BF16),
                   jax.ShapeDtypeStruct((B, H, T // tm, LANES, tm), BF16),
                   jax.ShapeDtypeStruct((B, H, T), F32)],
        scratch_shapes=[pltpu.VMEM((8, LANES), F32)],
        compiler_params=_params(("parallel", "arbitrary")),
        name="shared_kv",
    )(x, g.reshape(1, D), wk, wvt, wf, fb, kg.reshape(1, D))


def _fox_attn_kernel(q_ref, k_ref, vt_ref, o_ref, acc_scr, *, tq, nh):
    i = pl.program_id(2)
    qt = [q_ref[0, h] for h in range(nh)]
    acc_scr[...] = jnp.zeros_like(acc_scr)
    key_i = lax.broadcasted_iota(jnp.int32, (tq, tq), 0)
    qry_i = lax.broadcasted_iota(jnp.int32, (tq, tq), 1)
    causal = key_i <= qry_i

    def step(j, m, masked):
        off = pl.multiple_of(j * tq, tq)
        s = [_dot(k_ref[0, h, pl.ds(off, tq), :], qt[h]) for h in range(nh)]
        if masked:
            s = [jnp.where(causal, s[h], NEG_BIG) for h in range(nh)]
        m_new = [jnp.maximum(m[h], jnp.max(s[h], axis=0, keepdims=True)) for h in range(nh)]
        p = [jnp.exp2(s[h] - m_new[h]).astype(BF16) for h in range(nh)]
        alpha = [jnp.exp2(m[h] - m_new[h]) for h in range(nh)]
        pv = [_dot(vt_ref[0, h, j], p[h]) for h in range(nh)]
        for h in range(nh):
            acc_scr[h] = alpha[h] * acc_scr[h] + pv[h]
        return tuple(m_new)

    m0 = tuple(jnp.full((1, tq), NEG_BIG, F32) for _ in range(nh))
    m = lax.fori_loop(0, i, functools.partial(step, masked=False), m0)
    step(i, m, True)
    ot = [acc_scr[h, :HEAD_DIM, :] * (1.0 / acc_scr[h, HEAD_DIM:HEAD_DIM + 1, :]) for h in range(nh)]
    o_ref[0] = jnp.concatenate(ot, axis=0).T.astype(o_ref.dtype)


def _fox_attn(qt, ka, vt, *, tq, nh):
    B, H, _, T = qt.shape
    nb = T // tq
    return pl.pallas_call(
        functools.partial(_fox_attn_kernel, tq=tq, nh=nh),
        grid=(B, H // nh, nb),
        in_specs=[pl.BlockSpec((1, nh, LANES, tq), lambda b, p, i: (b, p, 0, i)),
                  pl.BlockSpec((1, nh, T, LANES), lambda b, p, i: (b, p, 0, 0)),
                  pl.BlockSpec((1, nh, nb, LANES, tq), lambda b, p, i: (b, p, 0, 0, 0))],
        out_specs=pl.BlockSpec((1, tq, nh * HEAD_DIM), lambda b, p, i: (b, i, p)),
        out_shape=jax.ShapeDtypeStruct((B, T, H * HEAD_DIM), BF16),
        scratch_shapes=[pltpu.VMEM((nh, LANES, tq), F32)],
        compiler_params=_params(("parallel", "parallel", "arbitrary")),
        name="fox_attn",
    )(qt, ka, vt)


def kernel(x, rwkv_norm_g, rwkv_mu, rwkv_w_rkv, rwkv_w0, rwkv_w1, rwkv_w2, rwkv_a0, rwkv_a1, rwkv_a2, rwkv_g1, rwkv_g2, rwkv_k_k, rwkv_k_a, rwkv_r_k, rwkv_lnx_w, rwkv_lnx_b, rwkv_w_o, kv_norm_g, kv_w, kv_f_bias, k_norm_g, attn_norm_g, attn_w_q, q_norm_g, attn_w_o, mlp_norm_g, mlp_w_in, mlp_w_out):
    B, T, D = x.shape
    M = B * T
    n_a = rwkv_norm_g.shape[0]
    depth = mlp_norm_g.shape[0]
    bf = lambda w: w.astype(BF16)
    tm_prep = min(256, T)
    tm = min(512, T)
    tq = min(256, T)
    tf = min(1024, mlp_w_in.shape[-1])
    npairs = D // LANES

    k_sh = v_sh = c_sh = None
    for layer in range(depth):
        if layer < n_a:
            i = layer
            r, lw, k, v, kk, b, g = _rwkv_prep(
                x, rwkv_norm_g[i], rwkv_mu[i], bf(rwkv_w_rkv[i, 0]), bf(rwkv_w_rkv[i, 1]),
                bf(rwkv_w_rkv[i, 2]), rwkv_w0[i], bf(rwkv_w1[i]), bf(rwkv_w2[i]), rwkv_a0[i],
                bf(rwkv_a1[i]), bf(rwkv_a2[i]), bf(rwkv_g1[i]), bf(rwkv_g2[i]),
                rwkv_k_k[i], rwkv_k_a[i], tm=tm_prep)
            mix = _rwkv_scan(r, lw, k, v, kk, b, g, rwkv_r_k[i], rwkv_lnx_w[i], rwkv_lnx_b[i],
                             nbatch=2 if B % 2 == 0 else 1, npairs=npairs)
            w_o = rwkv_w_o[i]
        else:
            j = layer - n_a
            qt = _q_proj(x, attn_norm_g[j], bf(attn_w_q[j].T), q_norm_g[j], c_sh,
                         tm=tq, scale=HEAD_DIM ** -0.5 * LOG2E)
            mix = _fox_attn(qt, k_sh, v_sh, tq=tq, nh=min(8, D // HEAD_DIM))
            w_o = attn_w_o[j]
        x = _proj_mlp(x.reshape(M, D), mix.reshape(M, D), bf(w_o), mlp_norm_g[layer],
                      bf(mlp_w_in[layer]), bf(mlp_w_out[layer]), tm=tm, tf=tf).reshape(B, T, D)
        if layer == n_a - 1:
            wf = jnp.pad(kv_w[:, 2 * D:], ((0, 0), (0, LANES - (kv_w.shape[1] - 2 * D))))
            fb = jnp.pad(kv_f_bias, (0, LANES - kv_f_bias.shape[0])).reshape(1, LANES)
            k_sh, v_sh, c_sh = _shared_kv(
                x, kv_norm_g, bf(kv_w[:, :D]), bf(kv_w[:, D:2 * D].T), bf(wf), fb, k_norm_g, tm=tq)
    return x
```

```python
import functools

import jax
import jax.numpy as jnp
from jax import lax
from jax.experimental import pallas as pl
from jax.experimental.pallas import tpu as pltpu

HEAD_DIM = 64
LANES = 128
NORM_EPS = 1e-6
GN_EPS = 64e-5
CHUNK = 64
NEG_BIG = -1e30
LOG2E = 1.4426950408889634
EXP_M_HALF = 0.6065306597126334
V_ROWS = 80
VMEM_LIMIT = 56 * 1024 * 1024

BF16 = jnp.bfloat16
F32 = jnp.float32

_NT = (((1,), (1,)), ((), ()))
_TN = (((0,), (0,)), ((), ()))


def _dot(a, b):
    return jnp.dot(a, b, preferred_element_type=F32)


def _dot_nt(a, b):
    return lax.dot_general(a, b, _NT, preferred_element_type=F32)


def _dot_tn(a, b):
    return lax.dot_general(a, b, _TN, preferred_element_type=F32)


def _split(a):
    hi = a.astype(BF16)
    return hi, (a - hi.astype(F32)).astype(BF16)


def _split3(a):
    hi = a.astype(BF16).astype(F32)
    r1 = a - hi
    mid = r1.astype(BF16).astype(F32)
    return hi, mid, r1 - mid


def _rms(x, g):
    return x * lax.rsqrt(jnp.mean(x * x, axis=-1, keepdims=True) + NORM_EPS) * g


def _head_sum(x):
    outs = []
    for c in range(x.shape[1] // LANES):
        xc = x[:, c * LANES:(c + 1) * LANES]
        lo = lax.broadcasted_iota(jnp.int32, xc.shape, 1) < HEAD_DIM
        s0 = jnp.sum(jnp.where(lo, xc, 0.0), axis=1, keepdims=True)
        s1 = jnp.sum(jnp.where(lo, 0.0, xc), axis=1, keepdims=True)
        outs.append(jnp.where(lo, s0, s1))
    return outs[0] if len(outs) == 1 else jnp.concatenate(outs, axis=1)


def _head_rms(t, g):
    ms = _head_sum(t * t) * (1.0 / HEAD_DIM)
    return t * lax.rsqrt(ms + NORM_EPS) * g


def _sigmoid(z):
    return 1.0 / (1.0 + jnp.exp(-z))


def _const_spec(shape):
    nd = len(shape)
    return pl.BlockSpec(shape, lambda *_: (0,) * nd)


def _params(sem):
    return pltpu.CompilerParams(dimension_semantics=sem, vmem_limit_bytes=VMEM_LIMIT)


def _rwkv_prep_kernel(x_ref, xp_ref, ng_ref, mu_ref, wr_ref, wk_ref, wv_ref,
                      w0_ref, w1_ref, w2_ref, a0_ref, a1_ref, a2_ref, g1_ref, g2_ref,
                      kkw_ref, kaw_ref,
                      r_out, lw_out, k_out, v_out, kk_out, b_out, g_out):
    i = pl.program_id(1)
    ng = ng_ref[...]
    h = _rms(x_ref[0], ng)
    hp = _rms(xp_ref[0][7:8, :], ng)
    hp = jnp.where(i > 0, hp, 0.0)
    rolled = pltpu.roll(h, 1, 0)
    first = jnp.where(lax.broadcasted_iota(jnp.int32, (8, h.shape[1]), 0) == 0, hp, rolled[:8])
    hs = jnp.concatenate([first, rolled[8:]], axis=0)
    hb = h.astype(BF16)
    xxb = (hs - h).astype(BF16)
    mub = mu_ref[...].astype(BF16)

    def mix(j):
        return hb + xxb * mub[j:j + 1, :]

    wl = _dot(jnp.tanh(_dot(mix(1), w1_ref[...])).astype(BF16), w2_ref[...])
    al = _dot(_dot(mix(4), a1_ref[...]).astype(BF16), a2_ref[...])
    k = _dot(mix(2), wk_ref[...])

    lw_out[0] = -EXP_M_HALF * _sigmoid(w0_ref[...] + wl)
    a = _sigmoid(a0_ref[...] + al)
    kk = k * kkw_ref[...]
    kk = kk * lax.rsqrt(jnp.maximum(_head_sum(kk * kk), 1e-24))
    k_out[0] = k * (1.0 + (a - 1.0) * kaw_ref[...])
    kk_out[0] = kk
    b_out[0] = kk * a

    g_out[0] = _dot(_sigmoid(_dot(mix(5), g1_ref[...])).astype(BF16), g2_ref[...])
    r_out[0] = _dot(mix(0), wr_ref[...])
    v_out[0] = _dot(mix(3), wv_ref[...])


def _rwkv_prep(x, ng, mu, wr, wk, wv, w0, w1, w2, a0, a1, a2, g1, g2, kkw, kaw, *, tm):
    B, T, D = x.shape
    row = lambda a: a.reshape(1, D)
    consts = [row(ng), mu, wr, wk, wv, row(w0), w1, w2, row(a0), a1, a2, g1, g2, row(kkw), row(kaw)]
    tile = pl.BlockSpec((1, tm, D), lambda b, i: (b, i, 0))
    prev = pl.BlockSpec((1, 8, D), lambda b, i: (b, jnp.maximum(i * (tm // 8) - 1, 0), 0))
    out = jax.ShapeDtypeStruct((B, T, D), F32)
    return pl.pallas_call(
        _rwkv_prep_kernel,
        grid=(B, T // tm),
        in_specs=[tile, prev] + [_const_spec(c.shape) for c in consts],
        out_specs=[tile] * 7,
        out_shape=[out] * 7,
        compiler_params=_params(("parallel", "parallel")),
        name="rwkv_prep",
    )(x, x, *consts)


def _blockdiag(z, lo):
    z = z.astype(BF16)
    zero = jnp.zeros_like(z)
    return jnp.concatenate([jnp.where(lo, z, zero), jnp.where(lo, zero, z)], axis=0)


def _rwkv_scan_kernel(r_ref, lw_ref, k_ref, v_ref, kk_ref, b_ref, g_ref,
                      rk_ref, lnw_ref, lnb_ref, y_out, s_scr, *, nbatch, npairs):
    c = pl.program_id(2)

    @pl.when(c == 0)
    def _():
        s_scr[...] = jnp.zeros_like(s_scr)

    C = CHUNK
    t_i = lax.broadcasted_iota(jnp.int32, (C, C), 0)
    j_i = lax.broadcasted_iota(jnp.int32, (C, C), 1)
    ltri = (j_i <= t_i).astype(BF16)
    row = lax.broadcasted_iota(jnp.int32, (C, LANES), 0)
    lane = lax.broadcasted_iota(jnp.int32, (C, LANES), 1)
    lo = lane < HEAD_DIM
    col = jnp.bitwise_and(lane, HEAD_DIM - 1)
    strict = col < row
    incl = col <= row
    rr = lax.broadcasted_iota(jnp.int32, (LANES, LANES), 0)
    cc = lax.broadcasted_iota(jnp.int32, (LANES, LANES), 1)
    same_head = (rr < HEAD_DIM) == (cc < HEAD_DIM)
    bd = functools.partial(_blockdiag, lo=lo)
    cat0 = lambda *xs: jnp.concatenate([x.astype(BF16) for x in xs], axis=0)
    cat1 = lambda *xs: jnp.concatenate([x.astype(BF16) for x in xs], axis=1)

    units = [(bi, slice(p * LANES, (p + 1) * LANES)) for bi in range(nbatch) for p in range(npairs)]
    P = range(len(units))
    r = [r_ref[bi, :, sl] for bi, sl in units]
    lw = [lw_ref[bi, :, sl] for bi, sl in units]
    k = [k_ref[bi, :, sl] for bi, sl in units]
    v = [v_ref[bi, :, sl] for bi, sl in units]
    kk = [kk_ref[bi, :, sl] for bi, sl in units]
    b = [b_ref[bi, :, sl] for bi, sl in units]

    cw2 = [_dot(ltri, cat1(*_split(lw[p]))) for p in P]
    cw = [cw2[p][:, :LANES] + cw2[p][:, LANES:] for p in P]
    cwl = [cw[p][C - 1:C, :] for p in P]
    at = [-kk[p] * jnp.exp(cw[p] - lw[p]) for p in P]
    dinv = [jnp.exp(-cw[p]) for p in P]
    rt = [r[p] * jnp.exp(cw[p]) for p in P]
    dend = [jnp.exp(cwl[p] - cw[p]) for p in P]

    x = [_dot_nt(cat0(at[p], rt[p]), cat0(bd(b[p] * dinv[p]), bd(k[p] * dinv[p]))) for p in P]
    aab = [jnp.where(strict, x[p][:C, :LANES], 0.0) for p in P]
    arb = [jnp.where(incl, x[p][C:, :LANES], 0.0) for p in P]
    aak = [jnp.where(strict, x[p][:C, LANES:], 0.0) for p in P]
    ark = [jnp.where(incl, x[p][C:, LANES:], 0.0) for p in P]

    bdv = [bd(v[p]) for p in P]
    av = [_dot(aak[p].astype(BF16), bdv[p]) for p in P]
    eye = jnp.where(col == row, 1.0, 0.0)
    tinv = [eye + aab[p] for p in P]
    n = [_dot(aab[p].astype(BF16), bd(aab[p])) for p in P]
    for it in range(5):
        last = it == 4
        res = [_dot(n[p].astype(BF16),
                    jnp.concatenate([bd(tinv[p])] + ([] if last else [bd(n[p])]), axis=1)) for p in P]
        tinv = [tinv[p] + res[p][:, :LANES] for p in P]
        if not last:
            n = [res[p][:, LANES:] for p in P]
    z = [_dot(tinv[p].astype(BF16), jnp.concatenate([bd(at[p]), bd(av[p])], axis=1)) for p in P]
    z1 = [z[p][:, :LANES] for p in P]
    z2 = [z[p][:, LANES:] for p in P]

    s = [s_scr[p] for p in P]
    ws = [_dot_nt(cat0(z1[p], rt[p]), s[p].astype(BF16)) for p in P]
    u = [ws[p][:C] + z2[p] for p in P]
    y = [ws[p][C:] + _dot(cat1(arb[p], ark[p]), cat0(bd(u[p]), bdv[p])) for p in P]
    upd = [_dot_tn(cat0(u[p], v[p]), cat0(b[p] * dend[p], k[p] * dend[p])) for p in P]
    for p in P:
        s_scr[p] = s[p] * jnp.exp(cwl[p]) + jnp.where(same_head, upd[p], 0.0)

    for p in P:
        bi, sl = units[p]
        mean = _head_sum(y[p]) * (1.0 / HEAD_DIM)
        d = y[p] - mean
        var = _head_sum(d * d) * (1.0 / HEAD_DIM)
        yn = d * lax.rsqrt(var + GN_EPS)
        bonus = _head_sum(r[p] * k[p] * rk_ref[:, sl]) * v[p]
        out = (yn * lnw_ref[:, sl] + lnb_ref[:, sl] + bonus) * g_ref[bi, :, sl]
        y_out[bi, :, sl] = out.astype(y_out.dtype)


def _rwkv_scan(r, lw, k, v, kk, b, g, rk, lnw, lnb, *, nbatch, npairs):
    B, T, D = r.shape
    W = npairs * LANES
    tile = pl.BlockSpec((nbatch, CHUNK, W), lambda bi, p, c: (bi, c, p))
    vec = pl.BlockSpec((1, W), lambda bi, p, c: (0, p))
    return pl.pallas_call(
        functools.partial(_rwkv_scan_kernel, nbatch=nbatch, npairs=npairs),
        grid=(B // nbatch, D // W, T // CHUNK),
        in_specs=[tile] * 7 + [vec] * 3,
        out_specs=tile,
        out_shape=jax.ShapeDtypeStruct((B, T, D), BF16),
        scratch_shapes=[pltpu.VMEM((nbatch * npairs, LANES, LANES), F32)],
        compiler_params=_params(("parallel", "parallel", "arbitrary")),
        name="rwkv_scan",
    )(r, lw, k, v, kk, b, g, rk.reshape(1, D), lnw.reshape(1, D), lnb.reshape(1, D))


def _proj_mlp_kernel(res_ref, a_ref, wo_ref, g_ref, win_ref, wout_ref, o_ref, *, tf):
    x = res_ref[...] + _dot(a_ref[...], wo_ref[...])
    xn = _rms(x, g_ref[...]).astype(BF16)
    acc = x
    for f in range(win_ref.shape[1] // tf):
        hid = jnp.maximum(_dot(xn, win_ref[:, f * tf:(f + 1) * tf]), 0.0)
        acc = acc + _dot((hid * hid).astype(BF16), wout_ref[f * tf:(f + 1) * tf, :])
    o_ref[...] = acc


def _proj_mlp(res, a, w_o, g, w_in, w_out, *, tm, tf):
    M, D = res.shape
    tile = pl.BlockSpec((tm, D), lambda i: (i, 0))
    return pl.pallas_call(
        functools.partial(_proj_mlp_kernel, tf=tf),
        grid=(M // tm,),
        in_specs=[tile, tile, _const_spec(w_o.shape), _const_spec((1, D)),
                  _const_spec(w_in.shape), _const_spec(w_out.shape)],
        out_specs=tile,
        out_shape=jax.ShapeDtypeStruct((M, D), F32),
        compiler_params=_params(("parallel",)),
        name="proj_mlp",
    )(res, a, w_o, g.reshape(1, D), w_in, w_out)


def _q_proj_kernel(x_ref, g_ref, wt_ref, qg_ref, c_ref, q_out, *, scale):
    hn = _rms(x_ref[0], g_ref[...])
    qt = _dot_nt(wt_ref[...], hn.astype(BF16))
    tm = hn.shape[0]
    row = lax.broadcasted_iota(jnp.int32, (HEAD_DIM, tm), 0)
    for h in range(qt.shape[0] // HEAD_DIM):
        hs = slice(h * HEAD_DIM, (h + 1) * HEAD_DIM)
        qh = qt[hs, :]
        ms = jnp.mean(qh * qh, axis=0, keepdims=True)
        qn = qh * lax.rsqrt(ms + NORM_EPS) * (qg_ref[hs, :] * scale)
        hi, mid, lo = _split3(c_ref[0, h:h + 1, :] * LOG2E)
        aug = jnp.where(row == 0, hi, jnp.where(row == 1, mid, jnp.where(
            row == 2, lo, jnp.where(row < 6, 1.0, 0.0))))
        q_out[0, h] = jnp.concatenate([qn, aug], axis=0).astype(q_out.dtype)


def _q_proj(x, g, wt, qg, c_row, *, tm, scale):
    B, T, D = x.shape
    H = D // HEAD_DIM
    return pl.pallas_call(
        functools.partial(_q_proj_kernel, scale=scale),
        grid=(B, T // tm),
        in_specs=[pl.BlockSpec((1, tm, D), lambda b, i: (b, i, 0)), _const_spec((1, D)),
                  _const_spec(wt.shape), _const_spec((D, 1)),
                  pl.BlockSpec((1, H, tm), lambda b, i: (b, 0, i))],
        out_specs=pl.BlockSpec((1, H, LANES, tm), lambda b, i: (b, 0, 0, i)),
        out_shape=jax.ShapeDtypeStruct((B, H, LANES, T), BF16),
        compiler_params=_params(("parallel", "parallel")),
        name="q_proj",
    )(x, g.reshape(1, D), wt, qg.reshape(D, 1), c_row)


def _shared_kv_kernel(x_ref, g_ref, wk_ref, wvt_ref, wf_ref, fb_ref, kg_ref,
                      k_out, vt_out, c_out, carry_scr):
    i = pl.program_id(1)

    @pl.when(i == 0)
    def _():
        carry_scr[...] = jnp.zeros_like(carry_scr)

    hn = _rms(x_ref[0], g_ref[...])
    hb = hn.astype(BF16)
    tm, D = hn.shape
    H = D // HEAD_DIM
    k = _head_rms(_dot(hb, wk_ref[...]), kg_ref[...])
    vt = _dot_nt(wvt_ref[...], hb)

    f = _dot(hb, wf_ref[...]) + fb_ref[...]
    logf = jnp.minimum(f, 0.0) - jnp.log(1.0 + jnp.exp(-jnp.abs(f)))
    t_i = lax.broadcasted_iota(jnp.int32, (tm, tm), 0)
    j_i = lax.broadcasted_iota(jnp.int32, (tm, tm), 1)
    ltri = (j_i <= t_i).astype(BF16)
    c3 = _dot(ltri, jnp.concatenate([t.astype(BF16) for t in _split3(logf)], axis=1))
    c = (c3[:, :LANES] + c3[:, LANES:2 * LANES]) + c3[:, 2 * LANES:] + carry_scr[0:1, :]
    carry_scr[...] = jnp.broadcast_to(c[tm - 1:tm, :], carry_scr.shape)
    c_out[0] = c.T[:H, :]

    nck = c * (-LOG2E)
    lane = lax.broadcasted_iota(jnp.int32, (tm, LANES), 1)
    vrow = lax.broadcasted_iota(jnp.int32, (HEAD_DIM, tm), 0)
    ones_row = jnp.where(vrow == 0, 1.0, 0.0)
    for h in range(H):
        base = k[:, (h // 2) * LANES:(h // 2 + 1) * LANES]
        if h % 2:
            base = pltpu.roll(base, HEAD_DIM, 1)
        hi, mid, lo = _split3(nck[:, h:h + 1])
        tile = jnp.where(lane < HEAD_DIM, base, jnp.where(lane < HEAD_DIM + 3, 1.0, jnp.where(
            lane == HEAD_DIM + 3, hi, jnp.where(lane == HEAD_DIM + 4, mid, jnp.where(
                lane == HEAD_DIM + 5, lo, 0.0)))))
        k_out[0, h] = tile.astype(k_out.dtype)
        vt_out[0, h, 0] = jnp.concatenate(
            [vt[h * HEAD_DIM:(h + 1) * HEAD_DIM, :], ones_row], axis=0).astype(vt_out.dtype)


def _shared_kv(x, g, wk, wvt, wf, fb, kg, *, tm):
    B, T, D = x.shape
    H = D // HEAD_DIM
    return pl.pallas_call(
        _shared_kv_kernel,
        grid=(B, T // tm),
        in_specs=[pl.BlockSpec((1, tm, D), lambda b, i: (b, i, 0)), _const_spec((1, D)),
                  _const_spec(wk.shape), _const_spec(wvt.shape), _const_spec(wf.shape),
                  _const_spec((1, LANES)), _const_spec((1, D))],
        out_specs=[pl.BlockSpec((1, H, tm, LANES), lambda b, i: (b, 0, i, 0)),
                   pl.BlockSpec((1, H, 1, LANES, tm), lambda b, i: (b, 0, i, 0, 0)),
                   pl.BlockSpec((1, H, tm), lambda b, i: (b, 0, i))],
        out_shape=[jax.ShapeDtypeStruct((B, H, T, LANES), BF16),
                   jax.ShapeDtypeStruct((B, H, T // tm, LANES, tm), BF16),
                   jax.ShapeDtypeStruct((B, H, T), F32)],
        scratch_shapes=[pltpu.VMEM((8, LANES), F32)],
        compiler_params=_params(("parallel", "arbitrary")),
        name="shared_kv",
    )(x, g.reshape(1, D), wk, wvt, wf, fb, kg.reshape(1, D))


def _fox_attn_kernel(q_ref, k_ref, vt_ref, o_ref, acc_scr, *, tq, nh):
    i = pl.program_id(2)
    qt = [q_ref[0, h] for h in range(nh)]
    acc_scr[...] = jnp.zeros_like(acc_scr)
    key_i = lax.broadcasted_iota(jnp.int32, (tq, tq), 0)
    qry_i = lax.broadcasted_iota(jnp.int32, (tq, tq), 1)
    causal = key_i <= qry_i

    def step(j, m, masked):
        off = pl.multiple_of(j * tq, tq)
        s = [_dot(k_ref[0, h, pl.ds(off, tq), :], qt[h]) for h in range(nh)]
        if masked:
            s = [jnp.where(causal, s[h], NEG_BIG) for h in range(nh)]
        m_new = [jnp.maximum(m[h], jnp.max(s[h], axis=0, keepdims=True)) for h in range(nh)]
        p = [jnp.exp2(s[h] - m_new[h]).astype(BF16) for h in range(nh)]
        alpha = [jnp.exp2(m[h] - m_new[h]) for h in range(nh)]
        pv = [_dot(vt_ref[0, h, j, :V_ROWS, :], p[h]) for h in range(nh)]
        for h in range(nh):
            acc_scr[h] = alpha[h] * acc_scr[h] + pv[h]
        return tuple(m_new)

    m0 = tuple(jnp.full((1, tq), NEG_BIG, F32) for _ in range(nh))
    m = lax.fori_loop(0, i, functools.partial(step, masked=False), m0)
    step(i, m, True)
    ot = [acc_scr[h, :HEAD_DIM, :] * (1.0 / acc_scr[h, HEAD_DIM:HEAD_DIM + 1, :]) for h in range(nh)]
    o_ref[0] = jnp.concatenate(ot, axis=0).T.astype(o_ref.dtype)


def _fox_attn(qt, ka, vt, *, tq, nh):
    B, H, _, T = qt.shape
    nb = T // tq
    return pl.pallas_call(
        functools.partial(_fox_attn_kernel, tq=tq, nh=nh),
        grid=(B, H // nh, nb),
        in_specs=[pl.BlockSpec((1, nh, LANES, tq), lambda b, p, i: (b, p, 0, i)),
                  pl.BlockSpec((1, nh, T, LANES), lambda b, p, i: (b, p, 0, 0)),
                  pl.BlockSpec((1, nh, nb, LANES, tq), lambda b, p, i: (b, p, 0, 0, 0))],
        out_specs=pl.BlockSpec((1, tq, nh * HEAD_DIM), lambda b, p, i: (b, i, p)),
        out_shape=jax.ShapeDtypeStruct((B, T, H * HEAD_DIM), BF16),
        scratch_shapes=[pltpu.VMEM((nh, V_ROWS, tq), F32)],
        compiler_params=_params(("parallel", "parallel", "arbitrary")),
        name="fox_attn",
    )(qt, ka, vt)


def kernel(x, rwkv_norm_g, rwkv_mu, rwkv_w_rkv, rwkv_w0, rwkv_w1, rwkv_w2, rwkv_a0, rwkv_a1, rwkv_a2, rwkv_g1, rwkv_g2, rwkv_k_k, rwkv_k_a, rwkv_r_k, rwkv_lnx_w, rwkv_lnx_b, rwkv_w_o, kv_norm_g, kv_w, kv_f_bias, k_norm_g, attn_norm_g, attn_w_q, q_norm_g, attn_w_o, mlp_norm_g, mlp_w_in, mlp_w_out):
    B, T, D = x.shape
    M = B * T
    n_a = rwkv_norm_g.shape[0]
    depth = mlp_norm_g.shape[0]
    bf = lambda w: w.astype(BF16)
    tm_prep = min(256, T)
    tm = min(512, T)
    tq = min(256, T)
    tf = min(1024, mlp_w_in.shape[-1])
    npairs = D // LANES

    k_sh = v_sh = c_sh = None
    for layer in range(depth):
        if layer < n_a:
            i = layer
            r, lw, k, v, kk, b, g = _rwkv_prep(
                x, rwkv_norm_g[i], rwkv_mu[i], bf(rwkv_w_rkv[i, 0]), bf(rwkv_w_rkv[i, 1]),
                bf(rwkv_w_rkv[i, 2]), rwkv_w0[i], bf(rwkv_w1[i]), bf(rwkv_w2[i]), rwkv_a0[i],
                bf(rwkv_a1[i]), bf(rwkv_a2[i]), bf(rwkv_g1[i]), bf(rwkv_g2[i]),
                rwkv_k_k[i], rwkv_k_a[i], tm=tm_prep)
            mix = _rwkv_scan(r, lw, k, v, kk, b, g, rwkv_r_k[i], rwkv_lnx_w[i], rwkv_lnx_b[i],
                             nbatch=2 if B % 2 == 0 else 1, npairs=npairs)
            w_o = rwkv_w_o[i]
        else:
            j = layer - n_a
            qt = _q_proj(x, attn_norm_g[j], bf(attn_w_q[j].T), q_norm_g[j], c_sh,
                         tm=tq, scale=HEAD_DIM ** -0.5 * LOG2E)
            mix = _fox_attn(qt, k_sh, v_sh, tq=tq, nh=min(16, D // HEAD_DIM))
            w_o = attn_w_o[j]
        x = _proj_mlp(x.reshape(M, D), mix.reshape(M, D), bf(w_o), mlp_norm_g[layer],
                      bf(mlp_w_in[layer]), bf(mlp_w_out[layer]), tm=tm, tf=tf).reshape(B, T, D)
        if layer == n_a - 1:
            wf = jnp.pad(kv_w[:, 2 * D:], ((0, 0), (0, LANES - (kv_w.shape[1] - 2 * D))))
            fb = jnp.pad(kv_f_bias, (0, LANES - kv_f_bias.shape[0])).reshape(1, LANES)
            k_sh, v_sh, c_sh = _shared_kv(
                x, kv_norm_g, bf(kv_w[:, :D]), bf(kv_w[:, D:2 * D].T), bf(wf), fb, k_norm_g, tm=tq)
    return x
```

```python
import functools

import jax
import jax.numpy as jnp
from jax import lax
from jax.experimental import pallas as pl
from jax.experimental.pallas import tpu as pltpu

HEAD_DIM = 64
LANES = 128
NORM_EPS = 1e-6
GN_EPS = 64e-5
CHUNK = 64
NEG_BIG = -1e30
LOG2E = 1.4426950408889634
EXP_M_HALF = 0.6065306597126334
V_ROWS = 80
VMEM_LIMIT = 56 * 1024 * 1024

BF16 = jnp.bfloat16
F32 = jnp.float32

_NT = (((1,), (1,)), ((), ()))
_TN = (((0,), (0,)), ((), ()))


def _dot(a, b):
    return jnp.dot(a, b, preferred_element_type=F32)


def _dot_nt(a, b):
    return lax.dot_general(a, b, _NT, preferred_element_type=F32)


def _dot_tn(a, b):
    return lax.dot_general(a, b, _TN, preferred_element_type=F32)


def _split(a):
    hi = a.astype(BF16)
    return hi, (a - hi.astype(F32)).astype(BF16)


def _split3(a):
    hi = a.astype(BF16).astype(F32)
    r1 = a - hi
    mid = r1.astype(BF16).astype(F32)
    return hi, mid, r1 - mid


def _rms(x, g):
    return x * lax.rsqrt(jnp.mean(x * x, axis=-1, keepdims=True) + NORM_EPS) * g


def _head_sum(x):
    outs = []
    for c in range(x.shape[1] // LANES):
        xc = x[:, c * LANES:(c + 1) * LANES]
        lo = lax.broadcasted_iota(jnp.int32, xc.shape, 1) < HEAD_DIM
        s0 = jnp.sum(jnp.where(lo, xc, 0.0), axis=1, keepdims=True)
        s1 = jnp.sum(jnp.where(lo, 0.0, xc), axis=1, keepdims=True)
        outs.append(jnp.where(lo, s0, s1))
    return outs[0] if len(outs) == 1 else jnp.concatenate(outs, axis=1)


def _head_rms(t, g):
    ms = _head_sum(t * t) * (1.0 / HEAD_DIM)
    return t * lax.rsqrt(ms + NORM_EPS) * g


def _sigmoid(z):
    return 1.0 / (1.0 + jnp.exp(-z))


def _const_spec(shape):
    nd = len(shape)
    return pl.BlockSpec(shape, lambda *_: (0,) * nd)


def _params(sem):
    return pltpu.CompilerParams(dimension_semantics=sem, vmem_limit_bytes=VMEM_LIMIT)


def _rwkv_prep_kernel(x_ref, xp_ref, ng_ref, mu_ref, wr_ref, wk_ref, wv_ref,
                      w0_ref, w1_ref, w2_ref, a0_ref, a1_ref, a2_ref, g1_ref, g2_ref,
                      kkw_ref, kaw_ref,
                      r_out, lw_out, k_out, v_out, kk_out, b_out, g_out):
    i = pl.program_id(1)
    ng = ng_ref[...]
    h = _rms(x_ref[0], ng)
    hp = _rms(xp_ref[0][7:8, :], ng)
    hp = jnp.where(i > 0, hp, 0.0)
    rolled = pltpu.roll(h, 1, 0)
    first = jnp.where(lax.broadcasted_iota(jnp.int32, (8, h.shape[1]), 0) == 0, hp, rolled[:8])
    hs = jnp.concatenate([first, rolled[8:]], axis=0)
    hb = h.astype(BF16)
    xxb = (hs - h).astype(BF16)
    mub = mu_ref[...].astype(BF16)

    def mix(j):
        return hb + xxb * mub[j:j + 1, :]

    tw = _dot(mix(1), w1_ref[...])
    ta = _dot(mix(4), a1_ref[...])
    tg = _dot(mix(5), g1_ref[...])
    k = _dot(mix(2), wk_ref[...])
    wl = _dot(jnp.tanh(tw).astype(BF16), w2_ref[...])
    al = _dot(ta.astype(BF16), a2_ref[...])
    g_out[0] = _dot(_sigmoid(tg).astype(BF16), g2_ref[...]).astype(g_out.dtype)

    lw_out[0] = -EXP_M_HALF * _sigmoid(w0_ref[...] + wl)
    a = _sigmoid(a0_ref[...] + al)
    kk = k * kkw_ref[...]
    kk = kk * lax.rsqrt(jnp.maximum(_head_sum(kk * kk), 1e-24))
    k_out[0] = (k * (1.0 + (a - 1.0) * kaw_ref[...])).astype(k_out.dtype)
    kk_out[0] = kk.astype(kk_out.dtype)
    b_out[0] = (kk * a).astype(b_out.dtype)

    r_out[0] = _dot(mix(0), wr_ref[...]).astype(r_out.dtype)
    v_out[0] = _dot(mix(3), wv_ref[...]).astype(v_out.dtype)


def _rwkv_prep(x, ng, mu, wr, wk, wv, w0, w1, w2, a0, a1, a2, g1, g2, kkw, kaw, *, tm):
    B, T, D = x.shape
    row = lambda a: a.reshape(1, D)
    consts = [row(ng), mu, wr, wk, wv, row(w0), w1, w2, row(a0), a1, a2, g1, g2, row(kkw), row(kaw)]
    tile = pl.BlockSpec((1, tm, D), lambda b, i: (b, i, 0))
    prev = pl.BlockSpec((1, 8, D), lambda b, i: (b, jnp.maximum(i * (tm // 8) - 1, 0), 0))
    out = lambda dt: jax.ShapeDtypeStruct((B, T, D), dt)
    return pl.pallas_call(
        _rwkv_prep_kernel,
        grid=(B, T // tm),
        in_specs=[tile, prev] + [_const_spec(c.shape) for c in consts],
        out_specs=[tile] * 7,
        out_shape=[out(BF16), out(F32)] + [out(BF16)] * 5,
        compiler_params=_params(("parallel", "parallel")),
        name="rwkv_prep",
    )(x, x, *consts)


def _blockdiag(z, lo):
    z = z.astype(BF16)
    zero = jnp.zeros_like(z)
    return jnp.concatenate([jnp.where(lo, z, zero), jnp.where(lo, zero, z)], axis=0)


def _rwkv_scan_kernel(r_ref, lw_ref, k_ref, v_ref, kk_ref, b_ref, g_ref,
                      rk_ref, lnw_ref, lnb_ref, y_out, s_scr, *, nbatch, npairs):
    c = pl.program_id(2)

    @pl.when(c == 0)
    def _():
        s_scr[...] = jnp.zeros_like(s_scr)

    C = CHUNK
    t_i = lax.broadcasted_iota(jnp.int32, (C, C), 0)
    j_i = lax.broadcasted_iota(jnp.int32, (C, C), 1)
    ltri = (j_i <= t_i).astype(BF16)
    row = lax.broadcasted_iota(jnp.int32, (C, LANES), 0)
    lane = lax.broadcasted_iota(jnp.int32, (C, LANES), 1)
    lo = lane < HEAD_DIM
    col = jnp.bitwise_and(lane, HEAD_DIM - 1)
    strict = col < row
    incl = col <= row
    rr = lax.broadcasted_iota(jnp.int32, (LANES, LANES), 0)
    cc = lax.broadcasted_iota(jnp.int32, (LANES, LANES), 1)
    same_head = (rr < HEAD_DIM) == (cc < HEAD_DIM)
    bd = functools.partial(_blockdiag, lo=lo)
    cat0 = lambda *xs: jnp.concatenate([x.astype(BF16) for x in xs], axis=0)
    cat1 = lambda *xs: jnp.concatenate([x.astype(BF16) for x in xs], axis=1)

    units = [(bi, slice(p * LANES, (p + 1) * LANES)) for bi in range(nbatch) for p in range(npairs)]
    P = range(len(units))
    r = [r_ref[bi, :, sl].astype(F32) for bi, sl in units]
    lw = [lw_ref[bi, :, sl] for bi, sl in units]
    k = [k_ref[bi, :, sl].astype(F32) for bi, sl in units]
    v = [v_ref[bi, :, sl].astype(F32) for bi, sl in units]
    kk = [kk_ref[bi, :, sl].astype(F32) for bi, sl in units]
    b = [b_ref[bi, :, sl].astype(F32) for bi, sl in units]

    cw2 = [_dot(ltri, cat1(*_split(lw[p]))) for p in P]
    cw = [cw2[p][:, :LANES] + cw2[p][:, LANES:] for p in P]
    cwl = [cw[p][C - 1:C, :] for p in P]
    at = [-kk[p] * jnp.exp(cw[p] - lw[p]) for p in P]
    dinv = [jnp.exp(-cw[p]) for p in P]
    rt = [r[p] * jnp.exp(cw[p]) for p in P]
    dend = [jnp.exp(cwl[p] - cw[p]) for p in P]

    x = [_dot_nt(cat0(at[p], rt[p]), cat0(bd(b[p] * dinv[p]), bd(k[p] * dinv[p]))) for p in P]
    aab = [jnp.where(strict, x[p][:C, :LANES], 0.0) for p in P]
    arb = [jnp.where(incl, x[p][C:, :LANES], 0.0) for p in P]
    aak = [jnp.where(strict, x[p][:C, LANES:], 0.0) for p in P]
    ark = [jnp.where(incl, x[p][C:, LANES:], 0.0) for p in P]

    bdv = [bd(v[p]) for p in P]
    av = [_dot(aak[p].astype(BF16), bdv[p]) for p in P]
    eye = jnp.where(col == row, 1.0, 0.0)
    tinv = [eye + aab[p] for p in P]
    n = [_dot(aab[p].astype(BF16), bd(aab[p])) for p in P]
    for it in range(5):
        last = it == 4
        res = [_dot(n[p].astype(BF16),
                    jnp.concatenate([bd(tinv[p])] + ([] if last else [bd(n[p])]), axis=1)) for p in P]
        tinv = [tinv[p] + res[p][:, :LANES] for p in P]
        if not last:
            n = [res[p][:, LANES:] for p in P]
    z = [_dot(tinv[p].astype(BF16), jnp.concatenate([bd(at[p]), bd(av[p])], axis=1)) for p in P]
    z1 = [z[p][:, :LANES] for p in P]
    z2 = [z[p][:, LANES:] for p in P]

    s = [s_scr[p] for p in P]
    ws = [_dot_nt(cat0(z1[p], rt[p]), s[p].astype(BF16)) for p in P]
    u = [ws[p][:C] + z2[p] for p in P]
    y = [ws[p][C:] + _dot(cat1(arb[p], ark[p]), cat0(bd(u[p]), bdv[p])) for p in P]
    upd = [_dot_tn(cat0(u[p], v[p]), cat0(b[p] * dend[p], k[p] * dend[p])) for p in P]
    for p in P:
        s_scr[p] = s[p] * jnp.exp(cwl[p]) + jnp.where(same_head, upd[p], 0.0)

    for p in P:
        bi, sl = units[p]
        mean = _head_sum(y[p]) * (1.0 / HEAD_DIM)
        d = y[p] - mean
        var = _head_sum(d * d) * (1.0 / HEAD_DIM)
        yn = d * lax.rsqrt(var + GN_EPS)
        bonus = _head_sum(r[p] * k[p] * rk_ref[:, sl]) * v[p]
        out = (yn * lnw_ref[:, sl] + lnb_ref[:, sl] + bonus) * g_ref[bi, :, sl]
        y_out[bi, :, sl] = out.astype(y_out.dtype)


def _rwkv_scan(r, lw, k, v, kk, b, g, rk, lnw, lnb, *, nbatch, npairs):
    B, T, D = r.shape
    W = npairs * LANES
    tile = pl.BlockSpec((nbatch, CHUNK, W), lambda bi, p, c: (bi, c, p))
    vec = pl.BlockSpec((1, W), lambda bi, p, c: (0, p))
    return pl.pallas_call(
        functools.partial(_rwkv_scan_kernel, nbatch=nbatch, npairs=npairs),
        grid=(B // nbatch, D // W, T // CHUNK),
        in_specs=[tile] * 7 + [vec] * 3,
        out_specs=tile,
        out_shape=jax.ShapeDtypeStruct((B, T, D), BF16),
        scratch_shapes=[pltpu.VMEM((nbatch * npairs, LANES, LANES), F32)],
        compiler_params=_params(("parallel", "parallel", "arbitrary")),
        name="rwkv_scan",
    )(r, lw, k, v, kk, b, g, rk.reshape(1, D), lnw.reshape(1, D), lnb.reshape(1, D))


def _proj_mlp_kernel(res_ref, a_ref, wo_ref, g_ref, win_ref, wout_ref, o_ref, *, tf):
    x = res_ref[...] + _dot(a_ref[...], wo_ref[...])
    xn = _rms(x, g_ref[...]).astype(BF16)
    acc = x
    for f in range(win_ref.shape[1] // tf):
        hid = jnp.maximum(_dot(xn, win_ref[:, f * tf:(f + 1) * tf]), 0.0)
        acc = acc + _dot((hid * hid).astype(BF16), wout_ref[f * tf:(f + 1) * tf, :])
    o_ref[...] = acc


def _proj_mlp(res, a, w_o, g, w_in, w_out, *, tm, tf):
    M, D = res.shape
    tile = pl.BlockSpec((tm, D), lambda i: (i, 0))
    return pl.pallas_call(
        functools.partial(_proj_mlp_kernel, tf=tf),
        grid=(M // tm,),
        in_specs=[tile, tile, _const_spec(w_o.shape), _const_spec((1, D)),
                  _const_spec(w_in.shape), _const_spec(w_out.shape)],
        out_specs=tile,
        out_shape=jax.ShapeDtypeStruct((M, D), F32),
        compiler_params=_params(("parallel",)),
        name="proj_mlp",
    )(res, a, w_o, g.reshape(1, D), w_in, w_out)


def _q_proj_kernel(x_ref, g_ref, wt_ref, qg_ref, c_ref, q_out, *, scale):
    hn = _rms(x_ref[0], g_ref[...])
    qt = _dot_nt(wt_ref[...], hn.astype(BF16))
    tm = hn.shape[0]
    row = lax.broadcasted_iota(jnp.int32, (HEAD_DIM, tm), 0)
    for h in range(qt.shape[0] // HEAD_DIM):
        hs = slice(h * HEAD_DIM, (h + 1) * HEAD_DIM)
        qh = qt[hs, :]
        ms = jnp.mean(qh * qh, axis=0, keepdims=True)
        qn = qh * lax.rsqrt(ms + NORM_EPS) * (qg_ref[hs, :] * scale)
        hi, mid, lo = _split3(c_ref[0, h:h + 1, :] * LOG2E)
        aug = jnp.where(row == 0, hi, jnp.where(row == 1, mid, jnp.where(
            row == 2, lo, jnp.where(row < 6, 1.0, 0.0))))
        q_out[0, h] = jnp.concatenate([qn, aug], axis=0).astype(q_out.dtype)


def _q_proj(x, g, wt, qg, c_row, *, tm, scale):
    B, T, D = x.shape
    H = D // HEAD_DIM
    return pl.pallas_call(
        functools.partial(_q_proj_kernel, scale=scale),
        grid=(B, T // tm),
        in_specs=[pl.BlockSpec((1, tm, D), lambda b, i: (b, i, 0)), _const_spec((1, D)),
                  _const_spec(wt.shape), _const_spec((D, 1)),
                  pl.BlockSpec((1, H, tm), lambda b, i: (b, 0, i))],
        out_specs=pl.BlockSpec((1, H, LANES, tm), lambda b, i: (b, 0, 0, i)),
        out_shape=jax.ShapeDtypeStruct((B, H, LANES, T), BF16),
        compiler_params=_params(("parallel", "parallel")),
        name="q_proj",
    )(x, g.reshape(1, D), wt, qg.reshape(D, 1), c_row)


def _shared_kv_kernel(x_ref, g_ref, wk_ref, wvt_ref, wf_ref, fb_ref, kg_ref,
                      k_out, vt_out, c_out, carry_scr, *, tq):
    i = pl.program_id(1)

    @pl.when(i == 0)
    def _():
        carry_scr[...] = jnp.zeros_like(carry_scr)

    hn = _rms(x_ref[0], g_ref[...])
    hb = hn.astype(BF16)
    tm, D = hn.shape
    H = D // HEAD_DIM
    k = _head_rms(_dot(hb, wk_ref[...]), kg_ref[...])
    vt = _dot_nt(wvt_ref[...], hb)

    f = _dot(hb, wf_ref[...]) + fb_ref[...]
    logf = jnp.minimum(f, 0.0) - jnp.log(1.0 + jnp.exp(-jnp.abs(f)))
    t_i = lax.broadcasted_iota(jnp.int32, (tm, tm), 0)
    j_i = lax.broadcasted_iota(jnp.int32, (tm, tm), 1)
    ltri = (j_i <= t_i).astype(BF16)
    c3 = _dot(ltri, jnp.concatenate([t.astype(BF16) for t in _split3(logf)], axis=1))
    c = (c3[:, :LANES] + c3[:, LANES:2 * LANES]) + c3[:, 2 * LANES:] + carry_scr[0:1, :]
    carry_scr[...] = jnp.broadcast_to(c[tm - 1:tm, :], carry_scr.shape)
    c_out[0] = c.T[:H, :]

    nck = c * (-LOG2E)
    lane = lax.broadcasted_iota(jnp.int32, (tm, LANES), 1)
    vrow = lax.broadcasted_iota(jnp.int32, (V_ROWS - HEAD_DIM, tq), 0)
    ones_row = jnp.where(vrow == 0, 1.0, 0.0)
    for h in range(H):
        base = k[:, (h // 2) * LANES:(h // 2 + 1) * LANES]
        if h % 2:
            base = pltpu.roll(base, HEAD_DIM, 1)
        hi, mid, lo = _split3(nck[:, h:h + 1])
        tile = jnp.where(lane < HEAD_DIM, base, jnp.where(lane < HEAD_DIM + 3, 1.0, jnp.where(
            lane == HEAD_DIM + 3, hi, jnp.where(lane == HEAD_DIM + 4, mid, jnp.where(
                lane == HEAD_DIM + 5, lo, 0.0)))))
        k_out[0, h] = tile.astype(k_out.dtype)
        for sb in range(tm // tq):
            vt_out[0, h, sb] = jnp.concatenate(
                [vt[h * HEAD_DIM:(h + 1) * HEAD_DIM, sb * tq:(sb + 1) * tq], ones_row],
                axis=0).astype(vt_out.dtype)


def _shared_kv(x, g, wk, wvt, wf, fb, kg, *, tm, tq):
    B, T, D = x.shape
    H = D // HEAD_DIM
    return pl.pallas_call(
        functools.partial(_shared_kv_kernel, tq=tq),
        grid=(B, T // tm),
        in_specs=[pl.BlockSpec((1, tm, D), lambda b, i: (b, i, 0)), _const_spec((1, D)),
                  _const_spec(wk.shape), _const_spec(wvt.shape), _const_spec(wf.shape),
                  _const_spec((1, LANES)), _const_spec((1, D))],
        out_specs=[pl.BlockSpec((1, H, tm, LANES), lambda b, i: (b, 0, i, 0)),
                   pl.BlockSpec((1, H, tm // tq, V_ROWS, tq), lambda b, i: (b, 0, i, 0, 0)),
                   pl.BlockSpec((1, H, tm), lambda b, i: (b, 0, i))],
        out_shape=[jax.ShapeDtypeStruct((B, H, T, LANES), BF16),
                   jax.ShapeDtypeStruct((B, H, T // tq, V_ROWS, tq), BF16),
                   jax.ShapeDtypeStruct((B, H, T), F32)],
        scratch_shapes=[pltpu.VMEM((8, LANES), F32)],
        compiler_params=_params(("parallel", "arbitrary")),
        name="shared_kv",
    )(x, g.reshape(1, D), wk, wvt, wf, fb, kg.reshape(1, D))


def _fox_attn_kernel(q_ref, k_ref, vt_ref, o_ref, acc_scr, *, tq, nh):
    i = pl.program_id(2)
    qt = [q_ref[0, h] for h in range(nh)]
    acc_scr[...] = jnp.zeros_like(acc_scr)
    key_i = lax.broadcasted_iota(jnp.int32, (tq, tq), 0)
    qry_i = lax.broadcasted_iota(jnp.int32, (tq, tq), 1)
    causal = key_i <= qry_i

    def step(j, m, masked):
        off = pl.multiple_of(j * tq, tq)
        s = [_dot(k_ref[0, h, pl.ds(off, tq), :], qt[h]) for h in range(nh)]
        if masked:
            s = [jnp.where(causal, s[h], NEG_BIG) for h in range(nh)]
        m_new = [jnp.maximum(m[h], jnp.max(s[h], axis=0, keepdims=True)) for h in range(nh)]
        p = [jnp.exp2(s[h] - m_new[h]).astype(BF16) for h in range(nh)]
        alpha = [jnp.exp2(m[h] - m_new[h]) for h in range(nh)]
        pv = [_dot(vt_ref[0, h, j], p[h]) for h in range(nh)]
        for h in range(nh):
            acc_scr[h] = alpha[h] * acc_scr[h] + pv[h]
        return tuple(m_new)

    m0 = tuple(jnp.full((1, tq), NEG_BIG, F32) for _ in range(nh))
    m = lax.fori_loop(0, i, functools.partial(step, masked=False), m0)
    step(i, m, True)
    ot = [acc_scr[h, :HEAD_DIM, :] * (1.0 / acc_scr[h, HEAD_DIM:HEAD_DIM + 1, :]) for h in range(nh)]
    o_ref[0] = jnp.concatenate(ot, axis=0).T.astype(o_ref.dtype)


def _fox_attn(qt, ka, vt, *, tq, nh):
    B, H, _, T = qt.shape
    nb = T // tq
    return pl.pallas_call(
        functools.partial(_fox_attn_kernel, tq=tq, nh=nh),
        grid=(B, H // nh, nb),
        in_specs=[pl.BlockSpec((1, nh, LANES, tq), lambda b, p, i: (b, p, 0, i)),
                  pl.BlockSpec((1, nh, T, LANES), lambda b, p, i: (b, p, 0, 0)),
                  pl.BlockSpec((1, nh, nb, V_ROWS, tq), lambda b, p, i: (b, p, 0, 0, 0))],
        out_specs=pl.BlockSpec((1, tq, nh * HEAD_DIM), lambda b, p, i: (b, i, p)),
        out_shape=jax.ShapeDtypeStruct((B, T, H * HEAD_DIM), BF16),
        scratch_shapes=[pltpu.VMEM((nh, V_ROWS, tq), F32)],
        compiler_params=_params(("parallel", "parallel", "arbitrary")),
        name="fox_attn",
    )(qt, ka, vt)


def kernel(x, rwkv_norm_g, rwkv_mu, rwkv_w_rkv, rwkv_w0, rwkv_w1, rwkv_w2, rwkv_a0, rwkv_a1, rwkv_a2, rwkv_g1, rwkv_g2, rwkv_k_k, rwkv_k_a, rwkv_r_k, rwkv_lnx_w, rwkv_lnx_b, rwkv_w_o, kv_norm_g, kv_w, kv_f_bias, k_norm_g, attn_norm_g, attn_w_q, q_norm_g, attn_w_o, mlp_norm_g, mlp_w_in, mlp_w_out):
    B, T, D = x.shape
    M = B * T
    n_a = rwkv_norm_g.shape[0]
    depth = mlp_norm_g.shape[0]
    bf = lambda w: w.astype(BF16)
    tm = min(512, T)
    tq = min(256, T)
    tf = min(1024, mlp_w_in.shape[-1])
    npairs = D // LANES

    k_sh = v_sh = c_sh = None
    for layer in range(depth):
        if layer < n_a:
            i = layer
            r, lw, k, v, kk, b, g = _rwkv_prep(
                x, rwkv_norm_g[i], rwkv_mu[i], bf(rwkv_w_rkv[i, 0]), bf(rwkv_w_rkv[i, 1]),
                bf(rwkv_w_rkv[i, 2]), rwkv_w0[i], bf(rwkv_w1[i]), bf(rwkv_w2[i]), rwkv_a0[i],
                bf(rwkv_a1[i]), bf(rwkv_a2[i]), bf(rwkv_g1[i]), bf(rwkv_g2[i]),
                rwkv_k_k[i], rwkv_k_a[i], tm=tm)
            mix = _rwkv_scan(r, lw, k, v, kk, b, g, rwkv_r_k[i], rwkv_lnx_w[i], rwkv_lnx_b[i],
                             nbatch=2 if B % 2 == 0 else 1, npairs=npairs)
            w_o = rwkv_w_o[i]
        else:
            j = layer - n_a
            qt = _q_proj(x, attn_norm_g[j], bf(attn_w_q[j].T), q_norm_g[j], c_sh,
                         tm=tm, scale=HEAD_DIM ** -0.5 * LOG2E)
            mix = _fox_attn(qt, k_sh, v_sh, tq=tq, nh=min(16, D // HEAD_DIM))
            w_o = attn_w_o[j]
        x = _proj_mlp(x.reshape(M, D), mix.reshape(M, D), bf(w_o), mlp_norm_g[layer],
                      bf(mlp_w_in[layer]), bf(mlp_w_out[layer]), tm=tm, tf=tf).reshape(B, T, D)
        if layer == n_a - 1:
            wf = jnp.pad(kv_w[:, 2 * D:], ((0, 0), (0, LANES - (kv_w.shape[1] - 2 * D))))
            fb = jnp.pad(kv_f_bias, (0, LANES - kv_f_bias.shape[0])).reshape(1, LANES)
            k_sh, v_sh, c_sh = _shared_kv(
                x, kv_norm_g, bf(kv_w[:, :D]), bf(kv_w[:, D:2 * D].T), bf(wf), fb, k_norm_g, tm=tm, tq=tq)
    return x
```

```python
import functools

import jax
import jax.numpy as jnp
import numpy as np
from jax import lax
from jax.experimental import pallas as pl
from jax.experimental.pallas import tpu as pltpu

HEAD_DIM = 64
LANES = 128
NORM_EPS = 1e-6
GN_EPS = 64e-5
CHUNK = 64
NEG_BIG = -1e30
LOG2E = 1.4426950408889634
EXP_M_HALF = 0.6065306597126334
V_ROWS = 80
VMEM_LIMIT = 56 * 1024 * 1024

BF16 = jnp.bfloat16
F32 = jnp.float32

_NT = (((1,), (1,)), ((), ()))
_TN = (((0,), (0,)), ((), ()))


def _dot(a, b):
    return jnp.dot(a, b, preferred_element_type=F32)


def _dot_nt(a, b):
    return lax.dot_general(a, b, _NT, preferred_element_type=F32)


def _dot_tn(a, b):
    return lax.dot_general(a, b, _TN, preferred_element_type=F32)


def _split(a):
    hi = a.astype(BF16)
    return hi, (a - hi.astype(F32)).astype(BF16)


def _split3(a):
    hi = a.astype(BF16).astype(F32)
    r1 = a - hi
    mid = r1.astype(BF16).astype(F32)
    return hi, mid, r1 - mid


def _rms(x, g):
    return x * lax.rsqrt(jnp.mean(x * x, axis=-1, keepdims=True) + NORM_EPS) * g


def _head_sum(x):
    outs = []
    for c in range(x.shape[1] // LANES):
        xc = x[:, c * LANES:(c + 1) * LANES]
        lo = lax.broadcasted_iota(jnp.int32, xc.shape, 1) < HEAD_DIM
        s0 = jnp.sum(jnp.where(lo, xc, 0.0), axis=1, keepdims=True)
        s1 = jnp.sum(jnp.where(lo, 0.0, xc), axis=1, keepdims=True)
        outs.append(jnp.where(lo, s0, s1))
    return outs[0] if len(outs) == 1 else jnp.concatenate(outs, axis=1)


def _head_rms(t, g):
    ms = _head_sum(t * t) * (1.0 / HEAD_DIM)
    return t * lax.rsqrt(ms + NORM_EPS) * g


def _sigmoid(z):
    return 1.0 / (1.0 + jnp.exp(-z))


def _const_spec(shape):
    nd = len(shape)
    return pl.BlockSpec(shape, lambda *_: (0,) * nd)


def _params(sem):
    return pltpu.CompilerParams(dimension_semantics=sem, vmem_limit_bytes=VMEM_LIMIT)


def _rwkv_prep_kernel(x_ref, xp_ref, ng_ref, mu_ref, wr_ref, wk_ref, wv_ref,
                      w0_ref, w1_ref, w2_ref, a0_ref, a1_ref, a2_ref, g1_ref, g2_ref,
                      kkw_ref, kaw_ref,
                      r_out, lw_out, k_out, v_out, kk_out, b_out, g_out):
    i = pl.program_id(1)
    ng = ng_ref[...]
    h = _rms(x_ref[0], ng)
    hp = _rms(xp_ref[0][7:8, :], ng)
    hp = jnp.where(i > 0, hp, 0.0)
    rolled = pltpu.roll(h, 1, 0)
    first = jnp.where(lax.broadcasted_iota(jnp.int32, (8, h.shape[1]), 0) == 0, hp, rolled[:8])
    hs = jnp.concatenate([first, rolled[8:]], axis=0)
    hb = h.astype(BF16)
    xxb = (hs - h).astype(BF16)
    mub = mu_ref[...].astype(BF16)

    def mix(j):
        return hb + xxb * mub[j:j + 1, :]

    tw = _dot(mix(1), w1_ref[...])
    ta = _dot(mix(4), a1_ref[...])
    tg = _dot(mix(5), g1_ref[...])
    k = _dot(mix(2), wk_ref[...])
    wl = _dot(jnp.tanh(tw).astype(BF16), w2_ref[...])
    al = _dot(ta.astype(BF16), a2_ref[...])
    g_out[0] = _dot(_sigmoid(tg).astype(BF16), g2_ref[...]).astype(g_out.dtype)

    lw_out[0] = -EXP_M_HALF * _sigmoid(w0_ref[...] + wl)
    a = _sigmoid(a0_ref[...] + al)
    kk = k * kkw_ref[...]
    kk = kk * lax.rsqrt(jnp.maximum(_head_sum(kk * kk), 1e-24))
    k_out[0] = (k * (1.0 + (a - 1.0) * kaw_ref[...])).astype(k_out.dtype)
    kk_out[0] = kk.astype(kk_out.dtype)
    b_out[0] = (kk * a).astype(b_out.dtype)

    r_out[0] = _dot(mix(0), wr_ref[...]).astype(r_out.dtype)
    v_out[0] = _dot(mix(3), wv_ref[...]).astype(v_out.dtype)


def _rwkv_prep(x, ng, mu, wr, wk, wv, w0, w1, w2, a0, a1, a2, g1, g2, kkw, kaw, *, tm):
    B, T, D = x.shape
    row = lambda a: a.reshape(1, D)
    consts = [row(ng), mu, wr, wk, wv, row(w0), w1, w2, row(a0), a1, a2, g1, g2, row(kkw), row(kaw)]
    tile = pl.BlockSpec((1, tm, D), lambda b, i: (b, i, 0))
    prev = pl.BlockSpec((1, 8, D), lambda b, i: (b, jnp.maximum(i * (tm // 8) - 1, 0), 0))
    out = lambda dt: jax.ShapeDtypeStruct((B, T, D), dt)
    return pl.pallas_call(
        _rwkv_prep_kernel,
        grid=(B, T // tm),
        in_specs=[tile, prev] + [_const_spec(c.shape) for c in consts],
        out_specs=[tile] * 7,
        out_shape=[out(BF16), out(F32)] + [out(BF16)] * 5,
        compiler_params=_params(("parallel", "parallel")),
        name="rwkv_prep",
    )(x, x, *consts)


def _blockdiag(z, lo):
    z = z.astype(BF16)
    zero = jnp.zeros_like(z)
    return jnp.concatenate([jnp.where(lo, z, zero), jnp.where(lo, zero, z)], axis=0)


def _rwkv_scan_kernel(r_ref, lw_ref, k_ref, v_ref, kk_ref, b_ref, g_ref,
                      rk_ref, lnw_ref, lnb_ref, y_out, s_scr, *, nbatch, npairs):
    c = pl.program_id(2)

    @pl.when(c == 0)
    def _():
        s_scr[...] = jnp.zeros_like(s_scr)

    C = CHUNK
    t_i = lax.broadcasted_iota(jnp.int32, (C, C), 0)
    j_i = lax.broadcasted_iota(jnp.int32, (C, C), 1)
    ltri = (j_i <= t_i).astype(BF16)
    row = lax.broadcasted_iota(jnp.int32, (C, LANES), 0)
    lane = lax.broadcasted_iota(jnp.int32, (C, LANES), 1)
    lo = lane < HEAD_DIM
    col = jnp.bitwise_and(lane, HEAD_DIM - 1)
    strict = col < row
    incl = col <= row
    rr = lax.broadcasted_iota(jnp.int32, (LANES, LANES), 0)
    cc = lax.broadcasted_iota(jnp.int32, (LANES, LANES), 1)
    same_head = (rr < HEAD_DIM) == (cc < HEAD_DIM)
    bd = functools.partial(_blockdiag, lo=lo)
    cat0 = lambda *xs: jnp.concatenate([x.astype(BF16) for x in xs], axis=0)
    cat1 = lambda *xs: jnp.concatenate([x.astype(BF16) for x in xs], axis=1)

    units = [(bi, slice(p * LANES, (p + 1) * LANES)) for bi in range(nbatch) for p in range(npairs)]
    P = range(len(units))
    r = [r_ref[bi, :, sl].astype(F32) for bi, sl in units]
    lw = [lw_ref[bi, :, sl] for bi, sl in units]
    k = [k_ref[bi, :, sl].astype(F32) for bi, sl in units]
    v = [v_ref[bi, :, sl].astype(F32) for bi, sl in units]
    kk = [kk_ref[bi, :, sl].astype(F32) for bi, sl in units]
    b = [b_ref[bi, :, sl].astype(F32) for bi, sl in units]

    cw2 = [_dot(ltri, cat1(*_split(lw[p]))) for p in P]
    cw = [cw2[p][:, :LANES] + cw2[p][:, LANES:] for p in P]
    cwl = [cw[p][C - 1:C, :] for p in P]
    at = [-kk[p] * jnp.exp(cw[p] - lw[p]) for p in P]
    dinv = [jnp.exp(-cw[p]) for p in P]
    rt = [r[p] * jnp.exp(cw[p]) for p in P]
    dend = [jnp.exp(cwl[p] - cw[p]) for p in P]

    x = [_dot_nt(cat0(at[p], rt[p]), cat0(bd(b[p] * dinv[p]), bd(k[p] * dinv[p]))) for p in P]
    aab = [jnp.where(strict, x[p][:C, :LANES], 0.0) for p in P]
    arb = [jnp.where(incl, x[p][C:, :LANES], 0.0) for p in P]
    aak = [jnp.where(strict, x[p][:C, LANES:], 0.0) for p in P]
    ark = [jnp.where(incl, x[p][C:, LANES:], 0.0) for p in P]

    bdv = [bd(v[p]) for p in P]
    av = [_dot(aak[p].astype(BF16), bdv[p]) for p in P]
    eye = jnp.where(col == row, 1.0, 0.0)
    tinv = [eye + aab[p] for p in P]
    n = [_dot(aab[p].astype(BF16), bd(aab[p])) for p in P]
    for it in range(5):
        last = it == 4
        res = [_dot(n[p].astype(BF16),
                    jnp.concatenate([bd(tinv[p])] + ([] if last else [bd(n[p])]), axis=1)) for p in P]
        tinv = [tinv[p] + res[p][:, :LANES] for p in P]
        if not last:
            n = [res[p][:, LANES:] for p in P]
    z = [_dot(tinv[p].astype(BF16), jnp.concatenate([bd(at[p]), bd(av[p])], axis=1)) for p in P]
    z1 = [z[p][:, :LANES] for p in P]
    z2 = [z[p][:, LANES:] for p in P]

    s = [s_scr[p] for p in P]
    ws = [_dot_nt(cat0(z1[p], rt[p]), s[p].astype(BF16)) for p in P]
    u = [ws[p][:C] + z2[p] for p in P]
    y = [ws[p][C:] + _dot(cat1(arb[p], ark[p]), cat0(bd(u[p]), bdv[p])) for p in P]
    upd = [_dot_tn(cat0(u[p], v[p]), cat0(b[p] * dend[p], k[p] * dend[p])) for p in P]
    for p in P:
        s_scr[p] = s[p] * jnp.exp(cwl[p]) + jnp.where(same_head, upd[p], 0.0)

    for p in P:
        bi, sl = units[p]
        mean = _head_sum(y[p]) * (1.0 / HEAD_DIM)
        d = y[p] - mean
        var = _head_sum(d * d) * (1.0 / HEAD_DIM)
        yn = d * lax.rsqrt(var + GN_EPS)
        bonus = _head_sum(r[p] * k[p] * rk_ref[:, sl]) * v[p]
        out = (yn * lnw_ref[:, sl] + lnb_ref[:, sl] + bonus) * g_ref[bi, :, sl]
        y_out[bi, :, sl] = out.astype(y_out.dtype)


def _rwkv_scan(r, lw, k, v, kk, b, g, rk, lnw, lnb, *, nbatch, npairs):
    B, T, D = r.shape
    W = npairs * LANES
    tile = pl.BlockSpec((nbatch, CHUNK, W), lambda bi, p, c: (bi, c, p))
    vec = pl.BlockSpec((1, W), lambda bi, p, c: (0, p))
    return pl.pallas_call(
        functools.partial(_rwkv_scan_kernel, nbatch=nbatch, npairs=npairs),
        grid=(B // nbatch, D // W, T // CHUNK),
        in_specs=[tile] * 7 + [vec] * 3,
        out_specs=tile,
        out_shape=jax.ShapeDtypeStruct((B, T, D), BF16),
        scratch_shapes=[pltpu.VMEM((nbatch * npairs, LANES, LANES), F32)],
        compiler_params=_params(("parallel", "parallel", "arbitrary")),
        name="rwkv_scan",
    )(r, lw, k, v, kk, b, g, rk.reshape(1, D), lnw.reshape(1, D), lnb.reshape(1, D))


def _proj_mlp_kernel(res_ref, a_ref, wo_ref, g_ref, win_ref, wout_ref, o_ref, *, tf):
    x = res_ref[...] + _dot(a_ref[...], wo_ref[...])
    xn = _rms(x, g_ref[...]).astype(BF16)
    acc = x
    for f in range(win_ref.shape[1] // tf):
        hid = jnp.maximum(_dot(xn, win_ref[:, f * tf:(f + 1) * tf]), 0.0)
        acc = acc + _dot((hid * hid).astype(BF16), wout_ref[f * tf:(f + 1) * tf, :])
    o_ref[...] = acc


def _proj_mlp(res, a, w_o, g, w_in, w_out, *, tm, tf):
    M, D = res.shape
    tile = pl.BlockSpec((tm, D), lambda i: (i, 0))
    return pl.pallas_call(
        functools.partial(_proj_mlp_kernel, tf=tf),
        grid=(M // tm,),
        in_specs=[tile, tile, _const_spec(w_o.shape), _const_spec((1, D)),
                  _const_spec(w_in.shape), _const_spec(w_out.shape)],
        out_specs=tile,
        out_shape=jax.ShapeDtypeStruct((M, D), F32),
        compiler_params=_params(("parallel",)),
        name="proj_mlp",
    )(res, a, w_o, g.reshape(1, D), w_in, w_out)


def _q_proj_kernel(x_ref, g_ref, wt_ref, qg_ref, c_ref, q_out, *, scale):
    hn = _rms(x_ref[0], g_ref[...])
    qt = _dot_nt(wt_ref[...], hn.astype(BF16))
    tm = hn.shape[0]
    row = lax.broadcasted_iota(jnp.int32, (HEAD_DIM, tm), 0)
    for h in range(qt.shape[0] // HEAD_DIM):
        hs = slice(h * HEAD_DIM, (h + 1) * HEAD_DIM)
        qh = qt[hs, :]
        ms = jnp.mean(qh * qh, axis=0, keepdims=True)
        qn = qh * lax.rsqrt(ms + NORM_EPS) * (qg_ref[hs, :] * scale)
        hi, mid, lo = _split3(c_ref[0, h:h + 1, :] * LOG2E)
        aug = jnp.where(row == 0, hi, jnp.where(row == 1, mid, jnp.where(
            row == 2, lo, jnp.where(row < 6, 1.0, 0.0))))
        q_out[0, h] = jnp.concatenate([qn, aug], axis=0).astype(q_out.dtype)


def _q_proj(x, g, wt, qg, c_row, *, tm, scale):
    B, T, D = x.shape
    H = D // HEAD_DIM
    return pl.pallas_call(
        functools.partial(_q_proj_kernel, scale=scale),
        grid=(B, T // tm),
        in_specs=[pl.BlockSpec((1, tm, D), lambda b, i: (b, i, 0)), _const_spec((1, D)),
                  _const_spec(wt.shape), _const_spec((D, 1)),
                  pl.BlockSpec((1, H, tm), lambda b, i: (b, 0, i))],
        out_specs=pl.BlockSpec((1, H, LANES, tm), lambda b, i: (b, 0, 0, i)),
        out_shape=jax.ShapeDtypeStruct((B, H, LANES, T), BF16),
        compiler_params=_params(("parallel", "parallel")),
        name="q_proj",
    )(x, g.reshape(1, D), wt, qg.reshape(D, 1), c_row)


def _shared_kv_kernel(x_ref, g_ref, wk_ref, wvt_ref, wf_ref, fb_ref, kg_ref, sel_ref,
                      k_out, vt_out, c_out, carry_scr, *, tq):
    i = pl.program_id(1)

    @pl.when(i == 0)
    def _():
        carry_scr[...] = jnp.zeros_like(carry_scr)

    hn = _rms(x_ref[0], g_ref[...])
    hb = hn.astype(BF16)
    tm, D = hn.shape
    H = D // HEAD_DIM
    k = _head_rms(_dot(hb, wk_ref[...]), kg_ref[...])
    vt = _dot_nt(wvt_ref[...], hb)

    f = _dot(hb, wf_ref[...]) + fb_ref[...]
    logf = jnp.minimum(f, 0.0) - jnp.log(1.0 + jnp.exp(-jnp.abs(f)))
    t_i = lax.broadcasted_iota(jnp.int32, (tm, tm), 0)
    j_i = lax.broadcasted_iota(jnp.int32, (tm, tm), 1)
    ltri = (j_i <= t_i).astype(BF16)
    c3 = _dot(ltri, jnp.concatenate([t.astype(BF16) for t in _split3(logf)], axis=1))
    c = (c3[:, :LANES] + c3[:, LANES:2 * LANES]) + c3[:, 2 * LANES:] + carry_scr[0:1, :]
    carry_scr[...] = jnp.broadcast_to(c[tm - 1:tm, :], carry_scr.shape)
    c_out[0] = c.T[:H, :]

    lane = lax.broadcasted_iota(jnp.int32, (tm, LANES), 1)
    hi, mid, lo = (jnp.where(lane < H, t, 0.0) for t in _split3(c * (-LOG2E)))
    packed = (hi + pltpu.roll(mid, H, 1)) + (pltpu.roll(lo, 2 * H, 1) + jnp.where(lane == 3 * H, 1.0, 0.0))
    aug = _dot(packed.astype(BF16), sel_ref[...])
    vrow = lax.broadcasted_iota(jnp.int32, (V_ROWS - HEAD_DIM, tq), 0)
    ones_row = jnp.where(vrow == 0, 1.0, 0.0)
    for h in range(H):
        base = k[:, (h // 2) * LANES:(h // 2 + 1) * LANES]
        if h % 2:
            base = pltpu.roll(base, HEAD_DIM, 1)
        tile = jnp.where(lane < HEAD_DIM, base, aug[:, h * LANES:(h + 1) * LANES])
        k_out[0, h] = tile.astype(k_out.dtype)
        for sb in range(tm // tq):
            vt_out[0, h, sb] = jnp.concatenate(
                [vt[h * HEAD_DIM:(h + 1) * HEAD_DIM, sb * tq:(sb + 1) * tq], ones_row],
                axis=0).astype(vt_out.dtype)


def _bias_selector(H):
    assert 3 * H + 1 <= LANES
    sel = np.zeros((LANES, H * LANES), np.float32)
    for h in range(H):
        sel[3 * H, h * LANES + HEAD_DIM:h * LANES + HEAD_DIM + 3] = 1.0
        for t in range(3):
            sel[t * H + h, h * LANES + HEAD_DIM + 3 + t] = 1.0
    return jnp.asarray(sel, BF16)


def _shared_kv(x, g, wk, wvt, wf, fb, kg, *, tm, tq):
    B, T, D = x.shape
    H = D // HEAD_DIM
    sel = _bias_selector(H)
    return pl.pallas_call(
        functools.partial(_shared_kv_kernel, tq=tq),
        grid=(B, T // tm),
        in_specs=[pl.BlockSpec((1, tm, D), lambda b, i: (b, i, 0)), _const_spec((1, D)),
                  _const_spec(wk.shape), _const_spec(wvt.shape), _const_spec(wf.shape),
                  _const_spec((1, LANES)), _const_spec((1, D)), _const_spec(sel.shape)],
        out_specs=[pl.BlockSpec((1, H, tm, LANES), lambda b, i: (b, 0, i, 0)),
                   pl.BlockSpec((1, H, tm // tq, V_ROWS, tq), lambda b, i: (b, 0, i, 0, 0)),
                   pl.BlockSpec((1, H, tm), lambda b, i: (b, 0, i))],
        out_shape=[jax.ShapeDtypeStruct((B, H, T, LANES), BF16),
                   jax.ShapeDtypeStruct((B, H, T // tq, V_ROWS, tq), BF16),
                   jax.ShapeDtypeStruct((B, H, T), F32)],
        scratch_shapes=[pltpu.VMEM((8, LANES), F32)],
        compiler_params=_params(("parallel", "arbitrary")),
        name="shared_kv",
    )(x, g.reshape(1, D), wk, wvt, wf, fb, kg.reshape(1, D), sel)


def _fox_attn_kernel(q_ref, k_ref, vt_ref, o_ref, acc_scr, *, tq, nh):
    i = pl.program_id(2)
    qt = [q_ref[0, h] for h in range(nh)]
    acc_scr[...] = jnp.zeros_like(acc_scr)
    key_i = lax.broadcasted_iota(jnp.int32, (tq, tq), 0)
    qry_i = lax.broadcasted_iota(jnp.int32, (tq, tq), 1)
    causal = key_i <= qry_i

    def step(j, m, masked):
        off = pl.multiple_of(j * tq, tq)
        s = [_dot(k_ref[0, h, pl.ds(off, tq), :], qt[h]) for h in range(nh)]
        if masked:
            s = [jnp.where(causal, s[h], NEG_BIG) for h in range(nh)]
        m_new = [jnp.maximum(m[h], jnp.max(s[h], axis=0, keepdims=True)) for h in range(nh)]
        p = [jnp.exp2(s[h] - m_new[h]).astype(BF16) for h in range(nh)]
        alpha = [jnp.exp2(m[h] - m_new[h]) for h in range(nh)]
        pv = [_dot(vt_ref[0, h, j], p[h]) for h in range(nh)]
        for h in range(nh):
            acc_scr[h] = alpha[h] * acc_scr[h] + pv[h]
        return tuple(m_new)

    m0 = tuple(jnp.full((1, tq), NEG_BIG, F32) for _ in range(nh))
    m = lax.fori_loop(0, i, functools.partial(step, masked=False), m0)
    step(i, m, True)
    ot = [acc_scr[h, :HEAD_DIM, :] * (1.0 / acc_scr[h, HEAD_DIM:HEAD_DIM + 1, :]) for h in range(nh)]
    o_ref[0] = jnp.concatenate(ot, axis=0).T.astype(o_ref.dtype)


def _fox_attn(qt, ka, vt, *, tq, nh):
    B, H, _, T = qt.shape
    nb = T // tq
    return pl.pallas_call(
        functools.partial(_fox_attn_kernel, tq=tq, nh=nh),
        grid=(B, H // nh, nb),
        in_specs=[pl.BlockSpec((1, nh, LANES, tq), lambda b, p, i: (b, p, 0, i)),
                  pl.BlockSpec((1, nh, T, LANES), lambda b, p, i: (b, p, 0, 0)),
                  pl.BlockSpec((1, nh, nb, V_ROWS, tq), lambda b, p, i: (b, p, 0, 0, 0))],
        out_specs=pl.BlockSpec((1, tq, nh * HEAD_DIM), lambda b, p, i: (b, i, p)),
        out_shape=jax.ShapeDtypeStruct((B, T, H * HEAD_DIM), BF16),
        scratch_shapes=[pltpu.VMEM((nh, V_ROWS, tq), F32)],
        compiler_params=_params(("parallel", "parallel", "arbitrary")),
        name="fox_attn",
    )(qt, ka, vt)


def kernel(x, rwkv_norm_g, rwkv_mu, rwkv_w_rkv, rwkv_w0, rwkv_w1, rwkv_w2, rwkv_a0, rwkv_a1, rwkv_a2, rwkv_g1, rwkv_g2, rwkv_k_k, rwkv_k_a, rwkv_r_k, rwkv_lnx_w, rwkv_lnx_b, rwkv_w_o, kv_norm_g, kv_w, kv_f_bias, k_norm_g, attn_norm_g, attn_w_q, q_norm_g, attn_w_o, mlp_norm_g, mlp_w_in, mlp_w_out):
    B, T, D = x.shape
    M = B * T
    n_a = rwkv_norm_g.shape[0]
    depth = mlp_norm_g.shape[0]
    bf = lambda w: w.astype(BF16)
    tm = min(512, T)
    tq = min(256, T)
    tf = min(1024, mlp_w_in.shape[-1])
    npairs = D // LANES

    k_sh = v_sh = c_sh = None
    for layer in range(depth):
        if layer < n_a:
            i = layer
            r, lw, k, v, kk, b, g = _rwkv_prep(
                x, rwkv_norm_g[i], rwkv_mu[i], bf(rwkv_w_rkv[i, 0]), bf(rwkv_w_rkv[i, 1]),
                bf(rwkv_w_rkv[i, 2]), rwkv_w0[i], bf(rwkv_w1[i]), bf(rwkv_w2[i]), rwkv_a0[i],
                bf(rwkv_a1[i]), bf(rwkv_a2[i]), bf(rwkv_g1[i]), bf(rwkv_g2[i]),
                rwkv_k_k[i], rwkv_k_a[i], tm=tm)
            mix = _rwkv_scan(r, lw, k, v, kk, b, g, rwkv_r_k[i], rwkv_lnx_w[i], rwkv_lnx_b[i],
                             nbatch=2 if B % 2 == 0 else 1, npairs=npairs)
            w_o = rwkv_w_o[i]
        else:
            j = layer - n_a
            qt = _q_proj(x, attn_norm_g[j], bf(attn_w_q[j].T), q_norm_g[j], c_sh,
                         tm=tm, scale=HEAD_DIM ** -0.5 * LOG2E)
            mix = _fox_attn(qt, k_sh, v_sh, tq=tq, nh=min(16, D // HEAD_DIM))
            w_o = attn_w_o[j]
        x = _proj_mlp(x.reshape(M, D), mix.reshape(M, D), bf(w_o), mlp_norm_g[layer],
                      bf(mlp_w_in[layer]), bf(mlp_w_out[layer]), tm=tm, tf=tf).reshape(B, T, D)
        if layer == n_a - 1:
            wf = jnp.pad(kv_w[:, 2 * D:], ((0, 0), (0, LANES - (kv_w.shape[1] - 2 * D))))
            fb = jnp.pad(kv_f_bias, (0, LANES - kv_f_bias.shape[0])).reshape(1, LANES)
            k_sh, v_sh, c_sh = _shared_kv(
                x, kv_norm_g, bf(kv_w[:, :D]), bf(kv_w[:, D:2 * D].T), bf(wf), fb, k_norm_g, tm=tm, tq=tq)
    return x
```

```python
import functools

import jax
import jax.numpy as jnp
import numpy as np
from jax import lax
from jax.experimental import pallas as pl
from jax.experimental.pallas import tpu as pltpu

HEAD_DIM = 64
LANES = 128
NORM_EPS = 1e-6
GN_EPS = 64e-5
CHUNK = 64
NEG_BIG = -1e30
LOG2E = 1.4426950408889634
EXP_M_HALF = 0.6065306597126334
V_ROWS = 80
VMEM_LIMIT = 56 * 1024 * 1024

BF16 = jnp.bfloat16
F32 = jnp.float32

_NT = (((1,), (1,)), ((), ()))
_TN = (((0,), (0,)), ((), ()))


def _dot(a, b):
    return jnp.dot(a, b, preferred_element_type=F32)


def _dot_nt(a, b):
    return lax.dot_general(a, b, _NT, preferred_element_type=F32)


def _dot_tn(a, b):
    return lax.dot_general(a, b, _TN, preferred_element_type=F32)


def _split(a):
    hi = a.astype(BF16)
    return hi, (a - hi.astype(F32)).astype(BF16)


def _split3(a):
    hi = a.astype(BF16).astype(F32)
    r1 = a - hi
    mid = r1.astype(BF16).astype(F32)
    return hi, mid, r1 - mid


def _rms(x, g):
    return x * lax.rsqrt(jnp.mean(x * x, axis=-1, keepdims=True) + NORM_EPS) * g


def _head_sum(x):
    outs = []
    for c in range(x.shape[1] // LANES):
        xc = x[:, c * LANES:(c + 1) * LANES]
        lo = lax.broadcasted_iota(jnp.int32, xc.shape, 1) < HEAD_DIM
        s0 = jnp.sum(jnp.where(lo, xc, 0.0), axis=1, keepdims=True)
        s1 = jnp.sum(jnp.where(lo, 0.0, xc), axis=1, keepdims=True)
        outs.append(jnp.where(lo, s0, s1))
    return outs[0] if len(outs) == 1 else jnp.concatenate(outs, axis=1)


def _head_rms(t, g):
    ms = _head_sum(t * t) * (1.0 / HEAD_DIM)
    return t * lax.rsqrt(ms + NORM_EPS) * g


def _sigmoid(z):
    return 1.0 / (1.0 + jnp.exp(-z))


def _const_spec(shape):
    nd = len(shape)
    return pl.BlockSpec(shape, lambda *_: (0,) * nd)


def _params(sem):
    return pltpu.CompilerParams(dimension_semantics=sem, vmem_limit_bytes=VMEM_LIMIT)


def _rwkv_prep_kernel(x_ref, xp_ref, ng_ref, mu_ref, wr_ref, wk_ref, wv_ref,
                      w0_ref, w1_ref, w2_ref, a0_ref, a1_ref, a2_ref, g1_ref, g2_ref,
                      kkw_ref, kaw_ref,
                      r_out, lw_out, k_out, v_out, kk_out, b_out, g_out):
    i = pl.program_id(1)
    ng = ng_ref[...]
    h = _rms(x_ref[0], ng)
    hp = _rms(xp_ref[0][7:8, :], ng)
    hp = jnp.where(i > 0, hp, 0.0)
    rolled = pltpu.roll(h, 1, 0)
    first = jnp.where(lax.broadcasted_iota(jnp.int32, (8, h.shape[1]), 0) == 0, hp, rolled[:8])
    hs = jnp.concatenate([first, rolled[8:]], axis=0)
    hb = h.astype(BF16)
    xxb = (hs - h).astype(BF16)
    mub = mu_ref[...].astype(BF16)

    def mix(j):
        return hb + xxb * mub[j:j + 1, :]

    tw = _dot(mix(1), w1_ref[...])
    ta = _dot(mix(4), a1_ref[...])
    tg = _dot(mix(5), g1_ref[...])
    k = _dot(mix(2), wk_ref[...])
    wl = _dot(jnp.tanh(tw).astype(BF16), w2_ref[...])
    al = _dot(ta.astype(BF16), a2_ref[...])
    g_out[0] = _dot(_sigmoid(tg).astype(BF16), g2_ref[...]).astype(g_out.dtype)

    lw_out[0] = -EXP_M_HALF * _sigmoid(w0_ref[...] + wl)
    a = _sigmoid(a0_ref[...] + al)
    kk = k * kkw_ref[...]
    kk = kk * lax.rsqrt(jnp.maximum(_head_sum(kk * kk), 1e-24))
    k_out[0] = (k * (1.0 + (a - 1.0) * kaw_ref[...])).astype(k_out.dtype)
    kk_out[0] = kk.astype(kk_out.dtype)
    b_out[0] = (kk * a).astype(b_out.dtype)

    r_out[0] = _dot(mix(0), wr_ref[...]).astype(r_out.dtype)
    v_out[0] = _dot(mix(3), wv_ref[...]).astype(v_out.dtype)


def _rwkv_prep(x, ng, mu, wr, wk, wv, w0, w1, w2, a0, a1, a2, g1, g2, kkw, kaw, *, tm):
    B, T, D = x.shape
    row = lambda a: a.reshape(1, D)
    consts = [row(ng), mu, wr, wk, wv, row(w0), w1, w2, row(a0), a1, a2, g1, g2, row(kkw), row(kaw)]
    tile = pl.BlockSpec((1, tm, D), lambda b, i: (b, i, 0))
    prev = pl.BlockSpec((1, 8, D), lambda b, i: (b, jnp.maximum(i * (tm // 8) - 1, 0), 0))
    out = lambda dt: jax.ShapeDtypeStruct((B, T, D), dt)
    return pl.pallas_call(
        _rwkv_prep_kernel,
        grid=(B, T // tm),
        in_specs=[tile, prev] + [_const_spec(c.shape) for c in consts],
        out_specs=[tile] * 7,
        out_shape=[out(BF16), out(F32)] + [out(BF16)] * 5,
        compiler_params=_params(("parallel", "parallel")),
        name="rwkv_prep",
    )(x, x, *consts)


def _blockdiag(z, lo):
    z = z.astype(BF16)
    zero = jnp.zeros_like(z)
    return jnp.concatenate([jnp.where(lo, z, zero), jnp.where(lo, zero, z)], axis=0)


def _rwkv_scan_kernel(r_ref, lw_ref, k_ref, v_ref, kk_ref, b_ref, y_out, s_scr, *, nbatch, npairs):
    c = pl.program_id(2)

    @pl.when(c == 0)
    def _():
        s_scr[...] = jnp.zeros_like(s_scr)

    C = CHUNK
    t_i = lax.broadcasted_iota(jnp.int32, (C, C), 0)
    j_i = lax.broadcasted_iota(jnp.int32, (C, C), 1)
    ltri = (j_i <= t_i).astype(BF16)
    row = lax.broadcasted_iota(jnp.int32, (C, LANES), 0)
    lane = lax.broadcasted_iota(jnp.int32, (C, LANES), 1)
    lo = lane < HEAD_DIM
    col = jnp.bitwise_and(lane, HEAD_DIM - 1)
    strict = col < row
    incl = col <= row
    rr = lax.broadcasted_iota(jnp.int32, (LANES, LANES), 0)
    cc = lax.broadcasted_iota(jnp.int32, (LANES, LANES), 1)
    same_head = (rr < HEAD_DIM) == (cc < HEAD_DIM)
    bd = functools.partial(_blockdiag, lo=lo)
    cat0 = lambda *xs: jnp.concatenate([x.astype(BF16) for x in xs], axis=0)
    cat1 = lambda *xs: jnp.concatenate([x.astype(BF16) for x in xs], axis=1)

    units = [(bi, slice(p * LANES, (p + 1) * LANES)) for bi in range(nbatch) for p in range(npairs)]
    P = range(len(units))
    r = [r_ref[bi, :, sl].astype(F32) for bi, sl in units]
    lw = [lw_ref[bi, :, sl] for bi, sl in units]
    k = [k_ref[bi, :, sl].astype(F32) for bi, sl in units]
    v = [v_ref[bi, :, sl].astype(F32) for bi, sl in units]
    kk = [kk_ref[bi, :, sl].astype(F32) for bi, sl in units]
    b = [b_ref[bi, :, sl].astype(F32) for bi, sl in units]

    cw2 = [_dot(ltri, cat1(*_split(lw[p]))) for p in P]
    cw = [cw2[p][:, :LANES] + cw2[p][:, LANES:] for p in P]
    cwl = [cw[p][C - 1:C, :] for p in P]
    at = [-kk[p] * jnp.exp(cw[p] - lw[p]) for p in P]
    dinv = [jnp.exp(-cw[p]) for p in P]
    rt = [r[p] * jnp.exp(cw[p]) for p in P]
    dend = [jnp.exp(cwl[p] - cw[p]) for p in P]

    x = [_dot_nt(cat0(at[p], rt[p]), cat0(bd(b[p] * dinv[p]), bd(k[p] * dinv[p]))) for p in P]
    aab = [jnp.where(strict, x[p][:C, :LANES], 0.0) for p in P]
    arb = [jnp.where(incl, x[p][C:, :LANES], 0.0) for p in P]
    aak = [jnp.where(strict, x[p][:C, LANES:], 0.0) for p in P]
    ark = [jnp.where(incl, x[p][C:, LANES:], 0.0) for p in P]

    bdv = [bd(v[p]) for p in P]
    av = [_dot(aak[p].astype(BF16), bdv[p]) for p in P]
    eye = jnp.where(col == row, 1.0, 0.0)
    tinv = [eye + aab[p] for p in P]
    n = [_dot(aab[p].astype(BF16), bd(aab[p])) for p in P]
    for it in range(5):
        last = it == 4
        res = [_dot(n[p].astype(BF16),
                    jnp.concatenate([bd(tinv[p])] + ([] if last else [bd(n[p])]), axis=1)) for p in P]
        tinv = [tinv[p] + res[p][:, :LANES] for p in P]
        if not last:
            n = [res[p][:, LANES:] for p in P]
    z = [_dot(tinv[p].astype(BF16), jnp.concatenate([bd(at[p]), bd(av[p])], axis=1)) for p in P]
    z1 = [z[p][:, :LANES] for p in P]
    z2 = [z[p][:, LANES:] for p in P]

    s = [s_scr[p] for p in P]
    ws = [_dot_nt(cat0(z1[p], rt[p]), s[p].astype(BF16)) for p in P]
    u = [ws[p][:C] + z2[p] for p in P]
    y = [ws[p][C:] + _dot(cat1(arb[p], ark[p]), cat0(bd(u[p]), bdv[p])) for p in P]
    upd = [_dot_tn(cat0(u[p], v[p]), cat0(b[p] * dend[p], k[p] * dend[p])) for p in P]
    for p in P:
        s_scr[p] = s[p] * jnp.exp(cwl[p]) + jnp.where(same_head, upd[p], 0.0)

    for p in P:
        bi, sl = units[p]
        y_out[bi, :, sl] = y[p]


def _rwkv_scan(r, lw, k, v, kk, b, *, nbatch, npairs):
    B, T, D = r.shape
    W = npairs * LANES
    tile = pl.BlockSpec((nbatch, CHUNK, W), lambda bi, p, c: (bi, c, p))
    return pl.pallas_call(
        functools.partial(_rwkv_scan_kernel, nbatch=nbatch, npairs=npairs),
        grid=(B // nbatch, D // W, T // CHUNK),
        in_specs=[tile] * 6,
        out_specs=tile,
        out_shape=jax.ShapeDtypeStruct((B, T, D), F32),
        scratch_shapes=[pltpu.VMEM((nbatch * npairs, LANES, LANES), F32)],
        compiler_params=_params(("parallel", "parallel", "arbitrary")),
        name="rwkv_scan",
    )(r, lw, k, v, kk, b)


def _mlp_tail(x, g_ref, win_ref, wout_ref, o_ref, tf):
    xn = _rms(x, g_ref[...]).astype(BF16)
    acc = x
    for f in range(win_ref.shape[1] // tf):
        hid = jnp.maximum(_dot(xn, win_ref[:, f * tf:(f + 1) * tf]), 0.0)
        acc = acc + _dot((hid * hid).astype(BF16), wout_ref[f * tf:(f + 1) * tf, :])
    o_ref[...] = acc


def _proj_mlp_kernel(res_ref, a_ref, wo_ref, g_ref, win_ref, wout_ref, o_ref, *, tf):
    x = res_ref[...] + _dot(a_ref[...], wo_ref[...])
    _mlp_tail(x, g_ref, win_ref, wout_ref, o_ref, tf)


def _rwkv_out_mlp_kernel(res_ref, y_ref, r_ref, k_ref, v_ref, gate_ref, rk_ref, lnw_ref, lnb_ref,
                         wo_ref, g_ref, win_ref, wout_ref, o_ref, a_scr, *, tf):
    i = pl.program_id(0)

    def output_stage():
        y = y_ref[...]
        d = y - _head_sum(y) * (1.0 / HEAD_DIM)
        var = _head_sum(d * d) * (1.0 / HEAD_DIM)
        yn = d * lax.rsqrt(var + GN_EPS)
        r = r_ref[...].astype(F32)
        k = k_ref[...].astype(F32)
        bonus = _head_sum(r * k * rk_ref[...]) * v_ref[...].astype(F32)
        return ((yn * lnw_ref[...] + lnb_ref[...] + bonus) * gate_ref[...].astype(F32)).astype(BF16)

    @pl.when(i == 0)
    def _():
        a_scr[...] = output_stage()

    @pl.when(i > 0)
    def _():
        x = res_ref[...] + _dot(a_scr[...], wo_ref[...])
        _mlp_tail(x, g_ref, win_ref, wout_ref, o_ref, tf)
        a_scr[...] = output_stage()


def _proj_mlp(res, mix, w_o, g, w_in, w_out, *, tm, tf, rwkv=None):
    M, D = res.shape
    tile = pl.BlockSpec((tm, D), lambda i: (i, 0))
    once = lambda shape: pl.BlockSpec(shape, lambda i: (0,) * len(shape), pipeline_mode=pl.Buffered(1))
    vec = once((1, D))
    weights = [once(w_o.shape), vec, once(w_in.shape), once(w_out.shape)]
    wargs = (w_o, g.reshape(1, D), w_in, w_out)
    n = M // tm
    if rwkv is None:
        body, grid, specs, args, out_spec, scratch = _proj_mlp_kernel, n, [tile, tile], (res, mix), tile, []
    else:
        r, k, v, gate, rk, lnw, lnb = rwkv
        prev = pl.BlockSpec((tm, D), lambda i: (jnp.maximum(i - 1, 0), 0))
        cur = pl.BlockSpec((tm, D), lambda i: (jnp.minimum(i, n - 1), 0))
        body, grid, specs, out_spec = _rwkv_out_mlp_kernel, n + 1, [prev] + [cur] * 5 + [vec] * 3, prev
        args = (res, mix, r, k, v, gate, rk.reshape(1, D), lnw.reshape(1, D), lnb.reshape(1, D))
        scratch = [pltpu.VMEM((tm, D), BF16)]
    return pl.pallas_call(
        functools.partial(body, tf=tf),
        grid=(grid,),
        in_specs=specs + weights,
        out_specs=out_spec,
        out_shape=jax.ShapeDtypeStruct((M, D), F32),
        scratch_shapes=scratch,
        compiler_params=_params(("arbitrary",)),
        name="proj_mlp",
    )(*args, *wargs)


def _q_proj_kernel(x_ref, g_ref, wt_ref, qg_ref, c_ref, q_out, *, scale):
    hn = _rms(x_ref[0], g_ref[...])
    qt = _dot_nt(wt_ref[...], hn.astype(BF16))
    tm = hn.shape[0]
    row = lax.broadcasted_iota(jnp.int32, (HEAD_DIM, tm), 0)
    for h in range(qt.shape[0] // HEAD_DIM):
        hs = slice(h * HEAD_DIM, (h + 1) * HEAD_DIM)
        qh = qt[hs, :]
        ms = jnp.mean(qh * qh, axis=0, keepdims=True)
        qn = qh * lax.rsqrt(ms + NORM_EPS) * (qg_ref[hs, :] * scale)
        hi, mid, lo = _split3(c_ref[0, h:h + 1, :] * LOG2E)
        aug = jnp.where(row == 0, hi, jnp.where(row == 1, mid, jnp.where(
            row == 2, lo, jnp.where(row < 6, 1.0, 0.0))))
        q_out[0, h] = jnp.concatenate([qn, aug], axis=0).astype(q_out.dtype)


def _q_proj(x, g, wt, qg, c_row, *, tm, scale):
    B, T, D = x.shape
    H = D // HEAD_DIM
    return pl.pallas_call(
        functools.partial(_q_proj_kernel, scale=scale),
        grid=(B, T // tm),
        in_specs=[pl.BlockSpec((1, tm, D), lambda b, i: (b, i, 0)), _const_spec((1, D)),
                  _const_spec(wt.shape), _const_spec((D, 1)),
                  pl.BlockSpec((1, H, tm), lambda b, i: (b, 0, i))],
        out_specs=pl.BlockSpec((1, H, LANES, tm), lambda b, i: (b, 0, 0, i)),
        out_shape=jax.ShapeDtypeStruct((B, H, LANES, T), BF16),
        compiler_params=_params(("parallel", "parallel")),
        name="q_proj",
    )(x, g.reshape(1, D), wt, qg.reshape(D, 1), c_row)


def _shared_kv_kernel(x_ref, g_ref, wk_ref, wvt_ref, wf_ref, fb_ref, kg_ref, sel_ref,
                      k_out, vt_out, c_out, carry_scr, *, tq):
    i = pl.program_id(1)

    @pl.when(i == 0)
    def _():
        carry_scr[...] = jnp.zeros_like(carry_scr)

    hn = _rms(x_ref[0], g_ref[...])
    hb = hn.astype(BF16)
    tm, D = hn.shape
    H = D // HEAD_DIM
    k = _head_rms(_dot(hb, wk_ref[...]), kg_ref[...])
    vt = _dot_nt(wvt_ref[...], hb)

    f = _dot(hb, wf_ref[...]) + fb_ref[...]
    logf = jnp.minimum(f, 0.0) - jnp.log(1.0 + jnp.exp(-jnp.abs(f)))
    t_i = lax.broadcasted_iota(jnp.int32, (tm, tm), 0)
    j_i = lax.broadcasted_iota(jnp.int32, (tm, tm), 1)
    ltri = (j_i <= t_i).astype(BF16)
    c3 = _dot(ltri, jnp.concatenate([t.astype(BF16) for t in _split3(logf)], axis=1))
    c = (c3[:, :LANES] + c3[:, LANES:2 * LANES]) + c3[:, 2 * LANES:] + carry_scr[0:1, :]
    carry_scr[...] = jnp.broadcast_to(c[tm - 1:tm, :], carry_scr.shape)
    c_out[0] = c.T[:H, :]

    lane = lax.broadcasted_iota(jnp.int32, (tm, LANES), 1)
    hi, mid, lo = (jnp.where(lane < H, t, 0.0) for t in _split3(c * (-LOG2E)))
    packed = (hi + pltpu.roll(mid, H, 1)) + (pltpu.roll(lo, 2 * H, 1) + jnp.where(lane == 3 * H, 1.0, 0.0))
    aug = _dot(packed.astype(BF16), sel_ref[...])
    vrow = lax.broadcasted_iota(jnp.int32, (V_ROWS - HEAD_DIM, tq), 0)
    ones_row = jnp.where(vrow == 0, 1.0, 0.0)
    for h in range(H):
        base = k[:, (h // 2) * LANES:(h // 2 + 1) * LANES]
        if h % 2:
            base = pltpu.roll(base, HEAD_DIM, 1)
        tile = jnp.where(lane < HEAD_DIM, base, aug[:, h * LANES:(h + 1) * LANES])
        k_out[0, h] = tile.astype(k_out.dtype)
        for sb in range(tm // tq):
            vt_out[0, h, sb] = jnp.concatenate(
                [vt[h * HEAD_DIM:(h + 1) * HEAD_DIM, sb * tq:(sb + 1) * tq], ones_row],
                axis=0).astype(vt_out.dtype)


def _bias_selector(H):
    assert 3 * H + 1 <= LANES
    sel = np.zeros((LANES, H * LANES), np.float32)
    for h in range(H):
        sel[3 * H, h * LANES + HEAD_DIM:h * LANES + HEAD_DIM + 3] = 1.0
        for t in range(3):
            sel[t * H + h, h * LANES + HEAD_DIM + 3 + t] = 1.0
    return jnp.asarray(sel, BF16)


def _shared_kv(x, g, wk, wvt, wf, fb, kg, *, tm, tq):
    B, T, D = x.shape
    H = D // HEAD_DIM
    sel = _bias_selector(H)
    return pl.pallas_call(
        functools.partial(_shared_kv_kernel, tq=tq),
        grid=(B, T // tm),
        in_specs=[pl.BlockSpec((1, tm, D), lambda b, i: (b, i, 0)), _const_spec((1, D)),
                  _const_spec(wk.shape), _const_spec(wvt.shape), _const_spec(wf.shape),
                  _const_spec((1, LANES)), _const_spec((1, D)), _const_spec(sel.shape)],
        out_specs=[pl.BlockSpec((1, H, tm, LANES), lambda b, i: (b, 0, i, 0)),
                   pl.BlockSpec((1, H, tm // tq, V_ROWS, tq), lambda b, i: (b, 0, i, 0, 0)),
                   pl.BlockSpec((1, H, tm), lambda b, i: (b, 0, i))],
        out_shape=[jax.ShapeDtypeStruct((B, H, T, LANES), BF16),
                   jax.ShapeDtypeStruct((B, H, T // tq, V_ROWS, tq), BF16),
                   jax.ShapeDtypeStruct((B, H, T), F32)],
        scratch_shapes=[pltpu.VMEM((8, LANES), F32)],
        compiler_params=_params(("parallel", "arbitrary")),
        name="shared_kv",
    )(x, g.reshape(1, D), wk, wvt, wf, fb, kg.reshape(1, D), sel)


def _fox_attn_kernel(q_ref, k_ref, vt_ref, o_ref, acc_scr, *, tq, nh):
    i = pl.program_id(2)
    qt = [q_ref[0, h] for h in range(nh)]
    acc_scr[...] = jnp.zeros_like(acc_scr)
    key_i = lax.broadcasted_iota(jnp.int32, (tq, tq), 0)
    qry_i = lax.broadcasted_iota(jnp.int32, (tq, tq), 1)
    causal = key_i <= qry_i

    def step(j, m, masked):
        off = pl.multiple_of(j * tq, tq)
        s = [_dot(k_ref[0, h, pl.ds(off, tq), :], qt[h]) for h in range(nh)]
        if masked:
            s = [jnp.where(causal, s[h], NEG_BIG) for h in range(nh)]
        m_new = [jnp.maximum(m[h], jnp.max(s[h], axis=0, keepdims=True)) for h in range(nh)]
        p = [jnp.exp2(s[h] - m_new[h]).astype(BF16) for h in range(nh)]
        alpha = [jnp.exp2(m[h] - m_new[h]) for h in range(nh)]
        pv = [_dot(vt_ref[0, h, j], p[h]) for h in range(nh)]
        for h in range(nh):
            acc_scr[h] = alpha[h] * acc_scr[h] + pv[h]
        return tuple(m_new)

    m0 = tuple(jnp.full((1, tq), NEG_BIG, F32) for _ in range(nh))
    m = lax.fori_loop(0, i, functools.partial(step, masked=False), m0)
    step(i, m, True)
    ot = [acc_scr[h, :HEAD_DIM, :] * (1.0 / acc_scr[h, HEAD_DIM:HEAD_DIM + 1, :]) for h in range(nh)]
    o_ref[0] = jnp.concatenate(ot, axis=0).T.astype(o_ref.dtype)


def _fox_attn(qt, ka, vt, *, tq, nh):
    B, H, _, T = qt.shape
    nb = T // tq
    return pl.pallas_call(
        functools.partial(_fox_attn_kernel, tq=tq, nh=nh),
        grid=(B, H // nh, nb),
        in_specs=[pl.BlockSpec((1, nh, LANES, tq), lambda b, p, i: (b, p, 0, i)),
                  pl.BlockSpec((1, nh, T, LANES), lambda b, p, i: (b, p, 0, 0)),
                  pl.BlockSpec((1, nh, nb, V_ROWS, tq), lambda b, p, i: (b, p, 0, 0, 0))],
        out_specs=pl.BlockSpec((1, tq, nh * HEAD_DIM), lambda b, p, i: (b, i, p)),
        out_shape=jax.ShapeDtypeStruct((B, T, H * HEAD_DIM), BF16),
        scratch_shapes=[pltpu.VMEM((nh, V_ROWS, tq), F32)],
        compiler_params=_params(("parallel", "parallel", "arbitrary")),
        name="fox_attn",
    )(qt, ka, vt)


def kernel(x, rwkv_norm_g, rwkv_mu, rwkv_w_rkv, rwkv_w0, rwkv_w1, rwkv_w2, rwkv_a0, rwkv_a1, rwkv_a2, rwkv_g1, rwkv_g2, rwkv_k_k, rwkv_k_a, rwkv_r_k, rwkv_lnx_w, rwkv_lnx_b, rwkv_w_o, kv_norm_g, kv_w, kv_f_bias, k_norm_g, attn_norm_g, attn_w_q, q_norm_g, attn_w_o, mlp_norm_g, mlp_w_in, mlp_w_out):
    B, T, D = x.shape
    M = B * T
    n_a = rwkv_norm_g.shape[0]
    depth = mlp_norm_g.shape[0]
    bf = lambda w: w.astype(BF16)
    tm = min(512, T)
    tq = min(256, T)
    tf = min(1024, mlp_w_in.shape[-1])
    npairs = D // LANES

    k_sh = v_sh = c_sh = None
    for layer in range(depth):
        if layer < n_a:
            i = layer
            r, lw, k, v, kk, b, g = _rwkv_prep(
                x, rwkv_norm_g[i], rwkv_mu[i], bf(rwkv_w_rkv[i, 0]), bf(rwkv_w_rkv[i, 1]),
                bf(rwkv_w_rkv[i, 2]), rwkv_w0[i], bf(rwkv_w1[i]), bf(rwkv_w2[i]), rwkv_a0[i],
                bf(rwkv_a1[i]), bf(rwkv_a2[i]), bf(rwkv_g1[i]), bf(rwkv_g2[i]),
                rwkv_k_k[i], rwkv_k_a[i], tm=tm)
            mix = _rwkv_scan(r, lw, k, v, kk, b, nbatch=2 if B % 2 == 0 else 1, npairs=npairs)
            flat = lambda t: t.reshape(M, D)
            w_o, extra = rwkv_w_o[i], (flat(r), flat(k), flat(v), flat(g), rwkv_r_k[i].reshape(D),
                                       rwkv_lnx_w[i], rwkv_lnx_b[i])
        else:
            j = layer - n_a
            qt = _q_proj(x, attn_norm_g[j], bf(attn_w_q[j].T), q_norm_g[j], c_sh,
                         tm=tm, scale=HEAD_DIM ** -0.5 * LOG2E)
            mix = _fox_attn(qt, k_sh, v_sh, tq=tq, nh=min(16, D // HEAD_DIM))
            w_o, extra = attn_w_o[j], None
        x = _proj_mlp(x.reshape(M, D), mix.reshape(M, D), bf(w_o), mlp_norm_g[layer],
                      bf(mlp_w_in[layer]), bf(mlp_w_out[layer]), tm=tm, tf=tf,
                      rwkv=extra).reshape(B, T, D)
        if layer == n_a - 1:
            wf = jnp.pad(kv_w[:, 2 * D:], ((0, 0), (0, LANES - (kv_w.shape[1] - 2 * D))))
            fb = jnp.pad(kv_f_bias, (0, LANES - kv_f_bias.shape[0])).reshape(1, LANES)
            k_sh, v_sh, c_sh = _shared_kv(
                x, kv_norm_g, bf(kv_w[:, :D]), bf(kv_w[:, D:2 * D].T), bf(wf), fb, k_norm_g, tm=tm, tq=tq)
    return x
```

```python
import functools

import jax
import jax.numpy as jnp
import numpy as np
from jax import lax
from jax.experimental import pallas as pl
from jax.experimental.pallas import tpu as pltpu

HEAD_DIM = 64
LANES = 128
NORM_EPS = 1e-6
GN_EPS = 64e-5
CHUNK = 64
NEG_BIG = -1e30
LOG2E = 1.4426950408889634
EXP_M_HALF = 0.6065306597126334
V_ROWS = 80
VMEM_LIMIT = 56 * 1024 * 1024

BF16 = jnp.bfloat16
F32 = jnp.float32

_NT = (((1,), (1,)), ((), ()))
_TN = (((0,), (0,)), ((), ()))


def _dot(a, b):
    return jnp.dot(a, b, preferred_element_type=F32)


def _dot_nt(a, b):
    return lax.dot_general(a, b, _NT, preferred_element_type=F32)


def _dot_tn(a, b):
    return lax.dot_general(a, b, _TN, preferred_element_type=F32)


def _split(a):
    hi = a.astype(BF16)
    return hi, (a - hi.astype(F32)).astype(BF16)


def _split3(a):
    hi = a.astype(BF16).astype(F32)
    r1 = a - hi
    mid = r1.astype(BF16).astype(F32)
    return hi, mid, r1 - mid


def _rms(x, g):
    return x * lax.rsqrt(jnp.mean(x * x, axis=-1, keepdims=True) + NORM_EPS) * g


def _head_sum(x):
    outs = []
    for c in range(x.shape[1] // LANES):
        xc = x[:, c * LANES:(c + 1) * LANES]
        lo = lax.broadcasted_iota(jnp.int32, xc.shape, 1) < HEAD_DIM
        s0 = jnp.sum(jnp.where(lo, xc, 0.0), axis=1, keepdims=True)
        s1 = jnp.sum(jnp.where(lo, 0.0, xc), axis=1, keepdims=True)
        outs.append(jnp.where(lo, s0, s1))
    return outs[0] if len(outs) == 1 else jnp.concatenate(outs, axis=1)


def _head_rms(t, g):
    ms = _head_sum(t * t) * (1.0 / HEAD_DIM)
    return t * lax.rsqrt(ms + NORM_EPS) * g


def _sigmoid(z):
    return 1.0 / (1.0 + jnp.exp(-z))


def _const_spec(shape):
    nd = len(shape)
    return pl.BlockSpec(shape, lambda *_: (0,) * nd)


def _params(sem):
    return pltpu.CompilerParams(dimension_semantics=sem, vmem_limit_bytes=VMEM_LIMIT)


def _rwkv_prep_kernel(x_ref, xp_ref, ng_ref, mu_ref, wr_ref, wk_ref, wv_ref,
                      w0_ref, w1_ref, w2_ref, a0_ref, a1_ref, a2_ref, g1_ref, g2_ref,
                      kkw_ref, kaw_ref,
                      r_out, lw_out, k_out, v_out, kk_out, b_out, g_out):
    i = pl.program_id(1)
    ng = ng_ref[...]
    h = _rms(x_ref[0], ng)
    hp = _rms(xp_ref[0][7:8, :], ng)
    hp = jnp.where(i > 0, hp, 0.0)
    rolled = pltpu.roll(h, 1, 0)
    first = jnp.where(lax.broadcasted_iota(jnp.int32, (8, h.shape[1]), 0) == 0, hp, rolled[:8])
    hs = jnp.concatenate([first, rolled[8:]], axis=0)
    hb = h.astype(BF16)
    xxb = (hs - h).astype(BF16)
    mub = mu_ref[...].astype(BF16)

    def mix(j):
        return hb + xxb * mub[j:j + 1, :]

    tw = _dot(mix(1), w1_ref[...])
    ta = _dot(mix(4), a1_ref[...])
    tg = _dot(mix(5), g1_ref[...])
    k = _dot(mix(2), wk_ref[...])
    wl = _dot(jnp.tanh(tw).astype(BF16), w2_ref[...])
    al = _dot(ta.astype(BF16), a2_ref[...])
    g_out[0] = _dot(_sigmoid(tg).astype(BF16), g2_ref[...]).astype(g_out.dtype)

    lw_out[0] = -EXP_M_HALF * _sigmoid(w0_ref[...] + wl)
    a = _sigmoid(a0_ref[...] + al)
    kk = k * kkw_ref[...]
    kk = kk * lax.rsqrt(jnp.maximum(_head_sum(kk * kk), 1e-24))
    k_out[0] = (k * (1.0 + (a - 1.0) * kaw_ref[...])).astype(k_out.dtype)
    kk_out[0] = kk.astype(kk_out.dtype)
    b_out[0] = (kk * a).astype(b_out.dtype)

    r_out[0] = _dot(mix(0), wr_ref[...]).astype(r_out.dtype)
    v_out[0] = _dot(mix(3), wv_ref[...]).astype(v_out.dtype)


def _rwkv_prep(x, ng, mu, wr, wk, wv, w0, w1, w2, a0, a1, a2, g1, g2, kkw, kaw, *, tm):
    B, T, D = x.shape
    row = lambda a: a.reshape(1, D)
    consts = [row(ng), mu, wr, wk, wv, row(w0), w1, w2, row(a0), a1, a2, g1, g2, row(kkw), row(kaw)]
    tile = pl.BlockSpec((1, tm, D), lambda b, i: (b, i, 0))
    prev = pl.BlockSpec((1, 8, D), lambda b, i: (b, jnp.maximum(i * (tm // 8) - 1, 0), 0))
    out = lambda dt: jax.ShapeDtypeStruct((B, T, D), dt)
    return pl.pallas_call(
        _rwkv_prep_kernel,
        grid=(B, T // tm),
        in_specs=[tile, prev] + [_const_spec(c.shape) for c in consts],
        out_specs=[tile] * 7,
        out_shape=[out(BF16), out(F32)] + [out(BF16)] * 5,
        compiler_params=_params(("parallel", "parallel")),
        name="rwkv_prep",
    )(x, x, *consts)


def _blockdiag(z, lo):
    z = z.astype(BF16)
    zero = jnp.zeros_like(z)
    return jnp.concatenate([jnp.where(lo, z, zero), jnp.where(lo, zero, z)], axis=0)


def _rwkv_scan_kernel(r_ref, lw_ref, k_ref, v_ref, kk_ref, b_ref, y_out, s_scr, *, nbatch, npairs):
    c = pl.program_id(2)

    @pl.when(c == 0)
    def _():
        s_scr[...] = jnp.zeros_like(s_scr)

    C = CHUNK
    t_i = lax.broadcasted_iota(jnp.int32, (C, C), 0)
    j_i = lax.broadcasted_iota(jnp.int32, (C, C), 1)
    ltri = (j_i <= t_i).astype(BF16)
    row = lax.broadcasted_iota(jnp.int32, (C, LANES), 0)
    lane = lax.broadcasted_iota(jnp.int32, (C, LANES), 1)
    lo = lane < HEAD_DIM
    col = jnp.bitwise_and(lane, HEAD_DIM - 1)
    strict = col < row
    incl = col <= row
    rr = lax.broadcasted_iota(jnp.int32, (LANES, LANES), 0)
    cc = lax.broadcasted_iota(jnp.int32, (LANES, LANES), 1)
    same_head = (rr < HEAD_DIM) == (cc < HEAD_DIM)
    bd = functools.partial(_blockdiag, lo=lo)
    cat0 = lambda *xs: jnp.concatenate([x.astype(BF16) for x in xs], axis=0)
    cat1 = lambda *xs: jnp.concatenate([x.astype(BF16) for x in xs], axis=1)

    units = [(bi, slice(p * LANES, (p + 1) * LANES)) for bi in range(nbatch) for p in range(npairs)]
    P = range(len(units))
    r = [r_ref[bi, :, sl].astype(F32) for bi, sl in units]
    lw = [lw_ref[bi, :, sl] for bi, sl in units]
    k = [k_ref[bi, :, sl].astype(F32) for bi, sl in units]
    v = [v_ref[bi, :, sl].astype(F32) for bi, sl in units]
    kk = [kk_ref[bi, :, sl].astype(F32) for bi, sl in units]
    b = [b_ref[bi, :, sl].astype(F32) for bi, sl in units]

    cw2 = [_dot(ltri, cat1(*_split(lw[p]))) for p in P]
    cw = [cw2[p][:, :LANES] + cw2[p][:, LANES:] for p in P]
    cwl = [cw[p][C - 1:C, :] for p in P]
    at = [-kk[p] * jnp.exp(cw[p] - lw[p]) for p in P]
    dinv = [jnp.exp(-cw[p]) for p in P]
    rt = [r[p] * jnp.exp(cw[p]) for p in P]
    dend = [jnp.exp(cwl[p] - cw[p]) for p in P]

    x = [_dot_nt(cat0(at[p], rt[p]), cat0(bd(b[p] * dinv[p]), bd(k[p] * dinv[p]))) for p in P]
    aab = [jnp.where(strict, x[p][:C, :LANES], 0.0) for p in P]
    arb = [jnp.where(incl, x[p][C:, :LANES], 0.0) for p in P]
    aak = [jnp.where(strict, x[p][:C, LANES:], 0.0) for p in P]
    ark = [jnp.where(incl, x[p][C:, LANES:], 0.0) for p in P]

    bdv = [bd(v[p]) for p in P]
    av = [_dot(aak[p].astype(BF16), bdv[p]) for p in P]
    eye = jnp.where(col == row, 1.0, 0.0)
    tinv = [eye + aab[p] for p in P]
    n = [_dot(aab[p].astype(BF16), bd(aab[p])) for p in P]
    for it in range(5):
        last = it == 4
        res = [_dot(n[p].astype(BF16),
                    jnp.concatenate([bd(tinv[p])] + ([] if last else [bd(n[p])]), axis=1)) for p in P]
        tinv = [tinv[p] + res[p][:, :LANES] for p in P]
        if not last:
            n = [res[p][:, LANES:] for p in P]
    z = [_dot(tinv[p].astype(BF16), jnp.concatenate([bd(at[p]), bd(av[p])], axis=1)) for p in P]
    z1 = [z[p][:, :LANES] for p in P]
    z2 = [z[p][:, LANES:] for p in P]

    s = [s_scr[p] for p in P]
    ws = [_dot_nt(cat0(z1[p], rt[p]), s[p].astype(BF16)) for p in P]
    u = [ws[p][:C] + z2[p] for p in P]
    y = [ws[p][C:] + _dot(cat1(arb[p], ark[p]), cat0(bd(u[p]), bdv[p])) for p in P]
    upd = [_dot_tn(cat0(u[p], v[p]), cat0(b[p] * dend[p], k[p] * dend[p])) for p in P]
    for p in P:
        s_scr[p] = s[p] * jnp.exp(cwl[p]) + jnp.where(same_head, upd[p], 0.0)

    for p in P:
        bi, sl = units[p]
        y_out[bi, :, sl] = y[p]


def _rwkv_scan(r, lw, k, v, kk, b, *, nbatch, npairs):
    B, T, D = r.shape
    W = npairs * LANES
    tile = pl.BlockSpec((nbatch, CHUNK, W), lambda bi, p, c: (bi, c, p))
    return pl.pallas_call(
        functools.partial(_rwkv_scan_kernel, nbatch=nbatch, npairs=npairs),
        grid=(B // nbatch, D // W, T // CHUNK),
        in_specs=[tile] * 6,
        out_specs=tile,
        out_shape=jax.ShapeDtypeStruct((B, T, D), F32),
        scratch_shapes=[pltpu.VMEM((nbatch * npairs, LANES, LANES), F32)],
        compiler_params=_params(("parallel", "parallel", "arbitrary")),
        name="rwkv_scan",
    )(r, lw, k, v, kk, b)


def _mlp_tail(x, g_ref, win_ref, wout_ref, o_ref, tf):
    xn = _rms(x, g_ref[...]).astype(BF16)
    acc = x
    for f in range(win_ref.shape[-1] // tf):
        hid = jnp.maximum(_dot(xn, win_ref[:, f * tf:(f + 1) * tf]), 0.0)
        acc = acc + _dot((hid * hid).astype(BF16), wout_ref[f * tf:(f + 1) * tf, :])
    o_ref[...] = acc


def _proj_mlp_kernel(res_ref, a_ref, wo_ref, g_ref, win_ref, wout_ref, o_ref, *, tf):
    x = res_ref[...] + _dot(a_ref[...], wo_ref[...])
    _mlp_tail(x, g_ref, win_ref, wout_ref, o_ref, tf)


def _rwkv_out_mlp_kernel(res_ref, y_ref, r_ref, k_ref, v_ref, gate_ref, rk_ref, lnw_ref, lnb_ref,
                         wo_ref, g_ref, win_ref, wout_ref, o_ref, a_scr, *, tf):
    i = pl.program_id(0)

    def output_stage():
        y = y_ref[...]
        d = y - _head_sum(y) * (1.0 / HEAD_DIM)
        var = _head_sum(d * d) * (1.0 / HEAD_DIM)
        yn = d * lax.rsqrt(var + GN_EPS)
        r = r_ref[...].astype(F32)
        k = k_ref[...].astype(F32)
        bonus = _head_sum(r * k * rk_ref[...]) * v_ref[...].astype(F32)
        return ((yn * lnw_ref[...] + lnb_ref[...] + bonus) * gate_ref[...].astype(F32)).astype(BF16)

    @pl.when(i == 0)
    def _():
        a_scr[...] = output_stage()

    @pl.when(i > 0)
    def _():
        x = res_ref[...] + _dot(a_scr[...], wo_ref[...])
        _mlp_tail(x, g_ref, win_ref, wout_ref, o_ref, tf)
        a_scr[...] = output_stage()


def _proj_mlp(res, mix, w_o, g, w_in, w_out, *, layer, tm, tf, rwkv=None):
    M, D = res.shape
    tile = pl.BlockSpec((tm, D), lambda i: (i, 0))
    once = lambda shape: pl.BlockSpec(shape, lambda i: (0,) * len(shape), pipeline_mode=pl.Buffered(1))
    of_layer = lambda w: pl.BlockSpec((None,) + w.shape[1:], lambda i: (layer, 0, 0),
                                      pipeline_mode=pl.Buffered(1))
    vec = once((1, D))
    weights = [once(w_o.shape), vec, of_layer(w_in), of_layer(w_out)]
    wargs = (w_o, g.reshape(1, D), w_in, w_out)
    n = M // tm
    if rwkv is None:
        body, grid, specs, args, out_spec, scratch = _proj_mlp_kernel, n, [tile, tile], (res, mix), tile, []
    else:
        r, k, v, gate, rk, lnw, lnb = rwkv
        prev = pl.BlockSpec((tm, D), lambda i: (jnp.maximum(i - 1, 0), 0))
        cur = pl.BlockSpec((tm, D), lambda i: (jnp.minimum(i, n - 1), 0))
        body, grid, specs, out_spec = _rwkv_out_mlp_kernel, n + 1, [prev] + [cur] * 5 + [vec] * 3, prev
        args = (res, mix, r, k, v, gate, rk.reshape(1, D), lnw.reshape(1, D), lnb.reshape(1, D))
        scratch = [pltpu.VMEM((tm, D), BF16)]
    return pl.pallas_call(
        functools.partial(body, tf=tf),
        grid=(grid,),
        in_specs=specs + weights,
        out_specs=out_spec,
        out_shape=jax.ShapeDtypeStruct((M, D), F32),
        scratch_shapes=scratch,
        compiler_params=_params(("arbitrary",)),
        name="proj_mlp",
    )(*args, *wargs)


def _q_proj_kernel(x_ref, g_ref, wt_ref, qg_ref, c_ref, q_out, *, scale):
    hn = _rms(x_ref[0], g_ref[...])
    qt = _dot_nt(wt_ref[...], hn.astype(BF16))
    tm = hn.shape[0]
    row = lax.broadcasted_iota(jnp.int32, (HEAD_DIM, tm), 0)
    for h in range(qt.shape[0] // HEAD_DIM):
        hs = slice(h * HEAD_DIM, (h + 1) * HEAD_DIM)
        qh = qt[hs, :]
        ms = jnp.mean(qh * qh, axis=0, keepdims=True)
        qn = qh * lax.rsqrt(ms + NORM_EPS) * (qg_ref[hs, :] * scale)
        hi, mid, lo = _split3(c_ref[0, h:h + 1, :] * LOG2E)
        aug = jnp.where(row == 0, hi, jnp.where(row == 1, mid, jnp.where(
            row == 2, lo, jnp.where(row < 6, 1.0, 0.0))))
        q_out[0, h] = jnp.concatenate([qn, aug], axis=0).astype(q_out.dtype)


def _q_proj(x, g, wt, qg, c_row, *, tm, scale):
    B, T, D = x.shape
    H = D // HEAD_DIM
    return pl.pallas_call(
        functools.partial(_q_proj_kernel, scale=scale),
        grid=(B, T // tm),
        in_specs=[pl.BlockSpec((1, tm, D), lambda b, i: (b, i, 0)), _const_spec((1, D)),
                  _const_spec(wt.shape), _const_spec((D, 1)),
                  pl.BlockSpec((1, H, tm), lambda b, i: (b, 0, i))],
        out_specs=pl.BlockSpec((1, H, LANES, tm), lambda b, i: (b, 0, 0, i)),
        out_shape=jax.ShapeDtypeStruct((B, H, LANES, T), BF16),
        compiler_params=_params(("parallel", "parallel")),
        name="q_proj",
    )(x, g.reshape(1, D), wt, qg.reshape(D, 1), c_row)


def _shared_kv_kernel(x_ref, g_ref, wk_ref, wvt_ref, wf_ref, fb_ref, kg_ref, sel_ref,
                      k_out, vt_out, c_out, carry_scr, *, tq):
    i = pl.program_id(1)

    @pl.when(i == 0)
    def _():
        carry_scr[...] = jnp.zeros_like(carry_scr)

    hn = _rms(x_ref[0], g_ref[...])
    hb = hn.astype(BF16)
    tm, D = hn.shape
    H = D // HEAD_DIM
    k = _head_rms(_dot(hb, wk_ref[...]), kg_ref[...])
    vt = _dot_nt(wvt_ref[...], hb)

    f = _dot(hb, wf_ref[...]) + fb_ref[...]
    logf = jnp.minimum(f, 0.0) - jnp.log(1.0 + jnp.exp(-jnp.abs(f)))
    t_i = lax.broadcasted_iota(jnp.int32, (tm, tm), 0)
    j_i = lax.broadcasted_iota(jnp.int32, (tm, tm), 1)
    ltri = (j_i <= t_i).astype(BF16)
    c3 = _dot(ltri, jnp.concatenate([t.astype(BF16) for t in _split3(logf)], axis=1))
    c = (c3[:, :LANES] + c3[:, LANES:2 * LANES]) + c3[:, 2 * LANES:] + carry_scr[0:1, :]
    carry_scr[...] = jnp.broadcast_to(c[tm - 1:tm, :], carry_scr.shape)
    c_out[0] = c.T[:H, :]

    lane = lax.broadcasted_iota(jnp.int32, (tm, LANES), 1)
    hi, mid, lo = (jnp.where(lane < H, t, 0.0) for t in _split3(c * (-LOG2E)))
    packed = (hi + pltpu.roll(mid, H, 1)) + (pltpu.roll(lo, 2 * H, 1) + jnp.where(lane == 3 * H, 1.0, 0.0))
    aug = _dot(packed.astype(BF16), sel_ref[...])
    vrow = lax.broadcasted_iota(jnp.int32, (V_ROWS - HEAD_DIM, tq), 0)
    ones_row = jnp.where(vrow == 0, 1.0, 0.0)
    for h in range(H):
        base = k[:, (h // 2) * LANES:(h // 2 + 1) * LANES]
        if h % 2:
            base = pltpu.roll(base, HEAD_DIM, 1)
        tile = jnp.where(lane < HEAD_DIM, base, aug[:, h * LANES:(h + 1) * LANES])
        k_out[0, h] = tile.astype(k_out.dtype)
        for sb in range(tm // tq):
            vt_out[0, h, sb] = jnp.concatenate(
                [vt[h * HEAD_DIM:(h + 1) * HEAD_DIM, sb * tq:(sb + 1) * tq], ones_row],
                axis=0).astype(vt_out.dtype)


def _bias_selector(H):
    assert 3 * H + 1 <= LANES
    sel = np.zeros((LANES, H * LANES), np.float32)
    for h in range(H):
        sel[3 * H, h * LANES + HEAD_DIM:h * LANES + HEAD_DIM + 3] = 1.0
        for t in range(3):
            sel[t * H + h, h * LANES + HEAD_DIM + 3 + t] = 1.0
    return jnp.asarray(sel, BF16)


def _shared_kv(x, g, wk, wvt, wf, fb, kg, *, tm, tq):
    B, T, D = x.shape
    H = D // HEAD_DIM
    sel = _bias_selector(H)
    return pl.pallas_call(
        functools.partial(_shared_kv_kernel, tq=tq),
        grid=(B, T // tm),
        in_specs=[pl.BlockSpec((1, tm, D), lambda b, i: (b, i, 0)), _const_spec((1, D)),
                  _const_spec(wk.shape), _const_spec(wvt.shape), _const_spec(wf.shape),
                  _const_spec((1, LANES)), _const_spec((1, D)), _const_spec(sel.shape)],
        out_specs=[pl.BlockSpec((1, H, tm, LANES), lambda b, i: (b, 0, i, 0)),
                   pl.BlockSpec((1, H, tm // tq, V_ROWS, tq), lambda b, i: (b, 0, i, 0, 0)),
                   pl.BlockSpec((1, H, tm), lambda b, i: (b, 0, i))],
        out_shape=[jax.ShapeDtypeStruct((B, H, T, LANES), BF16),
                   jax.ShapeDtypeStruct((B, H, T // tq, V_ROWS, tq), BF16),
                   jax.ShapeDtypeStruct((B, H, T), F32)],
        scratch_shapes=[pltpu.VMEM((8, LANES), F32)],
        compiler_params=_params(("parallel", "arbitrary")),
        name="shared_kv",
    )(x, g.reshape(1, D), wk, wvt, wf, fb, kg.reshape(1, D), sel)


def _fox_attn_kernel(q_ref, k_ref, vt_ref, o_ref, acc_scr, *, tq, nh):
    i = pl.program_id(2)
    qt = [q_ref[0, h] for h in range(nh)]
    acc_scr[...] = jnp.zeros_like(acc_scr)
    key_i = lax.broadcasted_iota(jnp.int32, (tq, tq), 0)
    qry_i = lax.broadcasted_iota(jnp.int32, (tq, tq), 1)
    causal = key_i <= qry_i

    def step(j, m, masked):
        off = pl.multiple_of(j * tq, tq)
        s = [_dot(k_ref[0, h, pl.ds(off, tq), :], qt[h]) for h in range(nh)]
        if masked:
            s = [jnp.where(causal, s[h], NEG_BIG) for h in range(nh)]
        m_new = [jnp.maximum(m[h], jnp.max(s[h], axis=0, keepdims=True)) for h in range(nh)]
        p = [jnp.exp2(s[h] - m_new[h]).astype(BF16) for h in range(nh)]
        alpha = [jnp.exp2(m[h] - m_new[h]) for h in range(nh)]
        pv = [_dot(vt_ref[0, h, j], p[h]) for h in range(nh)]
        for h in range(nh):
            acc_scr[h] = alpha[h] * acc_scr[h] + pv[h]
        return tuple(m_new)

    m0 = tuple(jnp.full((1, tq), NEG_BIG, F32) for _ in range(nh))
    m = lax.fori_loop(0, i, functools.partial(step, masked=False), m0)
    step(i, m, True)
    ot = [acc_scr[h, :HEAD_DIM, :] * (1.0 / acc_scr[h, HEAD_DIM:HEAD_DIM + 1, :]) for h in range(nh)]
    o_ref[0] = jnp.concatenate(ot, axis=0).T.astype(o_ref.dtype)


def _fox_attn(qt, ka, vt, *, tq, nh):
    B, H, _, T = qt.shape
    nb = T // tq
    return pl.pallas_call(
        functools.partial(_fox_attn_kernel, tq=tq, nh=nh),
        grid=(B, H // nh, nb),
        in_specs=[pl.BlockSpec((1, nh, LANES, tq), lambda b, p, i: (b, p, 0, i)),
                  pl.BlockSpec((1, nh, T, LANES), lambda b, p, i: (b, p, 0, 0)),
                  pl.BlockSpec((1, nh, nb, V_ROWS, tq), lambda b, p, i: (b, p, 0, 0, 0))],
        out_specs=pl.BlockSpec((1, tq, nh * HEAD_DIM), lambda b, p, i: (b, i, p)),
        out_shape=jax.ShapeDtypeStruct((B, T, H * HEAD_DIM), BF16),
        scratch_shapes=[pltpu.VMEM((nh, V_ROWS, tq), F32)],
        compiler_params=_params(("parallel", "parallel", "arbitrary")),
        name="fox_attn",
    )(qt, ka, vt)


def kernel(x, rwkv_norm_g, rwkv_mu, rwkv_w_rkv, rwkv_w0, rwkv_w1, rwkv_w2, rwkv_a0, rwkv_a1, rwkv_a2, rwkv_g1, rwkv_g2, rwkv_k_k, rwkv_k_a, rwkv_r_k, rwkv_lnx_w, rwkv_lnx_b, rwkv_w_o, kv_norm_g, kv_w, kv_f_bias, k_norm_g, attn_norm_g, attn_w_q, q_norm_g, attn_w_o, mlp_norm_g, mlp_w_in, mlp_w_out):
    B, T, D = x.shape
    M = B * T
    n_a = rwkv_norm_g.shape[0]
    depth = mlp_norm_g.shape[0]
    bf = lambda w: w.astype(BF16)
    tm = min(512, T)
    tq = min(256, T)
    tf = min(1024, mlp_w_in.shape[-1])
    npairs = D // LANES

    w_in_all, w_out_all = bf(mlp_w_in), bf(mlp_w_out)
    k_sh = v_sh = c_sh = None
    for layer in range(depth):
        if layer < n_a:
            i = layer
            r, lw, k, v, kk, b, g = _rwkv_prep(
                x, rwkv_norm_g[i], rwkv_mu[i], bf(rwkv_w_rkv[i, 0]), bf(rwkv_w_rkv[i, 1]),
                bf(rwkv_w_rkv[i, 2]), rwkv_w0[i], bf(rwkv_w1[i]), bf(rwkv_w2[i]), rwkv_a0[i],
                bf(rwkv_a1[i]), bf(rwkv_a2[i]), bf(rwkv_g1[i]), bf(rwkv_g2[i]),
                rwkv_k_k[i], rwkv_k_a[i], tm=tm)
            mix = _rwkv_scan(r, lw, k, v, kk, b, nbatch=2 if B % 2 == 0 else 1, npairs=npairs)
            flat = lambda t: t.reshape(M, D)
            w_o, extra = rwkv_w_o[i], (flat(r), flat(k), flat(v), flat(g), rwkv_r_k[i].reshape(D),
                                       rwkv_lnx_w[i], rwkv_lnx_b[i])
        else:
            j = layer - n_a
            qt = _q_proj(x, attn_norm_g[j], bf(attn_w_q[j].T), q_norm_g[j], c_sh,
                         tm=tm, scale=HEAD_DIM ** -0.5 * LOG2E)
            mix = _fox_attn(qt, k_sh, v_sh, tq=tq, nh=min(16, D // HEAD_DIM))
            w_o, extra = attn_w_o[j], None
        x = _proj_mlp(x.reshape(M, D), mix.reshape(M, D), bf(w_o), mlp_norm_g[layer],
                      w_in_all, w_out_all, layer=layer, tm=tm, tf=tf, rwkv=extra).reshape(B, T, D)
        if layer == n_a - 1:
            wf = jnp.pad(kv_w[:, 2 * D:], ((0, 0), (0, LANES - (kv_w.shape[1] - 2 * D))))
            fb = jnp.pad(kv_f_bias, (0, LANES - kv_f_bias.shape[0])).reshape(1, LANES)
            k_sh, v_sh, c_sh = _shared_kv(
                x, kv_norm_g, bf(kv_w[:, :D]), bf(kv_w[:, D:2 * D].T), bf(wf), fb, k_norm_g, tm=tm, tq=tq)
    return x
```

```python
import functools

import jax
import jax.numpy as jnp
import numpy as np
from jax import lax
from jax.experimental import pallas as pl
from jax.experimental.pallas import tpu as pltpu

HEAD_DIM = 64
LANES = 128
NORM_EPS = 1e-6
GN_EPS = 64e-5
CHUNK = 64
NEG_BIG = -1e30
LOG2E = 1.4426950408889634
EXP_M_HALF = 0.6065306597126334
V_ROWS = 80
VMEM_LIMIT = 56 * 1024 * 1024

BF16 = jnp.bfloat16
F32 = jnp.float32

_NT = (((1,), (1,)), ((), ()))
_TN = (((0,), (0,)), ((), ()))


def _dot(a, b):
    return jnp.dot(a, b, preferred_element_type=F32)


def _dot_nt(a, b):
    return lax.dot_general(a, b, _NT, preferred_element_type=F32)


def _dot_tn(a, b):
    return lax.dot_general(a, b, _TN, preferred_element_type=F32)


def _split(a):
    hi = a.astype(BF16)
    return hi, (a - hi.astype(F32)).astype(BF16)


def _split3(a):
    hi = a.astype(BF16).astype(F32)
    r1 = a - hi
    mid = r1.astype(BF16).astype(F32)
    return hi, mid, r1 - mid


def _rms(x, g):
    return x * lax.rsqrt(jnp.mean(x * x, axis=-1, keepdims=True) + NORM_EPS) * g


def _head_sum(x):
    outs = []
    for c in range(x.shape[1] // LANES):
        xc = x[:, c * LANES:(c + 1) * LANES]
        lo = lax.broadcasted_iota(jnp.int32, xc.shape, 1) < HEAD_DIM
        s0 = jnp.sum(jnp.where(lo, xc, 0.0), axis=1, keepdims=True)
        s1 = jnp.sum(jnp.where(lo, 0.0, xc), axis=1, keepdims=True)
        outs.append(jnp.where(lo, s0, s1))
    return outs[0] if len(outs) == 1 else jnp.concatenate(outs, axis=1)


def _head_rms(t, g):
    ms = _head_sum(t * t) * (1.0 / HEAD_DIM)
    return t * lax.rsqrt(ms + NORM_EPS) * g


def _sigmoid(z):
    return 1.0 / (1.0 + jnp.exp(-z))


def _const_spec(shape):
    nd = len(shape)
    return pl.BlockSpec(shape, lambda *_: (0,) * nd)


def _params(sem):
    return pltpu.CompilerParams(dimension_semantics=sem, vmem_limit_bytes=VMEM_LIMIT)


def _rwkv_prep_kernel(x_ref, xp_ref, ng_ref, mu_ref, wr_ref, wk_ref, wv_ref,
                      w0_ref, w1_ref, w2_ref, a0_ref, a1_ref, a2_ref, g1_ref, g2_ref,
                      kkw_ref, kaw_ref,
                      r_out, lw_out, k_out, v_out, kk_out, b_out, g_out):
    i = pl.program_id(1)
    ng = ng_ref[...]
    h = _rms(x_ref[0], ng)
    hp = _rms(xp_ref[0][7:8, :], ng)
    hp = jnp.where(i > 0, hp, 0.0)
    rolled = pltpu.roll(h, 1, 0)
    first = jnp.where(lax.broadcasted_iota(jnp.int32, (8, h.shape[1]), 0) == 0, hp, rolled[:8])
    hs = jnp.concatenate([first, rolled[8:]], axis=0)
    hb = h.astype(BF16)
    xxb = (hs - h).astype(BF16)
    mub = mu_ref[...].astype(BF16)

    def mix(j):
        return hb + xxb * mub[j:j + 1, :]

    tw = _dot(mix(1), w1_ref[...])
    ta = _dot(mix(4), a1_ref[...])
    tg = _dot(mix(5), g1_ref[...])
    k = _dot(mix(2), wk_ref[...])
    wl = _dot(jnp.tanh(tw).astype(BF16), w2_ref[...])
    al = _dot(ta.astype(BF16), a2_ref[...])
    g_out[0] = _dot(_sigmoid(tg).astype(BF16), g2_ref[...]).astype(g_out.dtype)

    lw_out[0] = -EXP_M_HALF * _sigmoid(w0_ref[...] + wl)
    a = _sigmoid(a0_ref[...] + al)
    kk = k * kkw_ref[...]
    kk = kk * lax.rsqrt(jnp.maximum(_head_sum(kk * kk), 1e-24))
    k_out[0] = (k * (1.0 + (a - 1.0) * kaw_ref[...])).astype(k_out.dtype)
    kk_out[0] = kk.astype(kk_out.dtype)
    b_out[0] = (kk * a).astype(b_out.dtype)

    r_out[0] = _dot(mix(0), wr_ref[...]).astype(r_out.dtype)
    v_out[0] = _dot(mix(3), wv_ref[...]).astype(v_out.dtype)


def _rwkv_prep(x, ng, mu, wr, wk, wv, w0, w1, w2, a0, a1, a2, g1, g2, kkw, kaw, *, tm):
    B, T, D = x.shape
    row = lambda a: a.reshape(1, D)
    consts = [row(ng), mu, wr, wk, wv, row(w0), w1, w2, row(a0), a1, a2, g1, g2, row(kkw), row(kaw)]
    tile = pl.BlockSpec((1, tm, D), lambda b, i: (b, i, 0))
    prev = pl.BlockSpec((1, 8, D), lambda b, i: (b, jnp.maximum(i * (tm // 8) - 1, 0), 0))
    out = lambda dt: jax.ShapeDtypeStruct((B, T, D), dt)
    return pl.pallas_call(
        _rwkv_prep_kernel,
        grid=(B, T // tm),
        in_specs=[tile, prev] + [_const_spec(c.shape) for c in consts],
        out_specs=[tile] * 7,
        out_shape=[out(BF16), out(F32)] + [out(BF16)] * 5,
        compiler_params=_params(("parallel", "parallel")),
        name="rwkv_prep",
    )(x, x, *consts)


def _blockdiag(z, lo):
    z = z.astype(BF16)
    zero = jnp.zeros_like(z)
    return jnp.concatenate([jnp.where(lo, z, zero), jnp.where(lo, zero, z)], axis=0)


def _rwkv_scan_kernel(r_ref, lw_ref, k_ref, v_ref, kk_ref, b_ref, y_out, s_scr, *, nbatch, npairs):
    c = pl.program_id(2)

    @pl.when(c == 0)
    def _():
        s_scr[...] = jnp.zeros_like(s_scr)

    C = CHUNK
    t_i = lax.broadcasted_iota(jnp.int32, (C, C), 0)
    j_i = lax.broadcasted_iota(jnp.int32, (C, C), 1)
    ltri = (j_i <= t_i).astype(BF16)
    row = lax.broadcasted_iota(jnp.int32, (C, LANES), 0)
    lane = lax.broadcasted_iota(jnp.int32, (C, LANES), 1)
    lo = lane < HEAD_DIM
    col = jnp.bitwise_and(lane, HEAD_DIM - 1)
    strict = col < row
    incl = col <= row
    rr = lax.broadcasted_iota(jnp.int32, (LANES, LANES), 0)
    cc = lax.broadcasted_iota(jnp.int32, (LANES, LANES), 1)
    same_head = (rr < HEAD_DIM) == (cc < HEAD_DIM)
    bd = functools.partial(_blockdiag, lo=lo)
    cat0 = lambda *xs: jnp.concatenate([x.astype(BF16) for x in xs], axis=0)
    cat1 = lambda *xs: jnp.concatenate([x.astype(BF16) for x in xs], axis=1)

    units = [(bi, slice(p * LANES, (p + 1) * LANES)) for bi in range(nbatch) for p in range(npairs)]
    P = range(len(units))
    r = [r_ref[bi, :, sl].astype(F32) for bi, sl in units]
    lw = [lw_ref[bi, :, sl] for bi, sl in units]
    k = [k_ref[bi, :, sl].astype(F32) for bi, sl in units]
    v = [v_ref[bi, :, sl].astype(F32) for bi, sl in units]
    kk = [kk_ref[bi, :, sl].astype(F32) for bi, sl in units]
    b = [b_ref[bi, :, sl].astype(F32) for bi, sl in units]

    cw2 = [_dot(ltri, cat1(*_split(lw[p]))) for p in P]
    cw = [cw2[p][:, :LANES] + cw2[p][:, LANES:] for p in P]
    cwl = [cw[p][C - 1:C, :] for p in P]
    at = [-kk[p] * jnp.exp(cw[p] - lw[p]) for p in P]
    dinv = [jnp.exp(-cw[p]) for p in P]
    rt = [r[p] * jnp.exp(cw[p]) for p in P]
    dend = [jnp.exp(cwl[p] - cw[p]) for p in P]

    x = [_dot_nt(cat0(at[p], rt[p]), cat0(bd(b[p] * dinv[p]), bd(k[p] * dinv[p]))) for p in P]
    aab = [jnp.where(strict, x[p][:C, :LANES], 0.0) for p in P]
    arb = [jnp.where(incl, x[p][C:, :LANES], 0.0) for p in P]
    aak = [jnp.where(strict, x[p][:C, LANES:], 0.0) for p in P]
    ark = [jnp.where(incl, x[p][C:, LANES:], 0.0) for p in P]

    bdv = [bd(v[p]) for p in P]
    av = [_dot(aak[p].astype(BF16), bdv[p]) for p in P]
    eye = jnp.where(col == row, 1.0, 0.0)
    tinv = [eye + aab[p] for p in P]
    n = [_dot(aab[p].astype(BF16), bd(aab[p])) for p in P]
    for it in range(5):
        last = it == 4
        res = [_dot(n[p].astype(BF16),
                    jnp.concatenate([bd(tinv[p])] + ([] if last else [bd(n[p])]), axis=1)) for p in P]
        tinv = [tinv[p] + res[p][:, :LANES] for p in P]
        if not last:
            n = [res[p][:, LANES:] for p in P]
    z = [_dot(tinv[p].astype(BF16), jnp.concatenate([bd(at[p]), bd(av[p])], axis=1)) for p in P]
    z1 = [z[p][:, :LANES] for p in P]
    z2 = [z[p][:, LANES:] for p in P]

    s = [s_scr[p] for p in P]
    ws = [_dot_nt(cat0(z1[p], rt[p]), s[p].astype(BF16)) for p in P]
    u = [ws[p][:C] + z2[p] for p in P]
    y = [ws[p][C:] + _dot(cat1(arb[p], ark[p]), cat0(bd(u[p]), bdv[p])) for p in P]
    upd = [_dot_tn(cat0(u[p], v[p]), cat0(b[p] * dend[p], k[p] * dend[p])) for p in P]
    for p in P:
        s_scr[p] = s[p] * jnp.exp(cwl[p]) + jnp.where(same_head, upd[p], 0.0)

    for p in P:
        bi, sl = units[p]
        y_out[bi, :, sl] = y[p]


def _rwkv_scan(r, lw, k, v, kk, b, *, nbatch, npairs):
    B, T, D = r.shape
    W = npairs * LANES
    tile = pl.BlockSpec((nbatch, CHUNK, W), lambda bi, p, c: (bi, c, p))
    return pl.pallas_call(
        functools.partial(_rwkv_scan_kernel, nbatch=nbatch, npairs=npairs),
        grid=(B // nbatch, D // W, T // CHUNK),
        in_specs=[tile] * 6,
        out_specs=tile,
        out_shape=jax.ShapeDtypeStruct((B, T, D), F32),
        scratch_shapes=[pltpu.VMEM((nbatch * npairs, LANES, LANES), F32)],
        compiler_params=_params(("parallel", "parallel", "arbitrary")),
        name="rwkv_scan",
    )(r, lw, k, v, kk, b)


def _mlp_tail(x, g_ref, win_ref, wout_ref, o_ref, tf):
    xn = _rms(x, g_ref[...]).astype(BF16)
    acc = x
    for f in range(win_ref.shape[-1] // tf):
        hid = jnp.maximum(_dot(xn, win_ref[:, f * tf:(f + 1) * tf]), 0.0)
        acc = acc + _dot((hid * hid).astype(BF16), wout_ref[f * tf:(f + 1) * tf, :])
    o_ref[...] = acc


def _proj_mlp_kernel(res_ref, a_ref, wo_ref, g_ref, win_ref, wout_ref, o_ref, *, tf):
    x = res_ref[...] + _dot(a_ref[...], wo_ref[...])
    _mlp_tail(x, g_ref, win_ref, wout_ref, o_ref, tf)


def _rwkv_out_mlp_kernel(res_ref, y_ref, r_ref, k_ref, v_ref, gate_ref, rk_ref, lnw_ref, lnb_ref,
                         wo_ref, g_ref, win_ref, wout_ref, o_ref, a_scr, *, tf):
    i = pl.program_id(0)

    def output_stage():
        y = y_ref[...]
        d = y - _head_sum(y) * (1.0 / HEAD_DIM)
        var = _head_sum(d * d) * (1.0 / HEAD_DIM)
        yn = d * lax.rsqrt(var + GN_EPS)
        r = r_ref[...].astype(F32)
        k = k_ref[...].astype(F32)
        bonus = _head_sum(r * k * rk_ref[...]) * v_ref[...].astype(F32)
        return ((yn * lnw_ref[...] + lnb_ref[...] + bonus) * gate_ref[...].astype(F32)).astype(BF16)

    @pl.when(i == 0)
    def _():
        a_scr[...] = output_stage()

    @pl.when(i > 0)
    def _():
        x = res_ref[...] + _dot(a_scr[...], wo_ref[...])
        _mlp_tail(x, g_ref, win_ref, wout_ref, o_ref, tf)
        a_scr[...] = output_stage()


def _proj_mlp(res, mix, w_o, g, w_in, w_out, *, layer, tm, tf, rwkv=None):
    M, D = res.shape
    tile = pl.BlockSpec((tm, D), lambda i: (i, 0))
    once = lambda shape: pl.BlockSpec(shape, lambda i: (0,) * len(shape), pipeline_mode=pl.Buffered(1))
    of_layer = lambda w: pl.BlockSpec((None,) + w.shape[1:], lambda i: (layer, 0, 0),
                                      pipeline_mode=pl.Buffered(1))
    vec = once((1, D))
    weights = [once(w_o.shape), vec, of_layer(w_in), of_layer(w_out)]
    wargs = (w_o, g.reshape(1, D), w_in, w_out)
    n = M // tm
    if rwkv is None:
        body, grid, specs, args, out_spec, scratch = _proj_mlp_kernel, n, [tile, tile], (res, mix), tile, []
    else:
        r, k, v, gate, rk, lnw, lnb = rwkv
        prev = pl.BlockSpec((tm, D), lambda i: (jnp.maximum(i - 1, 0), 0))
        cur = pl.BlockSpec((tm, D), lambda i: (jnp.minimum(i, n - 1), 0))
        body, grid, specs, out_spec = _rwkv_out_mlp_kernel, n + 1, [prev] + [cur] * 5 + [vec] * 3, prev
        args = (res, mix, r, k, v, gate, rk.reshape(1, D), lnw.reshape(1, D), lnb.reshape(1, D))
        scratch = [pltpu.VMEM((tm, D), BF16)]
    return pl.pallas_call(
        functools.partial(body, tf=tf),
        grid=(grid,),
        in_specs=specs + weights,
        out_specs=out_spec,
        out_shape=jax.ShapeDtypeStruct((M, D), F32),
        scratch_shapes=scratch,
        compiler_params=_params(("arbitrary",)),
        name="proj_mlp",
    )(*args, *wargs)


def _q_proj_kernel(x_ref, g_ref, wt_ref, qg_ref, c_ref, q_out, *, scale, tq):
    hn = _rms(x_ref[0], g_ref[...])
    qt = _dot_nt(wt_ref[...], hn.astype(BF16))
    tm = hn.shape[0]
    row = lax.broadcasted_iota(jnp.int32, (HEAD_DIM, tm), 0)
    for h in range(qt.shape[0] // HEAD_DIM):
        hs = slice(h * HEAD_DIM, (h + 1) * HEAD_DIM)
        qh = qt[hs, :]
        ms = jnp.mean(qh * qh, axis=0, keepdims=True)
        qn = qh * lax.rsqrt(ms + NORM_EPS) * (qg_ref[hs, :] * scale)
        hi, mid, lo = _split3(c_ref[0, h:h + 1, :] * LOG2E)
        aug = jnp.where(row == 0, hi, jnp.where(row == 1, mid, jnp.where(
            row == 2, lo, jnp.where(row < 6, 1.0, 0.0))))
        tile = jnp.concatenate([qn, aug], axis=0).astype(q_out.dtype)
        for sb in range(tm // tq):
            q_out[0, h, sb] = tile[:, sb * tq:(sb + 1) * tq]


def _q_proj(x, g, wt, qg, c_row, *, tm, tq, scale):
    B, T, D = x.shape
    H = D // HEAD_DIM
    return pl.pallas_call(
        functools.partial(_q_proj_kernel, scale=scale, tq=tq),
        grid=(B, T // tm),
        in_specs=[pl.BlockSpec((1, tm, D), lambda b, i: (b, i, 0)), _const_spec((1, D)),
                  _const_spec(wt.shape), _const_spec((D, 1)),
                  pl.BlockSpec((1, H, tm), lambda b, i: (b, 0, i))],
        out_specs=pl.BlockSpec((1, H, tm // tq, LANES, tq), lambda b, i: (b, 0, i, 0, 0)),
        out_shape=jax.ShapeDtypeStruct((B, H, T // tq, LANES, tq), BF16),
        compiler_params=_params(("parallel", "parallel")),
        name="q_proj",
    )(x, g.reshape(1, D), wt, qg.reshape(D, 1), c_row)


def _shared_kv_kernel(x_ref, g_ref, wk_ref, wvt_ref, wf_ref, fb_ref, kg_ref, sel_ref,
                      k_out, vt_out, c_out, carry_scr, *, tq):
    i = pl.program_id(1)

    @pl.when(i == 0)
    def _():
        carry_scr[...] = jnp.zeros_like(carry_scr)

    hn = _rms(x_ref[0], g_ref[...])
    hb = hn.astype(BF16)
    tm, D = hn.shape
    H = D // HEAD_DIM
    k = _head_rms(_dot(hb, wk_ref[...]), kg_ref[...])
    vt = _dot_nt(wvt_ref[...], hb)

    f = _dot(hb, wf_ref[...]) + fb_ref[...]
    logf = jnp.minimum(f, 0.0) - jnp.log(1.0 + jnp.exp(-jnp.abs(f)))
    t_i = lax.broadcasted_iota(jnp.int32, (tm, tm), 0)
    j_i = lax.broadcasted_iota(jnp.int32, (tm, tm), 1)
    ltri = (j_i <= t_i).astype(BF16)
    c3 = _dot(ltri, jnp.concatenate([t.astype(BF16) for t in _split3(logf)], axis=1))
    c = (c3[:, :LANES] + c3[:, LANES:2 * LANES]) + c3[:, 2 * LANES:] + carry_scr[0:1, :]
    carry_scr[...] = jnp.broadcast_to(c[tm - 1:tm, :], carry_scr.shape)
    c_out[0] = c.T[:H, :]

    lane = lax.broadcasted_iota(jnp.int32, (tm, LANES), 1)
    hi, mid, lo = (jnp.where(lane < H, t, 0.0) for t in _split3(c * (-LOG2E)))
    packed = (hi + pltpu.roll(mid, H, 1)) + (pltpu.roll(lo, 2 * H, 1) + jnp.where(lane == 3 * H, 1.0, 0.0))
    aug = _dot(packed.astype(BF16), sel_ref[...])
    vrow = lax.broadcasted_iota(jnp.int32, (V_ROWS - HEAD_DIM, tq), 0)
    ones_row = jnp.where(vrow == 0, 1.0, 0.0)
    for h in range(H):
        base = k[:, (h // 2) * LANES:(h // 2 + 1) * LANES]
        if h % 2:
            base = pltpu.roll(base, HEAD_DIM, 1)
        tile = jnp.where(lane < HEAD_DIM, base, aug[:, h * LANES:(h + 1) * LANES])
        k_out[0, h] = tile.astype(k_out.dtype)
        for sb in range(tm // tq):
            vt_out[0, h, sb] = jnp.concatenate(
                [vt[h * HEAD_DIM:(h + 1) * HEAD_DIM, sb * tq:(sb + 1) * tq], ones_row],
                axis=0).astype(vt_out.dtype)


def _bias_selector(H):
    assert 3 * H + 1 <= LANES
    sel = np.zeros((LANES, H * LANES), np.float32)
    for h in range(H):
        sel[3 * H, h * LANES + HEAD_DIM:h * LANES + HEAD_DIM + 3] = 1.0
        for t in range(3):
            sel[t * H + h, h * LANES + HEAD_DIM + 3 + t] = 1.0
    return jnp.asarray(sel, BF16)


def _shared_kv(x, g, wk, wvt, wf, fb, kg, *, tm, tq):
    B, T, D = x.shape
    H = D // HEAD_DIM
    sel = _bias_selector(H)
    return pl.pallas_call(
        functools.partial(_shared_kv_kernel, tq=tq),
        grid=(B, T // tm),
        in_specs=[pl.BlockSpec((1, tm, D), lambda b, i: (b, i, 0)), _const_spec((1, D)),
                  _const_spec(wk.shape), _const_spec(wvt.shape), _const_spec(wf.shape),
                  _const_spec((1, LANES)), _const_spec((1, D)), _const_spec(sel.shape)],
        out_specs=[pl.BlockSpec((1, H, tm, LANES), lambda b, i: (b, 0, i, 0)),
                   pl.BlockSpec((1, H, tm // tq, V_ROWS, tq), lambda b, i: (b, 0, i, 0, 0)),
                   pl.BlockSpec((1, H, tm), lambda b, i: (b, 0, i))],
        out_shape=[jax.ShapeDtypeStruct((B, H, T, LANES), BF16),
                   jax.ShapeDtypeStruct((B, H, T // tq, V_ROWS, tq), BF16),
                   jax.ShapeDtypeStruct((B, H, T), F32)],
        scratch_shapes=[pltpu.VMEM((8, LANES), F32)],
        compiler_params=_params(("parallel", "arbitrary")),
        name="shared_kv",
    )(x, g.reshape(1, D), wk, wvt, wf, fb, kg.reshape(1, D), sel)


def _fox_attn_kernel(q_ref, k_ref, vt_ref, o_ref, acc_scr, *, tq, nh):
    i = pl.program_id(2)
    qt = [q_ref[0, h, 0] for h in range(nh)]
    acc_scr[...] = jnp.zeros_like(acc_scr)
    key_i = lax.broadcasted_iota(jnp.int32, (tq, tq), 0)
    qry_i = lax.broadcasted_iota(jnp.int32, (tq, tq), 1)
    causal = key_i <= qry_i

    def step(j, m, masked):
        off = pl.multiple_of(j * tq, tq)
        s = [_dot(k_ref[0, h, pl.ds(off, tq), :], qt[h]) for h in range(nh)]
        if masked:
            s = [jnp.where(causal, s[h], NEG_BIG) for h in range(nh)]
        m_new = [jnp.maximum(m[h], jnp.max(s[h], axis=0, keepdims=True)) for h in range(nh)]
        p = [jnp.exp2(s[h] - m_new[h]).astype(BF16) for h in range(nh)]
        alpha = [jnp.exp2(m[h] - m_new[h]) for h in range(nh)]
        pv = [_dot(vt_ref[0, h, j], p[h]) for h in range(nh)]
        for h in range(nh):
            acc_scr[h] = alpha[h] * acc_scr[h] + pv[h]
        return tuple(m_new)

    m0 = tuple(jnp.full((1, tq), NEG_BIG, F32) for _ in range(nh))
    m = lax.fori_loop(0, i, functools.partial(step, masked=False), m0)
    step(i, m, True)
    ot = [acc_scr[h, :HEAD_DIM, :] * (1.0 / acc_scr[h, HEAD_DIM:HEAD_DIM + 1, :]) for h in range(nh)]
    o_ref[0] = jnp.concatenate(ot, axis=0).T.astype(o_ref.dtype)


def _fox_attn(qt, ka, vt, *, nh):
    B, H, nb, _, tq = qt.shape
    T = nb * tq
    return pl.pallas_call(
        functools.partial(_fox_attn_kernel, tq=tq, nh=nh),
        grid=(B, H // nh, nb),
        in_specs=[pl.BlockSpec((1, nh, 1, LANES, tq), lambda b, p, i: (b, p, i, 0, 0)),
                  pl.BlockSpec((1, nh, T, LANES), lambda b, p, i: (b, p, 0, 0)),
                  pl.BlockSpec((1, nh, nb, V_ROWS, tq), lambda b, p, i: (b, p, 0, 0, 0))],
        out_specs=pl.BlockSpec((1, tq, nh * HEAD_DIM), lambda b, p, i: (b, i, p)),
        out_shape=jax.ShapeDtypeStruct((B, T, H * HEAD_DIM), BF16),
        scratch_shapes=[pltpu.VMEM((nh, V_ROWS, tq), F32)],
        compiler_params=_params(("parallel", "parallel", "arbitrary")),
        name="fox_attn",
    )(qt, ka, vt)


def kernel(x, rwkv_norm_g, rwkv_mu, rwkv_w_rkv, rwkv_w0, rwkv_w1, rwkv_w2, rwkv_a0, rwkv_a1, rwkv_a2, rwkv_g1, rwkv_g2, rwkv_k_k, rwkv_k_a, rwkv_r_k, rwkv_lnx_w, rwkv_lnx_b, rwkv_w_o, kv_norm_g, kv_w, kv_f_bias, k_norm_g, attn_norm_g, attn_w_q, q_norm_g, attn_w_o, mlp_norm_g, mlp_w_in, mlp_w_out):
    B, T, D = x.shape
    M = B * T
    n_a = rwkv_norm_g.shape[0]
    depth = mlp_norm_g.shape[0]
    bf = lambda w: w.astype(BF16)
    tm = min(512, T)
    tq = min(256, T)
    tf = min(1024, mlp_w_in.shape[-1])
    npairs = D // LANES

    w_in_all, w_out_all = bf(mlp_w_in), bf(mlp_w_out)
    k_sh = v_sh = c_sh = None
    for layer in range(depth):
        if layer < n_a:
            i = layer
            r, lw, k, v, kk, b, g = _rwkv_prep(
                x, rwkv_norm_g[i], rwkv_mu[i], bf(rwkv_w_rkv[i, 0]), bf(rwkv_w_rkv[i, 1]),
                bf(rwkv_w_rkv[i, 2]), rwkv_w0[i], bf(rwkv_w1[i]), bf(rwkv_w2[i]), rwkv_a0[i],
                bf(rwkv_a1[i]), bf(rwkv_a2[i]), bf(rwkv_g1[i]), bf(rwkv_g2[i]),
                rwkv_k_k[i], rwkv_k_a[i], tm=tm)
            mix = _rwkv_scan(r, lw, k, v, kk, b, nbatch=2 if B % 2 == 0 else 1, npairs=npairs)
            flat = lambda t: t.reshape(M, D)
            w_o, extra = rwkv_w_o[i], (flat(r), flat(k), flat(v), flat(g), rwkv_r_k[i].reshape(D),
                                       rwkv_lnx_w[i], rwkv_lnx_b[i])
        else:
            j = layer - n_a
            qt = _q_proj(x, attn_norm_g[j], bf(attn_w_q[j].T), q_norm_g[j], c_sh,
                         tm=tm, tq=tq, scale=HEAD_DIM ** -0.5 * LOG2E)
            mix = _fox_attn(qt, k_sh, v_sh, nh=min(16, D // HEAD_DIM))
            w_o, extra = attn_w_o[j], None
        x = _proj_mlp(x.reshape(M, D), mix.reshape(M, D), bf(w_o), mlp_norm_g[layer],
                      w_in_all, w_out_all, layer=layer, tm=tm, tf=tf, rwkv=extra).reshape(B, T, D)
        if layer == n_a - 1:
            wf = jnp.pad(kv_w[:, 2 * D:], ((0, 0), (0, LANES - (kv_w.shape[1] - 2 * D))))
            fb = jnp.pad(kv_f_bias, (0, LANES - kv_f_bias.shape[0])).reshape(1, LANES)
            k_sh, v_sh, c_sh = _shared_kv(
                x, kv_norm_g, bf(kv_w[:, :D]), bf(kv_w[:, D:2 * D].T), bf(wf), fb, k_norm_g, tm=tm, tq=tq)
    return x
```

```python
import functools

import jax
import jax.numpy as jnp
import numpy as np
from jax import lax
from jax.experimental import pallas as pl
from jax.experimental.pallas import tpu as pltpu

HEAD_DIM = 64
LANES = 128
NORM_EPS = 1e-6
GN_EPS = 64e-5
CHUNK = 64
NEG_BIG = -1e30
LOG2E = 1.4426950408889634
EXP_M_HALF = 0.6065306597126334
V_ROWS = 80
VMEM_LIMIT = 56 * 1024 * 1024

BF16 = jnp.bfloat16
F32 = jnp.float32

_NT = (((1,), (1,)), ((), ()))
_TN = (((0,), (0,)), ((), ()))


def _dot(a, b):
    return jnp.dot(a, b, preferred_element_type=F32)


def _dot_nt(a, b):
    return lax.dot_general(a, b, _NT, preferred_element_type=F32)


def _dot_tn(a, b):
    return lax.dot_general(a, b, _TN, preferred_element_type=F32)


def _split(a):
    hi = a.astype(BF16)
    return hi, (a - hi.astype(F32)).astype(BF16)


def _split3(a):
    hi = a.astype(BF16).astype(F32)
    r1 = a - hi
    mid = r1.astype(BF16).astype(F32)
    return hi, mid, r1 - mid


def _rms(x, g):
    return x * lax.rsqrt(jnp.mean(x * x, axis=-1, keepdims=True) + NORM_EPS) * g


def _head_sum(x):
    outs = []
    for c in range(x.shape[1] // LANES):
        xc = x[:, c * LANES:(c + 1) * LANES]
        lo = lax.broadcasted_iota(jnp.int32, xc.shape, 1) < HEAD_DIM
        s0 = jnp.sum(jnp.where(lo, xc, 0.0), axis=1, keepdims=True)
        s1 = jnp.sum(jnp.where(lo, 0.0, xc), axis=1, keepdims=True)
        outs.append(jnp.where(lo, s0, s1))
    return outs[0] if len(outs) == 1 else jnp.concatenate(outs, axis=1)


def _head_rms(t, g):
    ms = _head_sum(t * t) * (1.0 / HEAD_DIM)
    return t * lax.rsqrt(ms + NORM_EPS) * g


def _sigmoid(z):
    return 1.0 / (1.0 + jnp.exp(-z))


def _const_spec(shape):
    nd = len(shape)
    return pl.BlockSpec(shape, lambda *_: (0,) * nd)


def _params(sem):
    return pltpu.CompilerParams(dimension_semantics=sem, vmem_limit_bytes=VMEM_LIMIT)


def _rwkv_prep_kernel(x_ref, xp_ref, ng_ref, mu_ref, wr_ref, wk_ref, wv_ref,
                      w0_ref, w1_ref, w2_ref, a0_ref, a1_ref, a2_ref, g1_ref, g2_ref,
                      kkw_ref, kaw_ref,
                      r_out, lw_out, k_out, v_out, kk_out, b_out, g_out):
    i = pl.program_id(1)
    ng = ng_ref[...]
    h = _rms(x_ref[0], ng)
    hp = _rms(xp_ref[0][7:8, :], ng)
    hp = jnp.where(i > 0, hp, 0.0)
    rolled = pltpu.roll(h, 1, 0)
    first = jnp.where(lax.broadcasted_iota(jnp.int32, (8, h.shape[1]), 0) == 0, hp, rolled[:8])
    hs = jnp.concatenate([first, rolled[8:]], axis=0)
    hb = h.astype(BF16)
    xxb = (hs - h).astype(BF16)
    mub = mu_ref[...].astype(BF16)

    def mix(j):
        return hb + xxb * mub[j:j + 1, :]

    tw = _dot(mix(1), w1_ref[...])
    ta = _dot(mix(4), a1_ref[...])
    tg = _dot(mix(5), g1_ref[...])
    k = _dot(mix(2), wk_ref[...])
    wl = _dot(jnp.tanh(tw).astype(BF16), w2_ref[...])
    al = _dot(ta.astype(BF16), a2_ref[...])
    g_out[0] = _dot(_sigmoid(tg).astype(BF16), g2_ref[...]).astype(g_out.dtype)

    lw_out[0] = -EXP_M_HALF * _sigmoid(w0_ref[...] + wl)
    a = _sigmoid(a0_ref[...] + al)
    kk = k * kkw_ref[...]
    kk = kk * lax.rsqrt(jnp.maximum(_head_sum(kk * kk), 1e-24))
    k_out[0] = (k * (1.0 + (a - 1.0) * kaw_ref[...])).astype(k_out.dtype)
    kk_out[0] = kk.astype(kk_out.dtype)
    b_out[0] = (kk * a).astype(b_out.dtype)

    r_out[0] = _dot(mix(0), wr_ref[...]).astype(r_out.dtype)
    v_out[0] = _dot(mix(3), wv_ref[...]).astype(v_out.dtype)


def _rwkv_prep(x, ng, mu, wr, wk, wv, w0, w1, w2, a0, a1, a2, g1, g2, kkw, kaw, *, tm):
    B, T, D = x.shape
    row = lambda a: a.reshape(1, D)
    consts = [row(ng), mu, wr, wk, wv, row(w0), w1, w2, row(a0), a1, a2, g1, g2, row(kkw), row(kaw)]
    tile = pl.BlockSpec((1, tm, D), lambda b, i: (b, i, 0))
    prev = pl.BlockSpec((1, 8, D), lambda b, i: (b, jnp.maximum(i * (tm // 8) - 1, 0), 0))
    out = lambda dt: jax.ShapeDtypeStruct((B, T, D), dt)
    return pl.pallas_call(
        _rwkv_prep_kernel,
        grid=(B, T // tm),
        in_specs=[tile, prev] + [_const_spec(c.shape) for c in consts],
        out_specs=[tile] * 7,
        out_shape=[out(BF16), out(F32)] + [out(BF16)] * 5,
        compiler_params=_params(("parallel", "parallel")),
        name="rwkv_prep",
    )(x, x, *consts)


def _blockdiag(z, lo):
    z = z.astype(BF16)
    zero = jnp.zeros_like(z)
    return jnp.concatenate([jnp.where(lo, z, zero), jnp.where(lo, zero, z)], axis=0)


def _rwkv_scan_kernel(r_ref, lw_ref, k_ref, v_ref, kk_ref, b_ref, y_out, s_scr, *, nbatch, npairs):
    c = pl.program_id(2)

    @pl.when(c == 0)
    def _():
        s_scr[...] = jnp.zeros_like(s_scr)

    C = CHUNK
    t_i = lax.broadcasted_iota(jnp.int32, (C, C), 0)
    j_i = lax.broadcasted_iota(jnp.int32, (C, C), 1)
    ltri = (j_i <= t_i).astype(BF16)
    row = lax.broadcasted_iota(jnp.int32, (C, LANES), 0)
    lane = lax.broadcasted_iota(jnp.int32, (C, LANES), 1)
    lo = lane < HEAD_DIM
    col = jnp.bitwise_and(lane, HEAD_DIM - 1)
    strict = col < row
    incl = col <= row
    rr = lax.broadcasted_iota(jnp.int32, (LANES, LANES), 0)
    cc = lax.broadcasted_iota(jnp.int32, (LANES, LANES), 1)
    same_head = (rr < HEAD_DIM) == (cc < HEAD_DIM)
    bd = functools.partial(_blockdiag, lo=lo)
    cat0 = lambda *xs: jnp.concatenate([x.astype(BF16) for x in xs], axis=0)
    cat1 = lambda *xs: jnp.concatenate([x.astype(BF16) for x in xs], axis=1)

    units = [(bi, slice(p * LANES, (p + 1) * LANES)) for bi in range(nbatch) for p in range(npairs)]
    P = range(len(units))
    r = [r_ref[bi, :, sl].astype(F32) for bi, sl in units]
    lw = [lw_ref[bi, :, sl] for bi, sl in units]
    k = [k_ref[bi, :, sl].astype(F32) for bi, sl in units]
    v = [v_ref[bi, :, sl].astype(F32) for bi, sl in units]
    kk = [kk_ref[bi, :, sl].astype(F32) for bi, sl in units]
    b = [b_ref[bi, :, sl].astype(F32) for bi, sl in units]

    cw2 = [_dot(ltri, cat1(*_split(lw[p]))) for p in P]
    cw = [cw2[p][:, :LANES] + cw2[p][:, LANES:] for p in P]
    cwl = [cw[p][C - 1:C, :] for p in P]
    at = [-kk[p] * jnp.exp(cw[p] - lw[p]) for p in P]
    dinv = [jnp.exp(-cw[p]) for p in P]
    rt = [r[p] * jnp.exp(cw[p]) for p in P]
    dend = [jnp.exp(cwl[p] - cw[p]) for p in P]

    x = [_dot_nt(cat0(at[p], rt[p]), cat0(bd(b[p] * dinv[p]), bd(k[p] * dinv[p]))) for p in P]
    aab = [jnp.where(strict, x[p][:C, :LANES], 0.0) for p in P]
    arb = [jnp.where(incl, x[p][C:, :LANES], 0.0) for p in P]
    aak = [jnp.where(strict, x[p][:C, LANES:], 0.0) for p in P]
    ark = [jnp.where(incl, x[p][C:, LANES:], 0.0) for p in P]

    bdv = [bd(v[p]) for p in P]
    av = [_dot(aak[p].astype(BF16), bdv[p]) for p in P]
    eye = jnp.where(col == row, 1.0, 0.0)
    tinv = [eye + aab[p] for p in P]
    n = [_dot(aab[p].astype(BF16), bd(aab[p])) for p in P]
    for it in range(5):
        last = it == 4
        res = [_dot(n[p].astype(BF16),
                    jnp.concatenate([bd(tinv[p])] + ([] if last else [bd(n[p])]), axis=1)) for p in P]
        tinv = [tinv[p] + res[p][:, :LANES] for p in P]
        if not last:
            n = [res[p][:, LANES:] for p in P]
    z = [_dot(tinv[p].astype(BF16), jnp.concatenate([bd(at[p]), bd(av[p])], axis=1)) for p in P]
    z1 = [z[p][:, :LANES] for p in P]
    z2 = [z[p][:, LANES:] for p in P]

    s = [s_scr[p] for p in P]
    ws = [_dot_nt(cat0(z1[p], rt[p]), s[p].astype(BF16)) for p in P]
    u = [ws[p][:C] + z2[p] for p in P]
    y = [ws[p][C:] + _dot(cat1(arb[p], ark[p]), cat0(bd(u[p]), bdv[p])) for p in P]
    upd = [_dot_tn(cat0(u[p], v[p]), cat0(b[p] * dend[p], k[p] * dend[p])) for p in P]
    for p in P:
        s_scr[p] = s[p] * jnp.exp(cwl[p]) + jnp.where(same_head, upd[p], 0.0)

    for p in P:
        bi, sl = units[p]
        y_out[bi, :, sl] = y[p]


def _rwkv_scan(r, lw, k, v, kk, b, *, nbatch, npairs):
    B, T, D = r.shape
    W = npairs * LANES
    tile = pl.BlockSpec((nbatch, CHUNK, W), lambda bi, p, c: (bi, c, p))
    return pl.pallas_call(
        functools.partial(_rwkv_scan_kernel, nbatch=nbatch, npairs=npairs),
        grid=(B // nbatch, D // W, T // CHUNK),
        in_specs=[tile] * 6,
        out_specs=tile,
        out_shape=jax.ShapeDtypeStruct((B, T, D), F32),
        scratch_shapes=[pltpu.VMEM((nbatch * npairs, LANES, LANES), F32)],
        compiler_params=_params(("parallel", "parallel", "arbitrary")),
        name="rwkv_scan",
    )(r, lw, k, v, kk, b)


def _mlp_tail(x, g_ref, win_ref, wout_ref, o_ref, tf):
    xn = _rms(x, g_ref[...]).astype(BF16)
    acc = x
    for f in range(win_ref.shape[-1] // tf):
        hid = jnp.maximum(_dot(xn, win_ref[:, f * tf:(f + 1) * tf]), 0.0)
        acc = acc + _dot((hid * hid).astype(BF16), wout_ref[f * tf:(f + 1) * tf, :])
    o_ref[...] = acc


def _proj_mlp_kernel(res_ref, a_ref, wo_ref, g_ref, win_ref, wout_ref, o_ref, *, tf):
    x = res_ref[...] + _dot(a_ref[...], wo_ref[...])
    _mlp_tail(x, g_ref, win_ref, wout_ref, o_ref, tf)


def _rwkv_out_mlp_kernel(res_ref, y_ref, r_ref, k_ref, v_ref, gate_ref, rk_ref, lnw_ref, lnb_ref,
                         wo_ref, g_ref, win_ref, wout_ref, o_ref, a_scr, *, tf):
    i = pl.program_id(0)

    def output_stage():
        y = y_ref[...]
        d = y - _head_sum(y) * (1.0 / HEAD_DIM)
        var = _head_sum(d * d) * (1.0 / HEAD_DIM)
        yn = d * lax.rsqrt(var + GN_EPS)
        r = r_ref[...].astype(F32)
        k = k_ref[...].astype(F32)
        bonus = _head_sum(r * k * rk_ref[...]) * v_ref[...].astype(F32)
        return ((yn * lnw_ref[...] + lnb_ref[...] + bonus) * gate_ref[...].astype(F32)).astype(BF16)

    @pl.when(i == 0)
    def _():
        a_scr[...] = output_stage()

    @pl.when(i > 0)
    def _():
        x = res_ref[...] + _dot(a_scr[...], wo_ref[...])
        _mlp_tail(x, g_ref, win_ref, wout_ref, o_ref, tf)
        a_scr[...] = output_stage()


def _proj_mlp(res, mix, w_o, g, w_in, w_out, *, layer, tm, tf, rwkv=None):
    M, D = res.shape
    tile = pl.BlockSpec((tm, D), lambda i: (i, 0))
    once = lambda shape: pl.BlockSpec(shape, lambda i: (0,) * len(shape), pipeline_mode=pl.Buffered(1))
    of_layer = lambda w: pl.BlockSpec((None,) + w.shape[1:], lambda i: (layer, 0, 0),
                                      pipeline_mode=pl.Buffered(1))
    vec = once((1, D))
    weights = [once(w_o.shape), vec, of_layer(w_in), of_layer(w_out)]
    wargs = (w_o, g.reshape(1, D), w_in, w_out)
    n = M // tm
    if rwkv is None:
        body, grid, specs, args, out_spec, scratch = _proj_mlp_kernel, n, [tile, tile], (res, mix), tile, []
    else:
        r, k, v, gate, rk, lnw, lnb = rwkv
        prev = pl.BlockSpec((tm, D), lambda i: (jnp.maximum(i - 1, 0), 0))
        cur = pl.BlockSpec((tm, D), lambda i: (jnp.minimum(i, n - 1), 0))
        body, grid, specs, out_spec = _rwkv_out_mlp_kernel, n + 1, [prev] + [cur] * 5 + [vec] * 3, prev
        args = (res, mix, r, k, v, gate, rk.reshape(1, D), lnw.reshape(1, D), lnb.reshape(1, D))
        scratch = [pltpu.VMEM((tm, D), BF16)]
    return pl.pallas_call(
        functools.partial(body, tf=tf),
        grid=(grid,),
        in_specs=specs + weights,
        out_specs=out_spec,
        out_shape=jax.ShapeDtypeStruct((M, D), F32),
        scratch_shapes=scratch,
        compiler_params=_params(("arbitrary",)),
        name="proj_mlp",
    )(*args, *wargs)


def _store_q_tiles(qt, c_rows, qg_ref, q_out, *, scale, tq):
    tm = qt.shape[1]
    row = lax.broadcasted_iota(jnp.int32, (HEAD_DIM, tm), 0)
    for h in range(qt.shape[0] // HEAD_DIM):
        hs = slice(h * HEAD_DIM, (h + 1) * HEAD_DIM)
        qh = qt[hs, :]
        ms = jnp.mean(qh * qh, axis=0, keepdims=True)
        qn = qh * lax.rsqrt(ms + NORM_EPS) * (qg_ref[hs, :] * scale)
        hi, mid, lo = _split3(c_rows[h:h + 1, :] * LOG2E)
        aug = jnp.where(row == 0, hi, jnp.where(row == 1, mid, jnp.where(
            row == 2, lo, jnp.where(row < 6, 1.0, 0.0))))
        tile = jnp.concatenate([qn, aug], axis=0).astype(q_out.dtype)
        for sb in range(tm // tq):
            q_out[0, h, sb] = tile[:, sb * tq:(sb + 1) * tq]


def _q_proj_kernel(x_ref, g_ref, wt_ref, qg_ref, c_ref, q_out, *, scale, tq):
    hn = _rms(x_ref[0], g_ref[...])
    qt = _dot_nt(wt_ref[...], hn.astype(BF16))
    _store_q_tiles(qt, c_ref[0], qg_ref, q_out, scale=scale, tq=tq)


def _q_proj(x, g, wt, qg, c_row, *, tm, tq, scale):
    B, T, D = x.shape
    H = D // HEAD_DIM
    return pl.pallas_call(
        functools.partial(_q_proj_kernel, scale=scale, tq=tq),
        grid=(B, T // tm),
        in_specs=[pl.BlockSpec((1, tm, D), lambda b, i: (b, i, 0)), _const_spec((1, D)),
                  _const_spec(wt.shape), _const_spec((D, 1)),
                  pl.BlockSpec((1, H, tm), lambda b, i: (b, 0, i))],
        out_specs=pl.BlockSpec((1, H, tm // tq, LANES, tq), lambda b, i: (b, 0, i, 0, 0)),
        out_shape=jax.ShapeDtypeStruct((B, H, T // tq, LANES, tq), BF16),
        compiler_params=_params(("parallel", "parallel")),
        name="q_proj",
    )(x, g.reshape(1, D), wt, qg.reshape(D, 1), c_row)


def _shared_kv_kernel(x_ref, g_ref, wk_ref, wvt_ref, wf_ref, fb_ref, kg_ref, sel_ref,
                      gq_ref, wqt_ref, qg_ref,
                      k_out, vt_out, c_out, q_out, carry_scr, *, tq, scale):
    i = pl.program_id(1)

    @pl.when(i == 0)
    def _():
        carry_scr[...] = jnp.zeros_like(carry_scr)

    x = x_ref[0]
    xn = x * lax.rsqrt(jnp.mean(x * x, axis=-1, keepdims=True) + NORM_EPS)
    hn = xn * g_ref[...]
    hb = hn.astype(BF16)
    tm, D = hn.shape
    H = D // HEAD_DIM
    k = _head_rms(_dot(hb, wk_ref[...]), kg_ref[...])
    vt = _dot_nt(wvt_ref[...], hb)

    f = _dot(hb, wf_ref[...]) + fb_ref[...]
    logf = jnp.minimum(f, 0.0) - jnp.log(1.0 + jnp.exp(-jnp.abs(f)))
    t_i = lax.broadcasted_iota(jnp.int32, (tm, tm), 0)
    j_i = lax.broadcasted_iota(jnp.int32, (tm, tm), 1)
    ltri = (j_i <= t_i).astype(BF16)
    c3 = _dot(ltri, jnp.concatenate([t.astype(BF16) for t in _split3(logf)], axis=1))
    c = (c3[:, :LANES] + c3[:, LANES:2 * LANES]) + c3[:, 2 * LANES:] + carry_scr[0:1, :]
    carry_scr[...] = jnp.broadcast_to(c[tm - 1:tm, :], carry_scr.shape)
    c_rows = c.T[:H, :]
    c_out[0] = c_rows
    qt = _dot_nt(wqt_ref[...], (xn * gq_ref[...]).astype(BF16))
    _store_q_tiles(qt, c_rows, qg_ref, q_out, scale=scale, tq=tq)

    lane = lax.broadcasted_iota(jnp.int32, (tm, LANES), 1)
    hi, mid, lo = (jnp.where(lane < H, t, 0.0) for t in _split3(c * (-LOG2E)))
    packed = (hi + pltpu.roll(mid, H, 1)) + (pltpu.roll(lo, 2 * H, 1) + jnp.where(lane == 3 * H, 1.0, 0.0))
    aug = _dot(packed.astype(BF16), sel_ref[...])
    vrow = lax.broadcasted_iota(jnp.int32, (V_ROWS - HEAD_DIM, tq), 0)
    ones_row = jnp.where(vrow == 0, 1.0, 0.0)
    for h in range(H):
        base = k[:, (h // 2) * LANES:(h // 2 + 1) * LANES]
        if h % 2:
            base = pltpu.roll(base, HEAD_DIM, 1)
        tile = jnp.where(lane < HEAD_DIM, base, aug[:, h * LANES:(h + 1) * LANES])
        k_out[0, h] = tile.astype(k_out.dtype)
        for sb in range(tm // tq):
            vt_out[0, h, sb] = jnp.concatenate(
                [vt[h * HEAD_DIM:(h + 1) * HEAD_DIM, sb * tq:(sb + 1) * tq], ones_row],
                axis=0).astype(vt_out.dtype)


def _bias_selector(H):
    assert 3 * H + 1 <= LANES
    sel = np.zeros((LANES, H * LANES), np.float32)
    for h in range(H):
        sel[3 * H, h * LANES + HEAD_DIM:h * LANES + HEAD_DIM + 3] = 1.0
        for t in range(3):
            sel[t * H + h, h * LANES + HEAD_DIM + 3 + t] = 1.0
    return jnp.asarray(sel, BF16)


def _shared_kv(x, g, wk, wvt, wf, fb, kg, gq, wqt, qg, *, tm, tq, scale):
    B, T, D = x.shape
    H = D // HEAD_DIM
    sel = _bias_selector(H)
    slabs = lambda rows: pl.BlockSpec((1, H, tm // tq, rows, tq), lambda b, i: (b, 0, i, 0, 0))
    return pl.pallas_call(
        functools.partial(_shared_kv_kernel, tq=tq, scale=scale),
        grid=(B, T // tm),
        in_specs=[pl.BlockSpec((1, tm, D), lambda b, i: (b, i, 0)), _const_spec((1, D)),
                  _const_spec(wk.shape), _const_spec(wvt.shape), _const_spec(wf.shape),
                  _const_spec((1, LANES)), _const_spec((1, D)), _const_spec(sel.shape),
                  _const_spec((1, D)), _const_spec(wqt.shape), _const_spec((D, 1))],
        out_specs=[pl.BlockSpec((1, H, tm, LANES), lambda b, i: (b, 0, i, 0)), slabs(V_ROWS),
                   pl.BlockSpec((1, H, tm), lambda b, i: (b, 0, i)), slabs(LANES)],
        out_shape=[jax.ShapeDtypeStruct((B, H, T, LANES), BF16),
                   jax.ShapeDtypeStruct((B, H, T // tq, V_ROWS, tq), BF16),
                   jax.ShapeDtypeStruct((B, H, T), F32),
                   jax.ShapeDtypeStruct((B, H, T // tq, LANES, tq), BF16)],
        scratch_shapes=[pltpu.VMEM((8, LANES), F32)],
        compiler_params=_params(("parallel", "arbitrary")),
        name="shared_kv",
    )(x, g.reshape(1, D), wk, wvt, wf, fb, kg.reshape(1, D), sel,
      gq.reshape(1, D), wqt, qg.reshape(D, 1))


def _fox_attn_kernel(q_ref, k_ref, vt_ref, o_ref, acc_scr, *, tq, nh):
    i = pl.program_id(2)
    qt = [q_ref[0, h, 0] for h in range(nh)]
    acc_scr[...] = jnp.zeros_like(acc_scr)
    key_i = lax.broadcasted_iota(jnp.int32, (tq, tq), 0)
    qry_i = lax.broadcasted_iota(jnp.int32, (tq, tq), 1)
    causal = key_i <= qry_i

    def step(j, m, masked):
        off = pl.multiple_of(j * tq, tq)
        s = [_dot(k_ref[0, h, pl.ds(off, tq), :], qt[h]) for h in range(nh)]
        if masked:
            s = [jnp.where(causal, s[h], NEG_BIG) for h in range(nh)]
        m_new = [jnp.maximum(m[h], jnp.max(s[h], axis=0, keepdims=True)) for h in range(nh)]
        p = [jnp.exp2(s[h] - m_new[h]).astype(BF16) for h in range(nh)]
        alpha = [jnp.exp2(m[h] - m_new[h]) for h in range(nh)]
        pv = [_dot(vt_ref[0, h, j], p[h]) for h in range(nh)]
        for h in range(nh):
            acc_scr[h] = alpha[h] * acc_scr[h] + pv[h]
        return tuple(m_new)

    m0 = tuple(jnp.full((1, tq), NEG_BIG, F32) for _ in range(nh))
    m = lax.fori_loop(0, i, functools.partial(step, masked=False), m0)
    step(i, m, True)
    ot = [acc_scr[h, :HEAD_DIM, :] * (1.0 / acc_scr[h, HEAD_DIM:HEAD_DIM + 1, :]) for h in range(nh)]
    o_ref[0] = jnp.concatenate(ot, axis=0).T.astype(o_ref.dtype)


def _fox_attn(qt, ka, vt, *, nh):
    B, H, nb, _, tq = qt.shape
    T = nb * tq
    return pl.pallas_call(
        functools.partial(_fox_attn_kernel, tq=tq, nh=nh),
        grid=(B, H // nh, nb),
        in_specs=[pl.BlockSpec((1, nh, 1, LANES, tq), lambda b, p, i: (b, p, i, 0, 0)),
                  pl.BlockSpec((1, nh, T, LANES), lambda b, p, i: (b, p, 0, 0)),
                  pl.BlockSpec((1, nh, nb, V_ROWS, tq), lambda b, p, i: (b, p, 0, 0, 0))],
        out_specs=pl.BlockSpec((1, tq, nh * HEAD_DIM), lambda b, p, i: (b, i, p)),
        out_shape=jax.ShapeDtypeStruct((B, T, H * HEAD_DIM), BF16),
        scratch_shapes=[pltpu.VMEM((nh, V_ROWS, tq), F32)],
        compiler_params=_params(("parallel", "parallel", "arbitrary")),
        name="fox_attn",
    )(qt, ka, vt)


def kernel(x, rwkv_norm_g, rwkv_mu, rwkv_w_rkv, rwkv_w0, rwkv_w1, rwkv_w2, rwkv_a0, rwkv_a1, rwkv_a2, rwkv_g1, rwkv_g2, rwkv_k_k, rwkv_k_a, rwkv_r_k, rwkv_lnx_w, rwkv_lnx_b, rwkv_w_o, kv_norm_g, kv_w, kv_f_bias, k_norm_g, attn_norm_g, attn_w_q, q_norm_g, attn_w_o, mlp_norm_g, mlp_w_in, mlp_w_out):
    B, T, D = x.shape
    M = B * T
    n_a = rwkv_norm_g.shape[0]
    depth = mlp_norm_g.shape[0]
    bf = lambda w: w.astype(BF16)
    tm = min(512, T)
    tq = min(256, T)
    tf = min(1024, mlp_w_in.shape[-1])
    npairs = D // LANES

    w_in_all, w_out_all = bf(mlp_w_in), bf(mlp_w_out)
    q_scale = HEAD_DIM ** -0.5 * LOG2E
    k_sh = v_sh = c_sh = q_first = None
    for layer in range(depth):
        if layer < n_a:
            i = layer
            r, lw, k, v, kk, b, g = _rwkv_prep(
                x, rwkv_norm_g[i], rwkv_mu[i], bf(rwkv_w_rkv[i, 0]), bf(rwkv_w_rkv[i, 1]),
                bf(rwkv_w_rkv[i, 2]), rwkv_w0[i], bf(rwkv_w1[i]), bf(rwkv_w2[i]), rwkv_a0[i],
                bf(rwkv_a1[i]), bf(rwkv_a2[i]), bf(rwkv_g1[i]), bf(rwkv_g2[i]),
                rwkv_k_k[i], rwkv_k_a[i], tm=tm)
            mix = _rwkv_scan(r, lw, k, v, kk, b, nbatch=2 if B % 2 == 0 else 1, npairs=npairs)
            flat = lambda t: t.reshape(M, D)
            w_o, extra = rwkv_w_o[i], (flat(r), flat(k), flat(v), flat(g), rwkv_r_k[i].reshape(D),
                                       rwkv_lnx_w[i], rwkv_lnx_b[i])
        else:
            j = layer - n_a
            qt = q_first if j == 0 else _q_proj(x, attn_norm_g[j], bf(attn_w_q[j].T), q_norm_g[j],
                                                 c_sh, tm=tm, tq=tq, scale=q_scale)
            mix = _fox_attn(qt, k_sh, v_sh, nh=min(16, D // HEAD_DIM))
            w_o, extra = attn_w_o[j], None
        x = _proj_mlp(x.reshape(M, D), mix.reshape(M, D), bf(w_o), mlp_norm_g[layer],
                      w_in_all, w_out_all, layer=layer, tm=tm, tf=tf, rwkv=extra).reshape(B, T, D)
        if layer == n_a - 1:
            wf = jnp.pad(kv_w[:, 2 * D:], ((0, 0), (0, LANES - (kv_w.shape[1] - 2 * D))))
            fb = jnp.pad(kv_f_bias, (0, LANES - kv_f_bias.shape[0])).reshape(1, LANES)
            k_sh, v_sh, c_sh, q_first = _shared_kv(
                x, kv_norm_g, bf(kv_w[:, :D]), bf(kv_w[:, D:2 * D].T), bf(wf), fb, k_norm_g,
                attn_norm_g[0], bf(attn_w_q[0].T), q_norm_g[0], tm=tm, tq=tq, scale=q_scale)
    return x
```

```python
import functools

import jax
import jax.numpy as jnp
import numpy as np
from jax import lax
from jax.experimental import pallas as pl
from jax.experimental.pallas import tpu as pltpu

HEAD_DIM = 64
LANES = 128
NORM_EPS = 1e-6
GN_EPS = 64e-5
CHUNK = 64
NEG_BIG = -1e30
LOG2E = 1.4426950408889634
EXP_M_HALF = 0.6065306597126334
V_ROWS = 80
VMEM_LIMIT = 56 * 1024 * 1024

BF16 = jnp.bfloat16
F32 = jnp.float32

_NT = (((1,), (1,)), ((), ()))
_TN = (((0,), (0,)), ((), ()))


def _dot(a, b):
    return jnp.dot(a, b, preferred_element_type=F32)


def _dot_nt(a, b):
    return lax.dot_general(a, b, _NT, preferred_element_type=F32)


def _dot_tn(a, b):
    return lax.dot_general(a, b, _TN, preferred_element_type=F32)


def _split(a):
    hi = a.astype(BF16)
    return hi, (a - hi.astype(F32)).astype(BF16)


def _split3(a):
    hi = a.astype(BF16).astype(F32)
    r1 = a - hi
    mid = r1.astype(BF16).astype(F32)
    return hi, mid, r1 - mid


def _rms(x, g):
    return x * lax.rsqrt(jnp.mean(x * x, axis=-1, keepdims=True) + NORM_EPS) * g


def _head_sum(x):
    outs = []
    for c in range(x.shape[1] // LANES):
        xc = x[:, c * LANES:(c + 1) * LANES]
        lo = lax.broadcasted_iota(jnp.int32, xc.shape, 1) < HEAD_DIM
        s0 = jnp.sum(jnp.where(lo, xc, 0.0), axis=1, keepdims=True)
        s1 = jnp.sum(jnp.where(lo, 0.0, xc), axis=1, keepdims=True)
        outs.append(jnp.where(lo, s0, s1))
    return outs[0] if len(outs) == 1 else jnp.concatenate(outs, axis=1)


def _head_rms(t, g):
    ms = _head_sum(t * t) * (1.0 / HEAD_DIM)
    return t * lax.rsqrt(ms + NORM_EPS) * g


def _sigmoid(z):
    return 1.0 / (1.0 + jnp.exp(-z))


def _const_spec(shape):
    nd = len(shape)
    return pl.BlockSpec(shape, lambda *_: (0,) * nd)


def _params(sem):
    return pltpu.CompilerParams(dimension_semantics=sem, vmem_limit_bytes=VMEM_LIMIT)


def _rwkv_prep_kernel(x_ref, xp_ref, ng_ref, mu_ref, wr_ref, wk_ref, wv_ref,
                      w0_ref, w1_ref, w2_ref, a0_ref, a1_ref, a2_ref, g1_ref, g2_ref,
                      kkw_ref, kaw_ref,
                      r_out, lw_out, k_out, v_out, kk_out, b_out, g_out):
    i = pl.program_id(1)
    ng = ng_ref[...]
    h = _rms(x_ref[0], ng)
    hp = _rms(xp_ref[0][7:8, :], ng)
    hp = jnp.where(i > 0, hp, 0.0)
    rolled = pltpu.roll(h, 1, 0)
    first = jnp.where(lax.broadcasted_iota(jnp.int32, (8, h.shape[1]), 0) == 0, hp, rolled[:8])
    hs = jnp.concatenate([first, rolled[8:]], axis=0)
    hb = h.astype(BF16)
    xxb = (hs - h).astype(BF16)
    mub = mu_ref[...].astype(BF16)

    def mix(j):
        return hb + xxb * mub[j:j + 1, :]

    tw = _dot(mix(1), w1_ref[...])
    ta = _dot(mix(4), a1_ref[...])
    tg = _dot(mix(5), g1_ref[...])
    k = _dot(mix(2), wk_ref[...])
    wl = _dot(jnp.tanh(tw).astype(BF16), w2_ref[...])
    al = _dot(ta.astype(BF16), a2_ref[...])
    g_out[0] = _dot(_sigmoid(tg).astype(BF16), g2_ref[...]).astype(g_out.dtype)

    lw_out[0] = -EXP_M_HALF * _sigmoid(w0_ref[...] + wl)
    a = _sigmoid(a0_ref[...] + al)
    kk = k * kkw_ref[...]
    kk = kk * lax.rsqrt(jnp.maximum(_head_sum(kk * kk), 1e-24))
    k_out[0] = (k * (1.0 + (a - 1.0) * kaw_ref[...])).astype(k_out.dtype)
    kk_out[0] = kk.astype(kk_out.dtype)
    b_out[0] = (kk * a).astype(b_out.dtype)

    r_out[0] = _dot(mix(0), wr_ref[...]).astype(r_out.dtype)
    v_out[0] = _dot(mix(3), wv_ref[...]).astype(v_out.dtype)


def _rwkv_prep(x, ng, mu, wr, wk, wv, w0, w1, w2, a0, a1, a2, g1, g2, kkw, kaw, *, tm):
    B, T, D = x.shape
    row = lambda a: a.reshape(1, D)
    consts = [row(ng), mu, wr, wk, wv, row(w0), w1, w2, row(a0), a1, a2, g1, g2, row(kkw), row(kaw)]
    tile = pl.BlockSpec((1, tm, D), lambda b, i: (b, i, 0))
    prev = pl.BlockSpec((1, 8, D), lambda b, i: (b, jnp.maximum(i * (tm // 8) - 1, 0), 0))
    out = lambda dt: jax.ShapeDtypeStruct((B, T, D), dt)
    return pl.pallas_call(
        _rwkv_prep_kernel,
        grid=(B, T // tm),
        in_specs=[tile, prev] + [_const_spec(c.shape) for c in consts],
        out_specs=[tile] * 7,
        out_shape=[out(BF16), out(F32)] + [out(BF16)] * 5,
        compiler_params=_params(("parallel", "parallel")),
        name="rwkv_prep",
    )(x, x, *consts)


def _blockdiag(z, lo):
    z = z.astype(BF16)
    zero = jnp.zeros_like(z)
    return jnp.concatenate([jnp.where(lo, z, zero), jnp.where(lo, zero, z)], axis=0)


def _rwkv_scan_kernel(r_ref, lw_ref, k_ref, v_ref, kk_ref, b_ref, y_out, s_scr, *,
                      nchunk, nbatch, npairs):
    c = pl.program_id(2)

    @pl.when(c == 0)
    def _():
        s_scr[...] = jnp.zeros_like(s_scr)

    C = CHUNK
    t_i = lax.broadcasted_iota(jnp.int32, (C, C), 0)
    j_i = lax.broadcasted_iota(jnp.int32, (C, C), 1)
    ltri = (j_i <= t_i).astype(BF16)
    row = lax.broadcasted_iota(jnp.int32, (C, LANES), 0)
    lane = lax.broadcasted_iota(jnp.int32, (C, LANES), 1)
    lo = lane < HEAD_DIM
    col = jnp.bitwise_and(lane, HEAD_DIM - 1)
    strict = col < row
    incl = col <= row
    rr = lax.broadcasted_iota(jnp.int32, (LANES, LANES), 0)
    cc = lax.broadcasted_iota(jnp.int32, (LANES, LANES), 1)
    same_head = (rr < HEAD_DIM) == (cc < HEAD_DIM)
    bd = functools.partial(_blockdiag, lo=lo)
    cat0 = lambda *xs: jnp.concatenate([x.astype(BF16) for x in xs], axis=0)
    cat1 = lambda *xs: jnp.concatenate([x.astype(BF16) for x in xs], axis=1)

    nstate = nbatch * npairs
    units = [(bi, slice(ci * C, (ci + 1) * C), slice(p * LANES, (p + 1) * LANES))
             for ci in range(nchunk) for bi in range(nbatch) for p in range(npairs)]
    P = range(len(units))
    r = [r_ref[u].astype(F32) for u in units]
    lw = [lw_ref[u] for u in units]
    k = [k_ref[u].astype(F32) for u in units]
    v = [v_ref[u].astype(F32) for u in units]
    kk = [kk_ref[u].astype(F32) for u in units]
    b = [b_ref[u].astype(F32) for u in units]

    cw2 = [_dot(ltri, cat1(*_split(lw[p]))) for p in P]
    cw = [cw2[p][:, :LANES] + cw2[p][:, LANES:] for p in P]
    cwl = [cw[p][C - 1:C, :] for p in P]
    at = [-kk[p] * jnp.exp(cw[p] - lw[p]) for p in P]
    dinv = [jnp.exp(-cw[p]) for p in P]
    rt = [r[p] * jnp.exp(cw[p]) for p in P]
    dend = [jnp.exp(cwl[p] - cw[p]) for p in P]

    x = [_dot_nt(cat0(at[p], rt[p]), cat0(bd(b[p] * dinv[p]), bd(k[p] * dinv[p]))) for p in P]
    aab = [jnp.where(strict, x[p][:C, :LANES], 0.0) for p in P]
    arb = [jnp.where(incl, x[p][C:, :LANES], 0.0) for p in P]
    aak = [jnp.where(strict, x[p][:C, LANES:], 0.0) for p in P]
    ark = [jnp.where(incl, x[p][C:, LANES:], 0.0) for p in P]

    bdv = [bd(v[p]) for p in P]
    av = [_dot(aak[p].astype(BF16), bdv[p]) for p in P]
    eye = jnp.where(col == row, 1.0, 0.0)
    tinv = [eye + aab[p] for p in P]
    n = [_dot(aab[p].astype(BF16), bd(aab[p])) for p in P]
    for it in range(5):
        last = it == 4
        res = [_dot(n[p].astype(BF16),
                    jnp.concatenate([bd(tinv[p])] + ([] if last else [bd(n[p])]), axis=1)) for p in P]
        tinv = [tinv[p] + res[p][:, :LANES] for p in P]
        if not last:
            n = [res[p][:, LANES:] for p in P]
    z = [_dot(tinv[p].astype(BF16), jnp.concatenate([bd(at[p]), bd(av[p])], axis=1)) for p in P]
    z1 = [z[p][:, :LANES] for p in P]
    z2 = [z[p][:, LANES:] for p in P]

    s = [s_scr[q] for q in range(nstate)]
    for ci in range(nchunk):
        Q = range(ci * nstate, (ci + 1) * nstate)
        ws = [_dot_nt(cat0(z1[p], rt[p]), s[p - Q[0]].astype(BF16)) for p in Q]
        u = [ws[p - Q[0]][:C] + z2[p] for p in Q]
        for p in Q:
            y_out[units[p]] = ws[p - Q[0]][C:] + _dot(cat1(arb[p], ark[p]),
                                                     cat0(bd(u[p - Q[0]]), bdv[p]))
        upd = [_dot_tn(cat0(u[p - Q[0]], v[p]), cat0(b[p] * dend[p], k[p] * dend[p])) for p in Q]
        s = [s[p - Q[0]] * jnp.exp(cwl[p]) + jnp.where(same_head, upd[p - Q[0]], 0.0) for p in Q]
    for q in range(nstate):
        s_scr[q] = s[q]


def _rwkv_scan(r, lw, k, v, kk, b, *, nchunk, nbatch, npairs):
    B, T, D = r.shape
    W = npairs * LANES
    tile = pl.BlockSpec((nbatch, nchunk * CHUNK, W), lambda bi, p, c: (bi, c, p))
    return pl.pallas_call(
        functools.partial(_rwkv_scan_kernel, nchunk=nchunk, nbatch=nbatch, npairs=npairs),
        grid=(B // nbatch, D // W, T // (nchunk * CHUNK)),
        in_specs=[tile] * 6,
        out_specs=tile,
        out_shape=jax.ShapeDtypeStruct((B, T, D), F32),
        scratch_shapes=[pltpu.VMEM((nbatch * npairs, LANES, LANES), F32)],
        compiler_params=_params(("parallel", "parallel", "arbitrary")),
        name="rwkv_scan",
    )(r, lw, k, v, kk, b)


def _mlp_tail(x, g_ref, win_ref, wout_ref, o_ref, tf):
    xn = _rms(x, g_ref[...]).astype(BF16)
    acc = x
    for f in range(win_ref.shape[-1] // tf):
        hid = jnp.maximum(_dot(xn, win_ref[:, f * tf:(f + 1) * tf]), 0.0)
        acc = acc + _dot((hid * hid).astype(BF16), wout_ref[f * tf:(f + 1) * tf, :])
    o_ref[...] = acc


def _proj_mlp_kernel(res_ref, a_ref, wo_ref, g_ref, win_ref, wout_ref, o_ref, *, tf):
    x = res_ref[...] + _dot(a_ref[...], wo_ref[...])
    _mlp_tail(x, g_ref, win_ref, wout_ref, o_ref, tf)


def _rwkv_out_mlp_kernel(res_ref, y_ref, r_ref, k_ref, v_ref, gate_ref, rk_ref, lnw_ref, lnb_ref,
                         wo_ref, g_ref, win_ref, wout_ref, o_ref, a_scr, *, tf):
    i = pl.program_id(0)

    def output_stage():
        y = y_ref[...]
        d = y - _head_sum(y) * (1.0 / HEAD_DIM)
        var = _head_sum(d * d) * (1.0 / HEAD_DIM)
        yn = d * lax.rsqrt(var + GN_EPS)
        r = r_ref[...].astype(F32)
        k = k_ref[...].astype(F32)
        bonus = _head_sum(r * k * rk_ref[...]) * v_ref[...].astype(F32)
        return ((yn * lnw_ref[...] + lnb_ref[...] + bonus) * gate_ref[...].astype(F32)).astype(BF16)

    @pl.when(i == 0)
    def _():
        a_scr[...] = output_stage()

    @pl.when(i > 0)
    def _():
        x = res_ref[...] + _dot(a_scr[...], wo_ref[...])
        _mlp_tail(x, g_ref, win_ref, wout_ref, o_ref, tf)
        a_scr[...] = output_stage()


def _proj_mlp(res, mix, w_o, g, w_in, w_out, *, layer, tm, tf, rwkv=None):
    M, D = res.shape
    tile = pl.BlockSpec((tm, D), lambda i: (i, 0))
    once = lambda shape: pl.BlockSpec(shape, lambda i: (0,) * len(shape), pipeline_mode=pl.Buffered(1))
    of_layer = lambda w: pl.BlockSpec((None,) + w.shape[1:], lambda i: (layer, 0, 0),
                                      pipeline_mode=pl.Buffered(1))
    vec = once((1, D))
    weights = [once(w_o.shape), vec, of_layer(w_in), of_layer(w_out)]
    wargs = (w_o, g.reshape(1, D), w_in, w_out)
    n = M // tm
    if rwkv is None:
        body, grid, specs, args, out_spec, scratch = _proj_mlp_kernel, n, [tile, tile], (res, mix), tile, []
    else:
        r, k, v, gate, rk, lnw, lnb = rwkv
        prev = pl.BlockSpec((tm, D), lambda i: (jnp.maximum(i - 1, 0), 0))
        cur = pl.BlockSpec((tm, D), lambda i: (jnp.minimum(i, n - 1), 0))
        body, grid, specs, out_spec = _rwkv_out_mlp_kernel, n + 1, [prev] + [cur] * 5 + [vec] * 3, prev
        args = (res, mix, r, k, v, gate, rk.reshape(1, D), lnw.reshape(1, D), lnb.reshape(1, D))
        scratch = [pltpu.VMEM((tm, D), BF16)]
    return pl.pallas_call(
        functools.partial(body, tf=tf),
        grid=(grid,),
        in_specs=specs + weights,
        out_specs=out_spec,
        out_shape=jax.ShapeDtypeStruct((M, D), F32),
        scratch_shapes=scratch,
        compiler_params=_params(("arbitrary",)),
        name="proj_mlp",
    )(*args, *wargs)


def _store_q_tiles(qt, c_rows, qg_ref, q_out, *, scale, tq):
    tm = qt.shape[1]
    row = lax.broadcasted_iota(jnp.int32, (HEAD_DIM, tm), 0)
    for h in range(qt.shape[0] // HEAD_DIM):
        hs = slice(h * HEAD_DIM, (h + 1) * HEAD_DIM)
        qh = qt[hs, :]
        ms = jnp.mean(qh * qh, axis=0, keepdims=True)
        qn = qh * lax.rsqrt(ms + NORM_EPS) * (qg_ref[hs, :] * scale)
        hi, mid, lo = _split3(c_rows[h:h + 1, :] * LOG2E)
        aug = jnp.where(row == 0, hi, jnp.where(row == 1, mid, jnp.where(
            row == 2, lo, jnp.where(row < 6, 1.0, 0.0))))
        tile = jnp.concatenate([qn, aug], axis=0).astype(q_out.dtype)
        for sb in range(tm // tq):
            q_out[0, h, sb] = tile[:, sb * tq:(sb + 1) * tq]


def _q_proj_kernel(x_ref, g_ref, wt_ref, qg_ref, c_ref, q_out, *, scale, tq):
    hn = _rms(x_ref[0], g_ref[...])
    qt = _dot_nt(wt_ref[...], hn.astype(BF16))
    _store_q_tiles(qt, c_ref[0], qg_ref, q_out, scale=scale, tq=tq)


def _q_proj(x, g, wt, qg, c_row, *, tm, tq, scale):
    B, T, D = x.shape
    H = D // HEAD_DIM
    return pl.pallas_call(
        functools.partial(_q_proj_kernel, scale=scale, tq=tq),
        grid=(B, T // tm),
        in_specs=[pl.BlockSpec((1, tm, D), lambda b, i: (b, i, 0)), _const_spec((1, D)),
                  _const_spec(wt.shape), _const_spec((D, 1)),
                  pl.BlockSpec((1, H, tm), lambda b, i: (b, 0, i))],
        out_specs=pl.BlockSpec((1, H, tm // tq, LANES, tq), lambda b, i: (b, 0, i, 0, 0)),
        out_shape=jax.ShapeDtypeStruct((B, H, T // tq, LANES, tq), BF16),
        compiler_params=_params(("parallel", "parallel")),
        name="q_proj",
    )(x, g.reshape(1, D), wt, qg.reshape(D, 1), c_row)


def _shared_kv_kernel(x_ref, g_ref, wk_ref, wvt_ref, wf_ref, fb_ref, kg_ref, sel_ref,
                      gq_ref, wqt_ref, qg_ref,
                      k_out, vt_out, c_out, q_out, carry_scr, *, tq, scale):
    i = pl.program_id(1)

    @pl.when(i == 0)
    def _():
        carry_scr[...] = jnp.zeros_like(carry_scr)

    x = x_ref[0]
    xn = x * lax.rsqrt(jnp.mean(x * x, axis=-1, keepdims=True) + NORM_EPS)
    hn = xn * g_ref[...]
    hb = hn.astype(BF16)
    tm, D = hn.shape
    H = D // HEAD_DIM
    k = _head_rms(_dot(hb, wk_ref[...]), kg_ref[...])
    vt = _dot_nt(wvt_ref[...], hb)

    f = _dot(hb, wf_ref[...]) + fb_ref[...]
    logf = jnp.minimum(f, 0.0) - jnp.log(1.0 + jnp.exp(-jnp.abs(f)))
    t_i = lax.broadcasted_iota(jnp.int32, (tm, tm), 0)
    j_i = lax.broadcasted_iota(jnp.int32, (tm, tm), 1)
    ltri = (j_i <= t_i).astype(BF16)
    c3 = _dot(ltri, jnp.concatenate([t.astype(BF16) for t in _split3(logf)], axis=1))
    c = (c3[:, :LANES] + c3[:, LANES:2 * LANES]) + c3[:, 2 * LANES:] + carry_scr[0:1, :]
    carry_scr[...] = jnp.broadcast_to(c[tm - 1:tm, :], carry_scr.shape)
    c_rows = c.T[:H, :]
    c_out[0] = c_rows
    qt = _dot_nt(wqt_ref[...], (xn * gq_ref[...]).astype(BF16))
    _store_q_tiles(qt, c_rows, qg_ref, q_out, scale=scale, tq=tq)

    lane = lax.broadcasted_iota(jnp.int32, (tm, LANES), 1)
    hi, mid, lo = (jnp.where(lane < H, t, 0.0) for t in _split3(c * (-LOG2E)))
    packed = (hi + pltpu.roll(mid, H, 1)) + (pltpu.roll(lo, 2 * H, 1) + jnp.where(lane == 3 * H, 1.0, 0.0))
    aug = _dot(packed.astype(BF16), sel_ref[...])
    vrow = lax.broadcasted_iota(jnp.int32, (V_ROWS - HEAD_DIM, tq), 0)
    ones_row = jnp.where(vrow == 0, 1.0, 0.0)
    for h in range(H):
        base = k[:, (h // 2) * LANES:(h // 2 + 1) * LANES]
        if h % 2:
            base = pltpu.roll(base, HEAD_DIM, 1)
        tile = jnp.where(lane < HEAD_DIM, base, aug[:, h * LANES:(h + 1) * LANES])
        k_out[0, h] = tile.astype(k_out.dtype)
        for sb in range(tm // tq):
            vt_out[0, h, sb] = jnp.concatenate(
                [vt[h * HEAD_DIM:(h + 1) * HEAD_DIM, sb * tq:(sb + 1) * tq], ones_row],
                axis=0).astype(vt_out.dtype)


def _bias_selector(H):
    assert 3 * H + 1 <= LANES
    sel = np.zeros((LANES, H * LANES), np.float32)
    for h in range(H):
        sel[3 * H, h * LANES + HEAD_DIM:h * LANES + HEAD_DIM + 3] = 1.0
        for t in range(3):
            sel[t * H + h, h * LANES + HEAD_DIM + 3 + t] = 1.0
    return jnp.asarray(sel, BF16)


def _shared_kv(x, g, wk, wvt, wf, fb, kg, gq, wqt, qg, *, tm, tq, scale):
    B, T, D = x.shape
    H = D // HEAD_DIM
    sel = _bias_selector(H)
    slabs = lambda rows: pl.BlockSpec((1, H, tm // tq, rows, tq), lambda b, i: (b, 0, i, 0, 0))
    return pl.pallas_call(
        functools.partial(_shared_kv_kernel, tq=tq, scale=scale),
        grid=(B, T // tm),
        in_specs=[pl.BlockSpec((1, tm, D), lambda b, i: (b, i, 0)), _const_spec((1, D)),
                  _const_spec(wk.shape), _const_spec(wvt.shape), _const_spec(wf.shape),
                  _const_spec((1, LANES)), _const_spec((1, D)), _const_spec(sel.shape),
                  _const_spec((1, D)), _const_spec(wqt.shape), _const_spec((D, 1))],
        out_specs=[pl.BlockSpec((1, H, tm, LANES), lambda b, i: (b, 0, i, 0)), slabs(V_ROWS),
                   pl.BlockSpec((1, H, tm), lambda b, i: (b, 0, i)), slabs(LANES)],
        out_shape=[jax.ShapeDtypeStruct((B, H, T, LANES), BF16),
                   jax.ShapeDtypeStruct((B, H, T // tq, V_ROWS, tq), BF16),
                   jax.ShapeDtypeStruct((B, H, T), F32),
                   jax.ShapeDtypeStruct((B, H, T // tq, LANES, tq), BF16)],
        scratch_shapes=[pltpu.VMEM((8, LANES), F32)],
        compiler_params=_params(("parallel", "arbitrary")),
        name="shared_kv",
    )(x, g.reshape(1, D), wk, wvt, wf, fb, kg.reshape(1, D), sel,
      gq.reshape(1, D), wqt, qg.reshape(D, 1))


def _fox_attn_kernel(q_ref, k_ref, vt_ref, o_ref, acc_scr, *, tq, nh):
    i = pl.program_id(2)
    qt = [q_ref[0, h, 0] for h in range(nh)]
    acc_scr[...] = jnp.zeros_like(acc_scr)
    key_i = lax.broadcasted_iota(jnp.int32, (tq, tq), 0)
    qry_i = lax.broadcasted_iota(jnp.int32, (tq, tq), 1)
    causal = key_i <= qry_i

    def step(j, m, masked):
        off = pl.multiple_of(j * tq, tq)
        s = [_dot(k_ref[0, h, pl.ds(off, tq), :], qt[h]) for h in range(nh)]
        if masked:
            s = [jnp.where(causal, s[h], NEG_BIG) for h in range(nh)]
        m_new = [jnp.maximum(m[h], jnp.max(s[h], axis=0, keepdims=True)) for h in range(nh)]
        p = [jnp.exp2(s[h] - m_new[h]).astype(BF16) for h in range(nh)]
        alpha = [jnp.exp2(m[h] - m_new[h]) for h in range(nh)]
        pv = [_dot(vt_ref[0, h, j], p[h]) for h in range(nh)]
        for h in range(nh):
            acc_scr[h] = alpha[h] * acc_scr[h] + pv[h]
        return tuple(m_new)

    m0 = tuple(jnp.full((1, tq), NEG_BIG, F32) for _ in range(nh))
    m = lax.fori_loop(0, i, functools.partial(step, masked=False), m0)
    step(i, m, True)
    ot = [acc_scr[h, :HEAD_DIM, :] * (1.0 / acc_scr[h, HEAD_DIM:HEAD_DIM + 1, :]) for h in range(nh)]
    o_ref[0] = jnp.concatenate(ot, axis=0).T.astype(o_ref.dtype)


def _fox_attn(qt, ka, vt, *, nh):
    B, H, nb, _, tq = qt.shape
    T = nb * tq
    return pl.pallas_call(
        functools.partial(_fox_attn_kernel, tq=tq, nh=nh),
        grid=(B, H // nh, nb),
        in_specs=[pl.BlockSpec((1, nh, 1, LANES, tq), lambda b, p, i: (b, p, i, 0, 0)),
                  pl.BlockSpec((1, nh, T, LANES), lambda b, p, i: (b, p, 0, 0)),
                  pl.BlockSpec((1, nh, nb, V_ROWS, tq), lambda b, p, i: (b, p, 0, 0, 0))],
        out_specs=pl.BlockSpec((1, tq, nh * HEAD_DIM), lambda b, p, i: (b, i, p)),
        out_shape=jax.ShapeDtypeStruct((B, T, H * HEAD_DIM), BF16),
        scratch_shapes=[pltpu.VMEM((nh, V_ROWS, tq), F32)],
        compiler_params=_params(("parallel", "parallel", "arbitrary")),
        name="fox_attn",
    )(qt, ka, vt)


def kernel(x, rwkv_norm_g, rwkv_mu, rwkv_w_rkv, rwkv_w0, rwkv_w1, rwkv_w2, rwkv_a0, rwkv_a1, rwkv_a2, rwkv_g1, rwkv_g2, rwkv_k_k, rwkv_k_a, rwkv_r_k, rwkv_lnx_w, rwkv_lnx_b, rwkv_w_o, kv_norm_g, kv_w, kv_f_bias, k_norm_g, attn_norm_g, attn_w_q, q_norm_g, attn_w_o, mlp_norm_g, mlp_w_in, mlp_w_out):
    B, T, D = x.shape
    M = B * T
    n_a = rwkv_norm_g.shape[0]
    depth = mlp_norm_g.shape[0]
    bf = lambda w: w.astype(BF16)
    tm = min(512, T)
    tq = min(256, T)
    tf = min(1024, mlp_w_in.shape[-1])
    npairs = D // LANES

    w_in_all, w_out_all = bf(mlp_w_in), bf(mlp_w_out)
    q_scale = HEAD_DIM ** -0.5 * LOG2E
    k_sh = v_sh = c_sh = q_first = None
    for layer in range(depth):
        if layer < n_a:
            i = layer
            r, lw, k, v, kk, b, g = _rwkv_prep(
                x, rwkv_norm_g[i], rwkv_mu[i], bf(rwkv_w_rkv[i, 0]), bf(rwkv_w_rkv[i, 1]),
                bf(rwkv_w_rkv[i, 2]), rwkv_w0[i], bf(rwkv_w1[i]), bf(rwkv_w2[i]), rwkv_a0[i],
                bf(rwkv_a1[i]), bf(rwkv_a2[i]), bf(rwkv_g1[i]), bf(rwkv_g2[i]),
                rwkv_k_k[i], rwkv_k_a[i], tm=tm)
            mix = _rwkv_scan(r, lw, k, v, kk, b, nchunk=4 if T % (4 * CHUNK) == 0 else 1,
                             nbatch=2 if B % 2 == 0 else 1, npairs=npairs)
            flat = lambda t: t.reshape(M, D)
            w_o, extra = rwkv_w_o[i], (flat(r), flat(k), flat(v), flat(g), rwkv_r_k[i].reshape(D),
                                       rwkv_lnx_w[i], rwkv_lnx_b[i])
        else:
            j = layer - n_a
            qt = q_first if j == 0 else _q_proj(x, attn_norm_g[j], bf(attn_w_q[j].T), q_norm_g[j],
                                                 c_sh, tm=tm, tq=tq, scale=q_scale)
            mix = _fox_attn(qt, k_sh, v_sh, nh=min(16, D // HEAD_DIM))
            w_o, extra = attn_w_o[j], None
        x = _proj_mlp(x.reshape(M, D), mix.reshape(M, D), bf(w_o), mlp_norm_g[layer],
                      w_in_all, w_out_all, layer=layer, tm=tm, tf=tf, rwkv=extra).reshape(B, T, D)
        if layer == n_a - 1:
            wf = jnp.pad(kv_w[:, 2 * D:], ((0, 0), (0, LANES - (kv_w.shape[1] - 2 * D))))
            fb = jnp.pad(kv_f_bias, (0, LANES - kv_f_bias.shape[0])).reshape(1, LANES)
            k_sh, v_sh, c_sh, q_first = _shared_kv(
                x, kv_norm_g, bf(kv_w[:, :D]), bf(kv_w[:, D:2 * D].T), bf(wf), fb, k_norm_g,
                attn_norm_g[0], bf(attn_w_q[0].T), q_norm_g[0], tm=tm, tq=tq, scale=q_scale)
    return x
```

```python
import functools

import jax
import jax.numpy as jnp
import numpy as np
from jax import lax
from jax.experimental import pallas as pl
from jax.experimental.pallas import tpu as pltpu

HEAD_DIM = 64
LANES = 128
NORM_EPS = 1e-6
GN_EPS = 64e-5
CHUNK = 64
NEG_BIG = -1e30
LOG2E = 1.4426950408889634
EXP_M_HALF = 0.6065306597126334
V_ROWS = 80
VMEM_LIMIT = 56 * 1024 * 1024

BF16 = jnp.bfloat16
F32 = jnp.float32

_NT = (((1,), (1,)), ((), ()))
_TN = (((0,), (0,)), ((), ()))


def _dot(a, b):
    return jnp.dot(a, b, preferred_element_type=F32)


def _dot_nt(a, b):
    return lax.dot_general(a, b, _NT, preferred_element_type=F32)


def _dot_tn(a, b):
    return lax.dot_general(a, b, _TN, preferred_element_type=F32)


def _split(a):
    hi = a.astype(BF16)
    return hi, (a - hi.astype(F32)).astype(BF16)


def _split3(a):
    hi = a.astype(BF16).astype(F32)
    r1 = a - hi
    mid = r1.astype(BF16).astype(F32)
    return hi, mid, r1 - mid


def _rms(x, g):
    return x * lax.rsqrt(jnp.mean(x * x, axis=-1, keepdims=True) + NORM_EPS) * g


def _head_sum(x):
    outs = []
    for c in range(x.shape[1] // LANES):
        xc = x[:, c * LANES:(c + 1) * LANES]
        lo = lax.broadcasted_iota(jnp.int32, xc.shape, 1) < HEAD_DIM
        s0 = jnp.sum(jnp.where(lo, xc, 0.0), axis=1, keepdims=True)
        s1 = jnp.sum(jnp.where(lo, 0.0, xc), axis=1, keepdims=True)
        outs.append(jnp.where(lo, s0, s1))
    return outs[0] if len(outs) == 1 else jnp.concatenate(outs, axis=1)


def _head_rms(t, g):
    ms = _head_sum(t * t) * (1.0 / HEAD_DIM)
    return t * lax.rsqrt(ms + NORM_EPS) * g


def _sigmoid(z):
    return 1.0 / (1.0 + jnp.exp(-z))


def _const_spec(shape):
    nd = len(shape)
    return pl.BlockSpec(shape, lambda *_: (0,) * nd)


def _params(sem):
    return pltpu.CompilerParams(dimension_semantics=sem, vmem_limit_bytes=VMEM_LIMIT)


def _rwkv_prep_kernel(x_ref, xp_ref, ng_ref, mu_ref, wr_ref, wk_ref, wv_ref,
                      w0_ref, w1_ref, w2_ref, a0_ref, a1_ref, a2_ref, g1_ref, g2_ref,
                      kkw_ref, kaw_ref,
                      r_out, lw_out, k_out, v_out, kk_out, b_out, g_out):
    i = pl.program_id(1)
    ng = ng_ref[...]
    h = _rms(x_ref[0], ng)
    hp = _rms(xp_ref[0][7:8, :], ng)
    hp = jnp.where(i > 0, hp, 0.0)
    rolled = pltpu.roll(h, 1, 0)
    first = jnp.where(lax.broadcasted_iota(jnp.int32, (8, h.shape[1]), 0) == 0, hp, rolled[:8])
    hs = jnp.concatenate([first, rolled[8:]], axis=0)
    hb = h.astype(BF16)
    xxb = (hs - h).astype(BF16)
    mub = mu_ref[...].astype(BF16)

    def mix(j):
        return hb + xxb * mub[j:j + 1, :]

    tw = _dot(mix(1), w1_ref[...])
    ta = _dot(mix(4), a1_ref[...])
    tg = _dot(mix(5), g1_ref[...])
    k = _dot(mix(2), wk_ref[...])
    wl = _dot(jnp.tanh(tw).astype(BF16), w2_ref[...])
    al = _dot(ta.astype(BF16), a2_ref[...])
    g_out[0] = _dot(_sigmoid(tg).astype(BF16), g2_ref[...]).astype(g_out.dtype)

    lw_out[0] = -EXP_M_HALF * _sigmoid(w0_ref[...] + wl)
    a = _sigmoid(a0_ref[...] + al)
    kk = k * kkw_ref[...]
    kk = kk * lax.rsqrt(jnp.maximum(_head_sum(kk * kk), 1e-24))
    k_out[0] = (k * (1.0 + (a - 1.0) * kaw_ref[...])).astype(k_out.dtype)
    kk_out[0] = kk.astype(kk_out.dtype)
    b_out[0] = (kk * a).astype(b_out.dtype)

    r_out[0] = _dot(mix(0), wr_ref[...]).astype(r_out.dtype)
    v_out[0] = _dot(mix(3), wv_ref[...]).astype(v_out.dtype)


def _rwkv_prep(x, ng, mu, wr, wk, wv, w0, w1, w2, a0, a1, a2, g1, g2, kkw, kaw, *, tm):
    B, T, D = x.shape
    row = lambda a: a.reshape(1, D)
    consts = [row(ng), mu, wr, wk, wv, row(w0), w1, w2, row(a0), a1, a2, g1, g2, row(kkw), row(kaw)]
    tile = pl.BlockSpec((1, tm, D), lambda b, i: (b, i, 0))
    prev = pl.BlockSpec((1, 8, D), lambda b, i: (b, jnp.maximum(i * (tm // 8) - 1, 0), 0))
    out = lambda dt: jax.ShapeDtypeStruct((B, T, D), dt)
    return pl.pallas_call(
        _rwkv_prep_kernel,
        grid=(B, T // tm),
        in_specs=[tile, prev] + [_const_spec(c.shape) for c in consts],
        out_specs=[tile] * 7,
        out_shape=[out(BF16), out(F32)] + [out(BF16)] * 5,
        compiler_params=_params(("parallel", "parallel")),
        name="rwkv_prep",
    )(x, x, *consts)


def _blockdiag(z, lo):
    z = z.astype(BF16)
    zero = jnp.zeros_like(z)
    return jnp.concatenate([jnp.where(lo, z, zero), jnp.where(lo, zero, z)], axis=0)


def _rwkv_scan_kernel(r_ref, lw_ref, k_ref, v_ref, kk_ref, b_ref, g_ref, rk_ref, lnw_ref, lnb_ref,
                      y_out, s_scr, *, nchunk, nbatch, npairs):
    c = pl.program_id(2)

    @pl.when(c == 0)
    def _():
        s_scr[...] = jnp.zeros_like(s_scr)

    C = CHUNK
    t_i = lax.broadcasted_iota(jnp.int32, (C, C), 0)
    j_i = lax.broadcasted_iota(jnp.int32, (C, C), 1)
    ltri = (j_i <= t_i).astype(BF16)
    row = lax.broadcasted_iota(jnp.int32, (C, LANES), 0)
    lane = lax.broadcasted_iota(jnp.int32, (C, LANES), 1)
    lo = lane < HEAD_DIM
    col = jnp.bitwise_and(lane, HEAD_DIM - 1)
    strict = col < row
    incl = col <= row
    rr = lax.broadcasted_iota(jnp.int32, (LANES, LANES), 0)
    cc = lax.broadcasted_iota(jnp.int32, (LANES, LANES), 1)
    same_head = (rr < HEAD_DIM) == (cc < HEAD_DIM)
    bd = functools.partial(_blockdiag, lo=lo)
    cat0 = lambda *xs: jnp.concatenate([x.astype(BF16) for x in xs], axis=0)
    cat1 = lambda *xs: jnp.concatenate([x.astype(BF16) for x in xs], axis=1)

    nstate = nbatch * npairs
    units = [(bi, slice(ci * C, (ci + 1) * C), slice(p * LANES, (p + 1) * LANES))
             for ci in range(nchunk) for bi in range(nbatch) for p in range(npairs)]
    P = range(len(units))
    r = [r_ref[u].astype(F32) for u in units]
    lw = [lw_ref[u] for u in units]
    k = [k_ref[u].astype(F32) for u in units]
    v = [v_ref[u].astype(F32) for u in units]
    kk = [kk_ref[u].astype(F32) for u in units]
    b = [b_ref[u].astype(F32) for u in units]

    cw2 = [_dot(ltri, cat1(*_split(lw[p]))) for p in P]
    cw = [cw2[p][:, :LANES] + cw2[p][:, LANES:] for p in P]
    cwl = [cw[p][C - 1:C, :] for p in P]
    at = [-kk[p] * jnp.exp(cw[p] - lw[p]) for p in P]
    dinv = [jnp.exp(-cw[p]) for p in P]
    rt = [r[p] * jnp.exp(cw[p]) for p in P]
    dend = [jnp.exp(cwl[p] - cw[p]) for p in P]

    x = [_dot_nt(cat0(at[p], rt[p]), cat0(bd(b[p] * dinv[p]), bd(k[p] * dinv[p]))) for p in P]
    aab = [jnp.where(strict, x[p][:C, :LANES], 0.0) for p in P]
    arb = [jnp.where(incl, x[p][C:, :LANES], 0.0) for p in P]
    aak = [jnp.where(strict, x[p][:C, LANES:], 0.0) for p in P]
    ark = [jnp.where(incl, x[p][C:, LANES:], 0.0) for p in P]

    bdv = [bd(v[p]) for p in P]
    av = [_dot(aak[p].astype(BF16), bdv[p]) for p in P]
    eye = jnp.where(col == row, 1.0, 0.0)
    tinv = [eye + aab[p] for p in P]
    n = [_dot(aab[p].astype(BF16), bd(aab[p])) for p in P]
    for it in range(5):
        last = it == 4
        res = [_dot(n[p].astype(BF16),
                    jnp.concatenate([bd(tinv[p])] + ([] if last else [bd(n[p])]), axis=1)) for p in P]
        tinv = [tinv[p] + res[p][:, :LANES] for p in P]
        if not last:
            n = [res[p][:, LANES:] for p in P]
    z = [_dot(tinv[p].astype(BF16), jnp.concatenate([bd(at[p]), bd(av[p])], axis=1)) for p in P]
    z1 = [z[p][:, :LANES] for p in P]
    z2 = [z[p][:, LANES:] for p in P]

    s = [s_scr[q] for q in range(nstate)]
    for ci in range(nchunk):
        Q = range(ci * nstate, (ci + 1) * nstate)
        ws = [_dot_nt(cat0(z1[p], rt[p]), s[p - Q[0]].astype(BF16)) for p in Q]
        u = [ws[p - Q[0]][:C] + z2[p] for p in Q]
        y = [ws[p - Q[0]][C:] + _dot(cat1(arb[p], ark[p]), cat0(bd(u[p - Q[0]]), bdv[p])) for p in Q]
        upd = [_dot_tn(cat0(u[p - Q[0]], v[p]), cat0(b[p] * dend[p], k[p] * dend[p])) for p in Q]
        s = [s[p - Q[0]] * jnp.exp(cwl[p]) + jnp.where(same_head, upd[p - Q[0]], 0.0) for p in Q]
        for p in Q:
            sl = units[p][2]
            d = y[p - Q[0]] - _head_sum(y[p - Q[0]]) * (1.0 / HEAD_DIM)
            var = _head_sum(d * d) * (1.0 / HEAD_DIM)
            yn = d * lax.rsqrt(var + GN_EPS)
            bonus = _head_sum(r[p] * k[p] * rk_ref[:, sl]) * v[p]
            out = (yn * lnw_ref[:, sl] + lnb_ref[:, sl] + bonus) * g_ref[units[p]].astype(F32)
            y_out[units[p]] = out.astype(y_out.dtype)
    for q in range(nstate):
        s_scr[q] = s[q]


def _rwkv_scan(r, lw, k, v, kk, b, g, rk, lnw, lnb, *, nchunk, nbatch, npairs):
    B, T, D = r.shape
    W = npairs * LANES
    tile = pl.BlockSpec((nbatch, nchunk * CHUNK, W), lambda bi, p, c: (bi, c, p))
    vec = pl.BlockSpec((1, W), lambda bi, p, c: (0, p))
    return pl.pallas_call(
        functools.partial(_rwkv_scan_kernel, nchunk=nchunk, nbatch=nbatch, npairs=npairs),
        grid=(B // nbatch, D // W, T // (nchunk * CHUNK)),
        in_specs=[tile] * 7 + [vec] * 3,
        out_specs=tile,
        out_shape=jax.ShapeDtypeStruct((B, T, D), BF16),
        scratch_shapes=[pltpu.VMEM((nbatch * npairs, LANES, LANES), F32)],
        compiler_params=_params(("parallel", "parallel", "arbitrary")),
        name="rwkv_scan",
    )(r, lw, k, v, kk, b, g, rk.reshape(1, D), lnw.reshape(1, D), lnb.reshape(1, D))


def _proj_mlp_kernel(res_ref, a_ref, wo_ref, g_ref, win_ref, wout_ref, o_ref, *, tf):
    x = res_ref[...] + _dot(a_ref[...], wo_ref[...])
    xn = _rms(x, g_ref[...]).astype(BF16)
    acc = x
    for f in range(win_ref.shape[-1] // tf):
        hid = jnp.maximum(_dot(xn, win_ref[:, f * tf:(f + 1) * tf]), 0.0)
        acc = acc + _dot((hid * hid).astype(BF16), wout_ref[f * tf:(f + 1) * tf, :])
    o_ref[...] = acc


def _proj_mlp(res, mix, w_o, g, w_in, w_out, *, layer, tm, tf):
    M, D = res.shape
    tile = pl.BlockSpec((tm, D), lambda i: (i, 0))
    once = lambda shape: pl.BlockSpec(shape, lambda i: (0,) * len(shape), pipeline_mode=pl.Buffered(1))
    of_layer = lambda w: pl.BlockSpec((None,) + w.shape[1:], lambda i: (layer, 0, 0),
                                      pipeline_mode=pl.Buffered(1))
    return pl.pallas_call(
        functools.partial(_proj_mlp_kernel, tf=tf),
        grid=(M // tm,),
        in_specs=[tile, tile, once(w_o.shape), once((1, D)), of_layer(w_in), of_layer(w_out)],
        out_specs=tile,
        out_shape=jax.ShapeDtypeStruct((M, D), F32),
        compiler_params=_params(("parallel",)),
        name="proj_mlp",
    )(res, mix, w_o, g.reshape(1, D), w_in, w_out)


def _store_q_tiles(qt, c_rows, qg_ref, q_out, *, scale, tq):
    tm = qt.shape[1]
    row = lax.broadcasted_iota(jnp.int32, (HEAD_DIM, tm), 0)
    for h in range(qt.shape[0] // HEAD_DIM):
        hs = slice(h * HEAD_DIM, (h + 1) * HEAD_DIM)
        qh = qt[hs, :]
        ms = jnp.mean(qh * qh, axis=0, keepdims=True)
        qn = qh * lax.rsqrt(ms + NORM_EPS) * (qg_ref[hs, :] * scale)
        hi, mid, lo = _split3(c_rows[h:h + 1, :] * LOG2E)
        aug = jnp.where(row == 0, hi, jnp.where(row == 1, mid, jnp.where(
            row == 2, lo, jnp.where(row < 6, 1.0, 0.0))))
        tile = jnp.concatenate([qn, aug], axis=0).astype(q_out.dtype)
        for sb in range(tm // tq):
            q_out[0, h, sb] = tile[:, sb * tq:(sb + 1) * tq]


def _q_proj_kernel(x_ref, g_ref, wt_ref, qg_ref, c_ref, q_out, *, scale, tq):
    hn = _rms(x_ref[0], g_ref[...])
    qt = _dot_nt(wt_ref[...], hn.astype(BF16))
    _store_q_tiles(qt, c_ref[0], qg_ref, q_out, scale=scale, tq=tq)


def _q_proj(x, g, wt, qg, c_row, *, tm, tq, scale):
    B, T, D = x.shape
    H = D // HEAD_DIM
    return pl.pallas_call(
        functools.partial(_q_proj_kernel, scale=scale, tq=tq),
        grid=(B, T // tm),
        in_specs=[pl.BlockSpec((1, tm, D), lambda b, i: (b, i, 0)), _const_spec((1, D)),
                  _const_spec(wt.shape), _const_spec((D, 1)),
                  pl.BlockSpec((1, H, tm), lambda b, i: (b, 0, i))],
        out_specs=pl.BlockSpec((1, H, tm // tq, LANES, tq), lambda b, i: (b, 0, i, 0, 0)),
        out_shape=jax.ShapeDtypeStruct((B, H, T // tq, LANES, tq), BF16),
        compiler_params=_params(("parallel", "parallel")),
        name="q_proj",
    )(x, g.reshape(1, D), wt, qg.reshape(D, 1), c_row)


def _shared_kv_kernel(x_ref, g_ref, wk_ref, wvt_ref, wf_ref, fb_ref, kg_ref, sel_ref,
                      gq_ref, wqt_ref, qg_ref,
                      k_out, vt_out, c_out, q_out, carry_scr, *, tq, scale):
    i = pl.program_id(1)

    @pl.when(i == 0)
    def _():
        carry_scr[...] = jnp.zeros_like(carry_scr)

    x = x_ref[0]
    xn = x * lax.rsqrt(jnp.mean(x * x, axis=-1, keepdims=True) + NORM_EPS)
    hn = xn * g_ref[...]
    hb = hn.astype(BF16)
    tm, D = hn.shape
    H = D // HEAD_DIM
    k = _head_rms(_dot(hb, wk_ref[...]), kg_ref[...])
    vt = _dot_nt(wvt_ref[...], hb)

    f = _dot(hb, wf_ref[...]) + fb_ref[...]
    logf = jnp.minimum(f, 0.0) - jnp.log(1.0 + jnp.exp(-jnp.abs(f)))
    t_i = lax.broadcasted_iota(jnp.int32, (tm, tm), 0)
    j_i = lax.broadcasted_iota(jnp.int32, (tm, tm), 1)
    ltri = (j_i <= t_i).astype(BF16)
    c3 = _dot(ltri, jnp.concatenate([t.astype(BF16) for t in _split3(logf)], axis=1))
    c = (c3[:, :LANES] + c3[:, LANES:2 * LANES]) + c3[:, 2 * LANES:] + carry_scr[0:1, :]
    carry_scr[...] = jnp.broadcast_to(c[tm - 1:tm, :], carry_scr.shape)
    c_rows = c.T[:H, :]
    c_out[0] = c_rows
    qt = _dot_nt(wqt_ref[...], (xn * gq_ref[...]).astype(BF16))
    _store_q_tiles(qt, c_rows, qg_ref, q_out, scale=scale, tq=tq)

    lane = lax.broadcasted_iota(jnp.int32, (tm, LANES), 1)
    hi, mid, lo = (jnp.where(lane < H, t, 0.0) for t in _split3(c * (-LOG2E)))
    packed = (hi + pltpu.roll(mid, H, 1)) + (pltpu.roll(lo, 2 * H, 1) + jnp.where(lane == 3 * H, 1.0, 0.0))
    aug = _dot(packed.astype(BF16), sel_ref[...])
    vrow = lax.broadcasted_iota(jnp.int32, (V_ROWS - HEAD_DIM, tq), 0)
    ones_row = jnp.where(vrow == 0, 1.0, 0.0)
    for h in range(H):
        base = k[:, (h // 2) * LANES:(h // 2 + 1) * LANES]
        if h % 2:
            base = pltpu.roll(base, HEAD_DIM, 1)
        tile = jnp.where(lane < HEAD_DIM, base, aug[:, h * LANES:(h + 1) * LANES])
        k_out[0, h] = tile.astype(k_out.dtype)
        for sb in range(tm // tq):
            vt_out[0, h, sb] = jnp.concatenate(
                [vt[h * HEAD_DIM:(h + 1) * HEAD_DIM, sb * tq:(sb + 1) * tq], ones_row],
                axis=0).astype(vt_out.dtype)


def _bias_selector(H):
    assert 3 * H + 1 <= LANES
    sel = np.zeros((LANES, H * LANES), np.float32)
    for h in range(H):
        sel[3 * H, h * LANES + HEAD_DIM:h * LANES + HEAD_DIM + 3] = 1.0
        for t in range(3):
            sel[t * H + h, h * LANES + HEAD_DIM + 3 + t] = 1.0
    return jnp.asarray(sel, BF16)


def _shared_kv(x, g, wk, wvt, wf, fb, kg, gq, wqt, qg, *, tm, tq, scale):
    B, T, D = x.shape
    H = D // HEAD_DIM
    sel = _bias_selector(H)
    slabs = lambda rows: pl.BlockSpec((1, H, tm // tq, rows, tq), lambda b, i: (b, 0, i, 0, 0))
    return pl.pallas_call(
        functools.partial(_shared_kv_kernel, tq=tq, scale=scale),
        grid=(B, T // tm),
        in_specs=[pl.BlockSpec((1, tm, D), lambda b, i: (b, i, 0)), _const_spec((1, D)),
                  _const_spec(wk.shape), _const_spec(wvt.shape), _const_spec(wf.shape),
                  _const_spec((1, LANES)), _const_spec((1, D)), _const_spec(sel.shape),
                  _const_spec((1, D)), _const_spec(wqt.shape), _const_spec((D, 1))],
        out_specs=[pl.BlockSpec((1, H, tm, LANES), lambda b, i: (b, 0, i, 0)), slabs(V_ROWS),
                   pl.BlockSpec((1, H, tm), lambda b, i: (b, 0, i)), slabs(LANES)],
        out_shape=[jax.ShapeDtypeStruct((B, H, T, LANES), BF16),
                   jax.ShapeDtypeStruct((B, H, T // tq, V_ROWS, tq), BF16),
                   jax.ShapeDtypeStruct((B, H, T), F32),
                   jax.ShapeDtypeStruct((B, H, T // tq, LANES, tq), BF16)],
        scratch_shapes=[pltpu.VMEM((8, LANES), F32)],
        compiler_params=_params(("parallel", "arbitrary")),
        name="shared_kv",
    )(x, g.reshape(1, D), wk, wvt, wf, fb, kg.reshape(1, D), sel,
      gq.reshape(1, D), wqt, qg.reshape(D, 1))


def _fox_attn_kernel(q_ref, k_ref, vt_ref, o_ref, acc_scr, *, tq, nh):
    i = pl.program_id(2)
    qt = [q_ref[0, h, 0] for h in range(nh)]
    acc_scr[...] = jnp.zeros_like(acc_scr)
    key_i = lax.broadcasted_iota(jnp.int32, (tq, tq), 0)
    qry_i = lax.broadcasted_iota(jnp.int32, (tq, tq), 1)
    causal = key_i <= qry_i

    def step(j, m, masked):
        off = pl.multiple_of(j * tq, tq)
        s = [_dot(k_ref[0, h, pl.ds(off, tq), :], qt[h]) for h in range(nh)]
        if masked:
            s = [jnp.where(causal, s[h], NEG_BIG) for h in range(nh)]
        m_new = [jnp.maximum(m[h], jnp.max(s[h], axis=0, keepdims=True)) for h in range(nh)]
        p = [jnp.exp2(s[h] - m_new[h]).astype(BF16) for h in range(nh)]
        alpha = [jnp.exp2(m[h] - m_new[h]) for h in range(nh)]
        pv = [_dot(vt_ref[0, h, j], p[h]) for h in range(nh)]
        for h in range(nh):
            acc_scr[h] = alpha[h] * acc_scr[h] + pv[h]
        return tuple(m_new)

    m0 = tuple(jnp.full((1, tq), NEG_BIG, F32) for _ in range(nh))
    m = lax.fori_loop(0, i, functools.partial(step, masked=False), m0)
    step(i, m, True)
    ot = [acc_scr[h, :HEAD_DIM, :] * (1.0 / acc_scr[h, HEAD_DIM:HEAD_DIM + 1, :]) for h in range(nh)]
    o_ref[0] = jnp.concatenate(ot, axis=0).T.astype(o_ref.dtype)


def _fox_attn(qt, ka, vt, *, nh):
    B, H, nb, _, tq = qt.shape
    T = nb * tq
    return pl.pallas_call(
        functools.partial(_fox_attn_kernel, tq=tq, nh=nh),
        grid=(B, H // nh, nb),
        in_specs=[pl.BlockSpec((1, nh, 1, LANES, tq), lambda b, p, i: (b, p, i, 0, 0)),
                  pl.BlockSpec((1, nh, T, LANES), lambda b, p, i: (b, p, 0, 0)),
                  pl.BlockSpec((1, nh, nb, V_ROWS, tq), lambda b, p, i: (b, p, 0, 0, 0))],
        out_specs=pl.BlockSpec((1, tq, nh * HEAD_DIM), lambda b, p, i: (b, i, p)),
        out_shape=jax.ShapeDtypeStruct((B, T, H * HEAD_DIM), BF16),
        scratch_shapes=[pltpu.VMEM((nh, V_ROWS, tq), F32)],
        compiler_params=_params(("parallel", "parallel", "arbitrary")),
        name="fox_attn",
    )(qt, ka, vt)


def kernel(x, rwkv_norm_g, rwkv_mu, rwkv_w_rkv, rwkv_w0, rwkv_w1, rwkv_w2, rwkv_a0, rwkv_a1, rwkv_a2, rwkv_g1, rwkv_g2, rwkv_k_k, rwkv_k_a, rwkv_r_k, rwkv_lnx_w, rwkv_lnx_b, rwkv_w_o, kv_norm_g, kv_w, kv_f_bias, k_norm_g, attn_norm_g, attn_w_q, q_norm_g, attn_w_o, mlp_norm_g, mlp_w_in, mlp_w_out):
    B, T, D = x.shape
    M = B * T
    n_a = rwkv_norm_g.shape[0]
    depth = mlp_norm_g.shape[0]
    bf = lambda w: w.astype(BF16)
    tm = min(512, T)
    tq = min(256, T)
    tf = min(1024, mlp_w_in.shape[-1])
    npairs = D // LANES

    w_in_all, w_out_all = bf(mlp_w_in), bf(mlp_w_out)
    q_scale = HEAD_DIM ** -0.5 * LOG2E
    k_sh = v_sh = c_sh = q_first = None
    for layer in range(depth):
        if layer < n_a:
            i = layer
            r, lw, k, v, kk, b, g = _rwkv_prep(
                x, rwkv_norm_g[i], rwkv_mu[i], bf(rwkv_w_rkv[i, 0]), bf(rwkv_w_rkv[i, 1]),
                bf(rwkv_w_rkv[i, 2]), rwkv_w0[i], bf(rwkv_w1[i]), bf(rwkv_w2[i]), rwkv_a0[i],
                bf(rwkv_a1[i]), bf(rwkv_a2[i]), bf(rwkv_g1[i]), bf(rwkv_g2[i]),
                rwkv_k_k[i], rwkv_k_a[i], tm=tm)
            mix = _rwkv_scan(r, lw, k, v, kk, b, g, rwkv_r_k[i], rwkv_lnx_w[i], rwkv_lnx_b[i],
                             nchunk=4 if T % (4 * CHUNK) == 0 else 1,
                             nbatch=2 if B % 2 == 0 else 1, npairs=npairs)
            w_o = rwkv_w_o[i]
        else:
            j = layer - n_a
            qt = q_first if j == 0 else _q_proj(x, attn_norm_g[j], bf(attn_w_q[j].T), q_norm_g[j],
                                                 c_sh, tm=tm, tq=tq, scale=q_scale)
            mix = _fox_attn(qt, k_sh, v_sh, nh=min(16, D // HEAD_DIM))
            w_o = attn_w_o[j]
        x = _proj_mlp(x.reshape(M, D), mix.reshape(M, D), bf(w_o), mlp_norm_g[layer],
                      w_in_all, w_out_all, layer=layer, tm=tm, tf=tf).reshape(B, T, D)
        if layer == n_a - 1:
            wf = jnp.pad(kv_w[:, 2 * D:], ((0, 0), (0, LANES - (kv_w.shape[1] - 2 * D))))
            fb = jnp.pad(kv_f_bias, (0, LANES - kv_f_bias.shape[0])).reshape(1, LANES)
            k_sh, v_sh, c_sh, q_first = _shared_kv(
                x, kv_norm_g, bf(kv_w[:, :D]), bf(kv_w[:, D:2 * D].T), bf(wf), fb, k_norm_g,
                attn_norm_g[0], bf(attn_w_q[0].T), q_norm_g[0], tm=tm, tq=tq, scale=q_scale)
    return x
```

```python
import functools

import jax
import jax.numpy as jnp
import numpy as np
from jax import lax
from jax.experimental import pallas as pl
from jax.experimental.pallas import tpu as pltpu

HEAD_DIM = 64
LANES = 128
NORM_EPS = 1e-6
GN_EPS = 64e-5
CHUNK = 64
NEG_BIG = -1e30
LOG2E = 1.4426950408889634
EXP_M_HALF = 0.6065306597126334
V_ROWS = 80
QK_DEPTH = 80
SKEW = 6
VMEM_LIMIT = 56 * 1024 * 1024

BF16 = jnp.bfloat16
F32 = jnp.float32

_NT = (((1,), (1,)), ((), ()))
_TN = (((0,), (0,)), ((), ()))


def _dot(a, b):
    return jnp.dot(a, b, preferred_element_type=F32)


def _dot_nt(a, b):
    return lax.dot_general(a, b, _NT, preferred_element_type=F32)


def _dot_tn(a, b):
    return lax.dot_general(a, b, _TN, preferred_element_type=F32)


def _split(a):
    hi = a.astype(BF16)
    return hi, (a - hi.astype(F32)).astype(BF16)


def _split3(a):
    hi = a.astype(BF16).astype(F32)
    r1 = a - hi
    mid = r1.astype(BF16).astype(F32)
    return hi, mid, r1 - mid


def _rms(x, g):
    return x * lax.rsqrt(jnp.mean(x * x, axis=-1, keepdims=True) + NORM_EPS) * g


def _head_sum(x):
    outs = []
    for c in range(x.shape[1] // LANES):
        xc = x[:, c * LANES:(c + 1) * LANES]
        lo = lax.broadcasted_iota(jnp.int32, xc.shape, 1) < HEAD_DIM
        s0 = jnp.sum(jnp.where(lo, xc, 0.0), axis=1, keepdims=True)
        s1 = jnp.sum(jnp.where(lo, 0.0, xc), axis=1, keepdims=True)
        outs.append(jnp.where(lo, s0, s1))
    return outs[0] if len(outs) == 1 else jnp.concatenate(outs, axis=1)


def _head_rms(t, g):
    ms = _head_sum(t * t) * (1.0 / HEAD_DIM)
    return t * lax.rsqrt(ms + NORM_EPS) * g


def _sigmoid(z):
    return 1.0 / (1.0 + jnp.exp(-z))


def _const_spec(shape):
    nd = len(shape)
    return pl.BlockSpec(shape, lambda *_: (0,) * nd)


def _params(sem):
    return pltpu.CompilerParams(dimension_semantics=sem, vmem_limit_bytes=VMEM_LIMIT)


def _rwkv_prep_kernel(x_ref, xp_ref, ng_ref, mu_ref, wr_ref, wk_ref, wv_ref,
                      w0_ref, w1_ref, w2_ref, a0_ref, a1_ref, a2_ref, g1_ref, g2_ref,
                      kkw_ref, kaw_ref,
                      r_out, lw_out, k_out, v_out, kk_out, b_out, g_out):
    i = pl.program_id(1)
    ng = ng_ref[...]
    h = _rms(x_ref[0], ng)
    hp = _rms(xp_ref[0][7:8, :], ng)
    hp = jnp.where(i > 0, hp, 0.0)
    rolled = pltpu.roll(h, 1, 0)
    first = jnp.where(lax.broadcasted_iota(jnp.int32, (8, h.shape[1]), 0) == 0, hp, rolled[:8])
    hs = jnp.concatenate([first, rolled[8:]], axis=0)
    hb = h.astype(BF16)
    xxb = (hs - h).astype(BF16)
    mub = mu_ref[...].astype(BF16)

    def mix(j):
        return hb + xxb * mub[j:j + 1, :]

    tw = _dot(mix(1), w1_ref[...])
    ta = _dot(mix(4), a1_ref[...])
    tg = _dot(mix(5), g1_ref[...])
    k = _dot(mix(2), wk_ref[...])
    wl = _dot(jnp.tanh(tw).astype(BF16), w2_ref[...])
    al = _dot(ta.astype(BF16), a2_ref[...])
    g_out[0] = _dot(_sigmoid(tg).astype(BF16), g2_ref[...]).astype(g_out.dtype)

    lw_out[0] = -EXP_M_HALF * _sigmoid(w0_ref[...] + wl)
    a = _sigmoid(a0_ref[...] + al)
    kk = k * kkw_ref[...]
    kk = kk * lax.rsqrt(jnp.maximum(_head_sum(kk * kk), 1e-24))
    k_out[0] = (k * (1.0 + (a - 1.0) * kaw_ref[...])).astype(k_out.dtype)
    kk_out[0] = kk.astype(kk_out.dtype)
    b_out[0] = (kk * a).astype(b_out.dtype)

    r_out[0] = _dot(mix(0), wr_ref[...]).astype(r_out.dtype)
    v_out[0] = _dot(mix(3), wv_ref[...]).astype(v_out.dtype)


def _rwkv_prep(x, ng, mu, wr, wk, wv, w0, w1, w2, a0, a1, a2, g1, g2, kkw, kaw, *, tm):
    B, T, D = x.shape
    row = lambda a: a.reshape(1, D)
    consts = [row(ng), mu, wr, wk, wv, row(w0), w1, w2, row(a0), a1, a2, g1, g2, row(kkw), row(kaw)]
    tile = pl.BlockSpec((1, tm, D), lambda b, i: (b, i, 0))
    prev = pl.BlockSpec((1, 8, D), lambda b, i: (b, jnp.maximum(i * (tm // 8) - 1, 0), 0))
    out = lambda dt: jax.ShapeDtypeStruct((B, T, D), dt)
    return pl.pallas_call(
        _rwkv_prep_kernel,
        grid=(B, T // tm),
        in_specs=[tile, prev] + [_const_spec(c.shape) for c in consts],
        out_specs=[tile] * 7,
        out_shape=[out(BF16), out(F32)] + [out(BF16)] * 5,
        compiler_params=_params(("parallel", "parallel")),
        name="rwkv_prep",
    )(x, x, *consts)


def _blockdiag(z, lo):
    z = z.astype(BF16)
    zero = jnp.zeros_like(z)
    return jnp.concatenate([jnp.where(lo, z, zero), jnp.where(lo, zero, z)], axis=0)


def _rwkv_scan_kernel(r_ref, lw_ref, k_ref, v_ref, kk_ref, b_ref, y_out, s_scr, *,
                      nchunk, nbatch, npairs):
    c = pl.program_id(2)

    @pl.when(c == 0)
    def _():
        s_scr[...] = jnp.zeros_like(s_scr)

    C = CHUNK
    t_i = lax.broadcasted_iota(jnp.int32, (C, C), 0)
    j_i = lax.broadcasted_iota(jnp.int32, (C, C), 1)
    ltri = (j_i <= t_i).astype(BF16)
    row = lax.broadcasted_iota(jnp.int32, (C, LANES), 0)
    lane = lax.broadcasted_iota(jnp.int32, (C, LANES), 1)
    lo = lane < HEAD_DIM
    col = jnp.bitwise_and(lane, HEAD_DIM - 1)
    strict = col < row
    incl = col <= row
    rr = lax.broadcasted_iota(jnp.int32, (LANES, LANES), 0)
    cc = lax.broadcasted_iota(jnp.int32, (LANES, LANES), 1)
    same_head = (rr < HEAD_DIM) == (cc < HEAD_DIM)
    bd = functools.partial(_blockdiag, lo=lo)
    cat0 = lambda *xs: jnp.concatenate([x.astype(BF16) for x in xs], axis=0)
    cat1 = lambda *xs: jnp.concatenate([x.astype(BF16) for x in xs], axis=1)

    nstate = nbatch * npairs
    units = [(bi, slice(ci * C, (ci + 1) * C), slice(p * LANES, (p + 1) * LANES))
             for ci in range(nchunk) for bi in range(nbatch) for p in range(npairs)]
    P = range(len(units))
    r = [r_ref[u].astype(F32) for u in units]
    lw = [lw_ref[u] for u in units]
    k = [k_ref[u].astype(F32) for u in units]
    v = [v_ref[u].astype(F32) for u in units]
    kk = [kk_ref[u].astype(F32) for u in units]
    b = [b_ref[u].astype(F32) for u in units]

    cw2 = [_dot(ltri, cat1(*_split(lw[p]))) for p in P]
    cw = [cw2[p][:, :LANES] + cw2[p][:, LANES:] for p in P]
    cwl = [cw[p][C - 1:C, :] for p in P]
    at = [-kk[p] * jnp.exp(cw[p] - lw[p]) for p in P]
    dinv = [jnp.exp(-cw[p]) for p in P]
    rt = [r[p] * jnp.exp(cw[p]) for p in P]
    dend = [jnp.exp(cwl[p] - cw[p]) for p in P]

    x = [_dot_nt(cat0(at[p], rt[p]), cat0(bd(b[p] * dinv[p]), bd(k[p] * dinv[p]))) for p in P]
    aab = [jnp.where(strict, x[p][:C, :LANES], 0.0) for p in P]
    arb = [jnp.where(incl, x[p][C:, :LANES], 0.0) for p in P]
    aak = [jnp.where(strict, x[p][:C, LANES:], 0.0) for p in P]
    ark = [jnp.where(incl, x[p][C:, LANES:], 0.0) for p in P]

    bdv = [bd(v[p]) for p in P]
    av = [_dot(aak[p].astype(BF16), bdv[p]) for p in P]
    eye = jnp.where(col == row, 1.0, 0.0)
    tinv = [eye + aab[p] for p in P]
    n = [_dot(aab[p].astype(BF16), bd(aab[p])) for p in P]
    for it in range(5):
        last = it == 4
        res = [_dot(n[p].astype(BF16),
                    jnp.concatenate([bd(tinv[p])] + ([] if last else [bd(n[p])]), axis=1)) for p in P]
        tinv = [tinv[p] + res[p][:, :LANES] for p in P]
        if not last:
            n = [res[p][:, LANES:] for p in P]
    z = [_dot(tinv[p].astype(BF16), jnp.concatenate([bd(at[p]), bd(av[p])], axis=1)) for p in P]
    z1 = [z[p][:, :LANES] for p in P]
    z2 = [z[p][:, LANES:] for p in P]

    s = [s_scr[q] for q in range(nstate)]
    for ci in range(nchunk):
        Q = range(ci * nstate, (ci + 1) * nstate)
        ws = [_dot_nt(cat0(z1[p], rt[p]), s[p - Q[0]].astype(BF16)) for p in Q]
        u = [ws[p - Q[0]][:C] + z2[p] for p in Q]
        for p in Q:
            y_out[units[p]] = ws[p - Q[0]][C:] + _dot(cat1(arb[p], ark[p]),
                                                     cat0(bd(u[p - Q[0]]), bdv[p]))
        upd = [_dot_tn(cat0(u[p - Q[0]], v[p]), cat0(b[p] * dend[p], k[p] * dend[p])) for p in Q]
        s = [s[p - Q[0]] * jnp.exp(cwl[p]) + jnp.where(same_head, upd[p - Q[0]], 0.0) for p in Q]
    for q in range(nstate):
        s_scr[q] = s[q]


def _rwkv_scan(r, lw, k, v, kk, b, *, nchunk, nbatch, npairs):
    B, T, D = r.shape
    W = npairs * LANES
    tile = pl.BlockSpec((nbatch, nchunk * CHUNK, W), lambda bi, p, c: (bi, c, p))
    return pl.pallas_call(
        functools.partial(_rwkv_scan_kernel, nchunk=nchunk, nbatch=nbatch, npairs=npairs),
        grid=(B // nbatch, D // W, T // (nchunk * CHUNK)),
        in_specs=[tile] * 6,
        out_specs=tile,
        out_shape=jax.ShapeDtypeStruct((B, T, D), F32),
        scratch_shapes=[pltpu.VMEM((nbatch * npairs, LANES, LANES), F32)],
        compiler_params=_params(("parallel", "parallel", "arbitrary")),
        name="rwkv_scan",
    )(r, lw, k, v, kk, b)


def _mlp_tail(x, g_ref, win_ref, wout_ref, o_ref, tf):
    xn = _rms(x, g_ref[...]).astype(BF16)
    acc = x
    for f in range(win_ref.shape[-1] // tf):
        hid = jnp.maximum(_dot(xn, win_ref[:, f * tf:(f + 1) * tf]), 0.0)
        acc = acc + _dot((hid * hid).astype(BF16), wout_ref[f * tf:(f + 1) * tf, :])
    o_ref[...] = acc


def _proj_mlp_kernel(res_ref, a_ref, wo_ref, g_ref, win_ref, wout_ref, o_ref, *, tf):
    x = res_ref[...] + _dot(a_ref[...], wo_ref[...])
    _mlp_tail(x, g_ref, win_ref, wout_ref, o_ref, tf)


def _rwkv_out_mlp_kernel(res_ref, y_ref, r_ref, k_ref, v_ref, gate_ref, rk_ref, lnw_ref, lnb_ref,
                         wo_ref, g_ref, win_ref, wout_ref, o_ref, a_scr, *, tf):
    i = pl.program_id(0)

    def output_stage():
        y = y_ref[...]
        d = y - _head_sum(y) * (1.0 / HEAD_DIM)
        var = _head_sum(d * d) * (1.0 / HEAD_DIM)
        yn = d * lax.rsqrt(var + GN_EPS)
        r = r_ref[...].astype(F32)
        k = k_ref[...].astype(F32)
        bonus = _head_sum(r * k * rk_ref[...]) * v_ref[...].astype(F32)
        return ((yn * lnw_ref[...] + lnb_ref[...] + bonus) * gate_ref[...].astype(F32)).astype(BF16)

    @pl.when(i == 0)
    def _():
        a_scr[...] = output_stage()

    @pl.when(i > 0)
    def _():
        x = res_ref[...] + _dot(a_scr[...], wo_ref[...])
        _mlp_tail(x, g_ref, win_ref, wout_ref, o_ref, tf)
        a_scr[...] = output_stage()


def _proj_mlp(res, mix, w_o, g, w_in, w_out, *, layer, tm, tf, rwkv=None):
    M, D = res.shape
    tile = pl.BlockSpec((tm, D), lambda i: (i, 0))
    once = lambda shape: pl.BlockSpec(shape, lambda i: (0,) * len(shape), pipeline_mode=pl.Buffered(1))
    of_layer = lambda w: pl.BlockSpec((None,) + w.shape[1:], lambda i: (layer, 0, 0),
                                      pipeline_mode=pl.Buffered(1))
    vec = once((1, D))
    weights = [once(w_o.shape), vec, of_layer(w_in), of_layer(w_out)]
    wargs = (w_o, g.reshape(1, D), w_in, w_out)
    n = M // tm
    if rwkv is None:
        body, grid, specs, args, out_spec, scratch = _proj_mlp_kernel, n, [tile, tile], (res, mix), tile, []
    else:
        r, k, v, gate, rk, lnw, lnb = rwkv
        prev = pl.BlockSpec((tm, D), lambda i: (jnp.maximum(i - 1, 0), 0))
        cur = pl.BlockSpec((tm, D), lambda i: (jnp.minimum(i, n - 1), 0))
        body, grid, specs, out_spec = _rwkv_out_mlp_kernel, n + 1, [prev] + [cur] * 5 + [vec] * 3, prev
        args = (res, mix, r, k, v, gate, rk.reshape(1, D), lnw.reshape(1, D), lnb.reshape(1, D))
        scratch = [pltpu.VMEM((tm, D), BF16)]
    return pl.pallas_call(
        functools.partial(body, tf=tf),
        grid=(grid,),
        in_specs=specs + weights,
        out_specs=out_spec,
        out_shape=jax.ShapeDtypeStruct((M, D), F32),
        scratch_shapes=scratch,
        compiler_params=_params(("arbitrary",)),
        name="proj_mlp",
    )(*args, *wargs)


def _store_q_tiles(qt, c_rows, qg_ref, q_out, *, scale, tq):
    tm = qt.shape[1]
    row = lax.broadcasted_iota(jnp.int32, (HEAD_DIM, tm), 0)
    for h in range(qt.shape[0] // HEAD_DIM):
        hs = slice(h * HEAD_DIM, (h + 1) * HEAD_DIM)
        qh = qt[hs, :]
        ms = jnp.mean(qh * qh, axis=0, keepdims=True)
        qn = qh * lax.rsqrt(ms + NORM_EPS) * (qg_ref[hs, :] * scale)
        hi, mid, lo = _split3(c_rows[h:h + 1, :] * LOG2E)
        aug = jnp.where(row == 0, hi, jnp.where(row == 1, mid, jnp.where(
            row == 2, lo, jnp.where(row < 6, 1.0, 0.0))))
        tile = jnp.concatenate([qn, aug], axis=0).astype(q_out.dtype)
        for sb in range(tm // tq):
            q_out[0, h, sb] = tile[:, sb * tq:(sb + 1) * tq]


def _q_proj_kernel(x_ref, g_ref, wt_ref, qg_ref, c_ref, q_out, *, scale, tq):
    hn = _rms(x_ref[0], g_ref[...])
    qt = _dot_nt(wt_ref[...], hn.astype(BF16))
    _store_q_tiles(qt, c_ref[0], qg_ref, q_out, scale=scale, tq=tq)


def _q_proj(x, g, wt, qg, c_row, *, tm, tq, scale):
    B, T, D = x.shape
    H = D // HEAD_DIM
    return pl.pallas_call(
        functools.partial(_q_proj_kernel, scale=scale, tq=tq),
        grid=(B, T // tm),
        in_specs=[pl.BlockSpec((1, tm, D), lambda b, i: (b, i, 0)), _const_spec((1, D)),
                  _const_spec(wt.shape), _const_spec((D, 1)),
                  pl.BlockSpec((1, H, tm), lambda b, i: (b, 0, i))],
        out_specs=pl.BlockSpec((1, H, tm // tq, LANES, tq), lambda b, i: (b, 0, i, 0, 0)),
        out_shape=jax.ShapeDtypeStruct((B, H, T // tq, LANES, tq), BF16),
        compiler_params=_params(("parallel", "parallel")),
        name="q_proj",
    )(x, g.reshape(1, D), wt, qg.reshape(D, 1), c_row)


def _shared_kv_kernel(x_ref, g_ref, wk_ref, wvt_ref, wf_ref, fb_ref, kg_ref, sel_ref,
                      gq_ref, wqt_ref, qg_ref,
                      k_out, vt_out, c_out, q_out, carry_scr, *, tq, scale):
    i = pl.program_id(1)

    @pl.when(i == 0)
    def _():
        carry_scr[...] = jnp.zeros_like(carry_scr)

    x = x_ref[0]
    xn = x * lax.rsqrt(jnp.mean(x * x, axis=-1, keepdims=True) + NORM_EPS)
    hn = xn * g_ref[...]
    hb = hn.astype(BF16)
    tm, D = hn.shape
    H = D // HEAD_DIM
    k = _head_rms(_dot(hb, wk_ref[...]), kg_ref[...])
    vt = _dot_nt(wvt_ref[...], hb)

    f = _dot(hb, wf_ref[...]) + fb_ref[...]
    logf = jnp.minimum(f, 0.0) - jnp.log(1.0 + jnp.exp(-jnp.abs(f)))
    t_i = lax.broadcasted_iota(jnp.int32, (tm, tm), 0)
    j_i = lax.broadcasted_iota(jnp.int32, (tm, tm), 1)
    ltri = (j_i <= t_i).astype(BF16)
    c3 = _dot(ltri, jnp.concatenate([t.astype(BF16) for t in _split3(logf)], axis=1))
    c = (c3[:, :LANES] + c3[:, LANES:2 * LANES]) + c3[:, 2 * LANES:] + carry_scr[0:1, :]
    carry_scr[...] = jnp.broadcast_to(c[tm - 1:tm, :], carry_scr.shape)
    c_rows = c.T[:H, :]
    c_out[0] = c_rows
    qt = _dot_nt(wqt_ref[...], (xn * gq_ref[...]).astype(BF16))
    _store_q_tiles(qt, c_rows, qg_ref, q_out, scale=scale, tq=tq)

    lane = lax.broadcasted_iota(jnp.int32, (tm, LANES), 1)
    hi, mid, lo = (jnp.where(lane < H, t, 0.0) for t in _split3(c * (-LOG2E)))
    packed = (hi + pltpu.roll(mid, H, 1)) + (pltpu.roll(lo, 2 * H, 1) + jnp.where(lane == 3 * H, 1.0, 0.0))
    aug = _dot(packed.astype(BF16), sel_ref[...])
    vrow = lax.broadcasted_iota(jnp.int32, (V_ROWS - HEAD_DIM, tq), 0)
    ones_row = jnp.where(vrow == 0, 1.0, 0.0)
    for h in range(H):
        base = k[:, (h // 2) * LANES:(h // 2 + 1) * LANES]
        if h % 2:
            base = pltpu.roll(base, HEAD_DIM, 1)
        tile = jnp.where(lane < HEAD_DIM, base, aug[:, h * LANES:(h + 1) * LANES])
        k_out[0, h] = tile.astype(k_out.dtype)
        for sb in range(tm // tq):
            vt_out[0, h, sb] = jnp.concatenate(
                [vt[h * HEAD_DIM:(h + 1) * HEAD_DIM, sb * tq:(sb + 1) * tq], ones_row],
                axis=0).astype(vt_out.dtype)


def _bias_selector(H):
    assert 3 * H + 1 <= LANES
    sel = np.zeros((LANES, H * LANES), np.float32)
    for h in range(H):
        sel[3 * H, h * LANES + HEAD_DIM:h * LANES + HEAD_DIM + 3] = 1.0
        for t in range(3):
            sel[t * H + h, h * LANES + HEAD_DIM + 3 + t] = 1.0
    return jnp.asarray(sel, BF16)


def _shared_kv(x, g, wk, wvt, wf, fb, kg, gq, wqt, qg, *, tm, tq, scale):
    B, T, D = x.shape
    H = D // HEAD_DIM
    sel = _bias_selector(H)
    slabs = lambda rows: pl.BlockSpec((1, H, tm // tq, rows, tq), lambda b, i: (b, 0, i, 0, 0))
    return pl.pallas_call(
        functools.partial(_shared_kv_kernel, tq=tq, scale=scale),
        grid=(B, T // tm),
        in_specs=[pl.BlockSpec((1, tm, D), lambda b, i: (b, i, 0)), _const_spec((1, D)),
                  _const_spec(wk.shape), _const_spec(wvt.shape), _const_spec(wf.shape),
                  _const_spec((1, LANES)), _const_spec((1, D)), _const_spec(sel.shape),
                  _const_spec((1, D)), _const_spec(wqt.shape), _const_spec((D, 1))],
        out_specs=[pl.BlockSpec((1, H, tm, LANES), lambda b, i: (b, 0, i, 0)), slabs(V_ROWS),
                   pl.BlockSpec((1, H, tm), lambda b, i: (b, 0, i)), slabs(LANES)],
        out_shape=[jax.ShapeDtypeStruct((B, H, T, LANES), BF16),
                   jax.ShapeDtypeStruct((B, H, T // tq, V_ROWS, tq), BF16),
                   jax.ShapeDtypeStruct((B, H, T), F32),
                   jax.ShapeDtypeStruct((B, H, T // tq, LANES, tq), BF16)],
        scratch_shapes=[pltpu.VMEM((8, LANES), F32)],
        compiler_params=_params(("parallel", "arbitrary")),
        name="shared_kv",
    )(x, g.reshape(1, D), wk, wvt, wf, fb, kg.reshape(1, D), sel,
      gq.reshape(1, D), wqt, qg.reshape(D, 1))


def _fox_attn_kernel(q_ref, k_ref, vt_ref, o_ref, acc_scr, *, tq, nh):
    i = pl.program_id(2)
    qt = [q_ref[0, h, 0, :QK_DEPTH, :] for h in range(nh)]
    acc_scr[...] = jnp.zeros_like(acc_scr)
    key_i = lax.broadcasted_iota(jnp.int32, (tq, tq), 0)
    qry_i = lax.broadcasted_iota(jnp.int32, (tq, tq), 1)
    causal = key_i <= qry_i

    def step(j, m, masked):
        off = pl.multiple_of(j * tq, tq)
        s, p, alpha, m_new = {}, {}, {}, [None] * nh
        for t in range(nh + 2 * SKEW):
            if t < nh:
                s[t] = _dot(k_ref[0, t, pl.ds(off, tq), :QK_DEPTH], qt[t])
                if masked:
                    s[t] = jnp.where(causal, s[t], NEG_BIG)
            h = t - SKEW
            if 0 <= h < nh:
                m_new[h] = jnp.maximum(m[h], jnp.max(s[h], axis=0, keepdims=True))
                p[h] = jnp.exp2(s.pop(h) - m_new[h]).astype(BF16)
                alpha[h] = jnp.exp2(m[h] - m_new[h])
            h = t - 2 * SKEW
            if 0 <= h < nh:
                acc_scr[h] = alpha[h] * acc_scr[h] + _dot(vt_ref[0, h, j], p.pop(h))
        return tuple(m_new)

    m0 = tuple(jnp.full((1, tq), NEG_BIG, F32) for _ in range(nh))
    m = lax.fori_loop(0, i, functools.partial(step, masked=False), m0)
    step(i, m, True)
    ot = [acc_scr[h, :HEAD_DIM, :] * (1.0 / acc_scr[h, HEAD_DIM:HEAD_DIM + 1, :]) for h in range(nh)]
    o_ref[0] = jnp.concatenate(ot, axis=0).T.astype(o_ref.dtype)


def _fox_attn(qt, ka, vt, *, nh):
    B, H, nb, _, tq = qt.shape
    T = nb * tq
    return pl.pallas_call(
        functools.partial(_fox_attn_kernel, tq=tq, nh=nh),
        grid=(B, H // nh, nb),
        in_specs=[pl.BlockSpec((1, nh, 1, LANES, tq), lambda b, p, i: (b, p, i, 0, 0)),
                  pl.BlockSpec((1, nh, T, LANES), lambda b, p, i: (b, p, 0, 0)),
                  pl.BlockSpec((1, nh, nb, V_ROWS, tq), lambda b, p, i: (b, p, 0, 0, 0))],
        out_specs=pl.BlockSpec((1, tq, nh * HEAD_DIM), lambda b, p, i: (b, i, p)),
        out_shape=jax.ShapeDtypeStruct((B, T, H * HEAD_DIM), BF16),
        scratch_shapes=[pltpu.VMEM((nh, V_ROWS, tq), F32)],
        compiler_params=_params(("parallel", "parallel", "arbitrary")),
        name="fox_attn",
    )(qt, ka, vt)


def kernel(x, rwkv_norm_g, rwkv_mu, rwkv_w_rkv, rwkv_w0, rwkv_w1, rwkv_w2, rwkv_a0, rwkv_a1, rwkv_a2, rwkv_g1, rwkv_g2, rwkv_k_k, rwkv_k_a, rwkv_r_k, rwkv_lnx_w, rwkv_lnx_b, rwkv_w_o, kv_norm_g, kv_w, kv_f_bias, k_norm_g, attn_norm_g, attn_w_q, q_norm_g, attn_w_o, mlp_norm_g, mlp_w_in, mlp_w_out):
    B, T, D = x.shape
    M = B * T
    n_a = rwkv_norm_g.shape[0]
    depth = mlp_norm_g.shape[0]
    bf = lambda w: w.astype(BF16)
    tm = min(512, T)
    tq = min(256, T)
    tf = min(1024, mlp_w_in.shape[-1])
    npairs = D // LANES

    w_in_all, w_out_all = bf(mlp_w_in), bf(mlp_w_out)
    q_scale = HEAD_DIM ** -0.5 * LOG2E
    k_sh = v_sh = c_sh = q_first = None
    for layer in range(depth):
        if layer < n_a:
            i = layer
            r, lw, k, v, kk, b, g = _rwkv_prep(
                x, rwkv_norm_g[i], rwkv_mu[i], bf(rwkv_w_rkv[i, 0]), bf(rwkv_w_rkv[i, 1]),
                bf(rwkv_w_rkv[i, 2]), rwkv_w0[i], bf(rwkv_w1[i]), bf(rwkv_w2[i]), rwkv_a0[i],
                bf(rwkv_a1[i]), bf(rwkv_a2[i]), bf(rwkv_g1[i]), bf(rwkv_g2[i]),
                rwkv_k_k[i], rwkv_k_a[i], tm=tm)
            mix = _rwkv_scan(r, lw, k, v, kk, b, nchunk=4 if T % (4 * CHUNK) == 0 else 1,
                             nbatch=2 if B % 2 == 0 else 1, npairs=npairs)
            flat = lambda t: t.reshape(M, D)
            w_o, extra = rwkv_w_o[i], (flat(r), flat(k), flat(v), flat(g), rwkv_r_k[i].reshape(D),
                                       rwkv_lnx_w[i], rwkv_lnx_b[i])
        else:
            j = layer - n_a
            qt = q_first if j == 0 else _q_proj(x, attn_norm_g[j], bf(attn_w_q[j].T), q_norm_g[j],
                                                 c_sh, tm=tm, tq=tq, scale=q_scale)
            mix = _fox_attn(qt, k_sh, v_sh, nh=min(16, D // HEAD_DIM))
            w_o, extra = attn_w_o[j], None
        x = _proj_mlp(x.reshape(M, D), mix.reshape(M, D), bf(w_o), mlp_norm_g[layer],
                      w_in_all, w_out_all, layer=layer, tm=tm, tf=tf, rwkv=extra).reshape(B, T, D)
        if layer == n_a - 1:
            wf = jnp.pad(kv_w[:, 2 * D:], ((0, 0), (0, LANES - (kv_w.shape[1] - 2 * D))))
            fb = jnp.pad(kv_f_bias, (0, LANES - kv_f_bias.shape[0])).reshape(1, LANES)
            k_sh, v_sh, c_sh, q_first = _shared_kv(
                x, kv_norm_g, bf(kv_w[:, :D]), bf(kv_w[:, D:2 * D].T), bf(wf), fb, k_norm_g,
                attn_norm_g[0], bf(attn_w_q[0].T), q_norm_g[0], tm=tm, tq=tq, scale=q_scale)
    return x
```

```python
import functools

import jax
import jax.numpy as jnp
import numpy as np
from jax import lax
from jax.experimental import pallas as pl
from jax.experimental.pallas import tpu as pltpu

HEAD_DIM = 64
LANES = 128
NORM_EPS = 1e-6
GN_EPS = 64e-5
CHUNK = 64
NEG_BIG = -1e30
LOG2E = 1.4426950408889634
EXP_M_HALF = 0.6065306597126334
V_ROWS = 80
QK_DEPTH = 80
LOOKAHEAD = 1
SKEW = 6
VMEM_LIMIT = 56 * 1024 * 1024

BF16 = jnp.bfloat16
F32 = jnp.float32

_NT = (((1,), (1,)), ((), ()))
_TN = (((0,), (0,)), ((), ()))


def _dot(a, b):
    return jnp.dot(a, b, preferred_element_type=F32)


def _dot_nt(a, b):
    return lax.dot_general(a, b, _NT, preferred_element_type=F32)


def _dot_tn(a, b):
    return lax.dot_general(a, b, _TN, preferred_element_type=F32)


def _split(a):
    hi = a.astype(BF16)
    return hi, (a - hi.astype(F32)).astype(BF16)


def _split3(a):
    hi = a.astype(BF16).astype(F32)
    r1 = a - hi
    mid = r1.astype(BF16).astype(F32)
    return hi, mid, r1 - mid


def _rms(x, g):
    return x * lax.rsqrt(jnp.mean(x * x, axis=-1, keepdims=True) + NORM_EPS) * g


def _head_sum(x):
    outs = []
    for c in range(x.shape[1] // LANES):
        xc = x[:, c * LANES:(c + 1) * LANES]
        lo = lax.broadcasted_iota(jnp.int32, xc.shape, 1) < HEAD_DIM
        s0 = jnp.sum(jnp.where(lo, xc, 0.0), axis=1, keepdims=True)
        s1 = jnp.sum(jnp.where(lo, 0.0, xc), axis=1, keepdims=True)
        outs.append(jnp.where(lo, s0, s1))
    return outs[0] if len(outs) == 1 else jnp.concatenate(outs, axis=1)


def _head_rms(t, g):
    ms = _head_sum(t * t) * (1.0 / HEAD_DIM)
    return t * lax.rsqrt(ms + NORM_EPS) * g


def _sigmoid(z):
    return 1.0 / (1.0 + jnp.exp(-z))


def _const_spec(shape):
    nd = len(shape)
    return pl.BlockSpec(shape, lambda *_: (0,) * nd)


def _params(sem):
    return pltpu.CompilerParams(dimension_semantics=sem, vmem_limit_bytes=VMEM_LIMIT)


def _rwkv_prep_kernel(x_ref, xp_ref, ng_ref, mu_ref, wr_ref, wk_ref, wv_ref,
                      w0_ref, w1_ref, w2_ref, a0_ref, a1_ref, a2_ref, g1_ref, g2_ref,
                      kkw_ref, kaw_ref,
                      r_out, lw_out, k_out, v_out, kk_out, b_out, g_out):
    i = pl.program_id(1)
    ng = ng_ref[...]
    h = _rms(x_ref[0], ng)
    hp = _rms(xp_ref[0][7:8, :], ng)
    hp = jnp.where(i > 0, hp, 0.0)
    rolled = pltpu.roll(h, 1, 0)
    first = jnp.where(lax.broadcasted_iota(jnp.int32, (8, h.shape[1]), 0) == 0, hp, rolled[:8])
    hs = jnp.concatenate([first, rolled[8:]], axis=0)
    hb = h.astype(BF16)
    xxb = (hs - h).astype(BF16)
    mub = mu_ref[...].astype(BF16)

    def mix(j):
        return hb + xxb * mub[j:j + 1, :]

    tw = _dot(mix(1), w1_ref[...])
    ta = _dot(mix(4), a1_ref[...])
    tg = _dot(mix(5), g1_ref[...])
    k = _dot(mix(2), wk_ref[...])
    wl = _dot(jnp.tanh(tw).astype(BF16), w2_ref[...])
    al = _dot(ta.astype(BF16), a2_ref[...])
    g_out[0] = _dot(_sigmoid(tg).astype(BF16), g2_ref[...]).astype(g_out.dtype)

    lw_out[0] = -EXP_M_HALF * _sigmoid(w0_ref[...] + wl)
    a = _sigmoid(a0_ref[...] + al)
    kk = k * kkw_ref[...]
    kk = kk * lax.rsqrt(jnp.maximum(_head_sum(kk * kk), 1e-24))
    k_out[0] = (k * (1.0 + (a - 1.0) * kaw_ref[...])).astype(k_out.dtype)
    kk_out[0] = kk.astype(kk_out.dtype)
    b_out[0] = (kk * a).astype(b_out.dtype)

    r_out[0] = _dot(mix(0), wr_ref[...]).astype(r_out.dtype)
    v_out[0] = _dot(mix(3), wv_ref[...]).astype(v_out.dtype)


def _rwkv_prep(x, ng, mu, wr, wk, wv, w0, w1, w2, a0, a1, a2, g1, g2, kkw, kaw, *, tm):
    B, T, D = x.shape
    row = lambda a: a.reshape(1, D)
    consts = [row(ng), mu, wr, wk, wv, row(w0), w1, w2, row(a0), a1, a2, g1, g2, row(kkw), row(kaw)]
    tile = pl.BlockSpec((1, tm, D), lambda b, i: (b, i, 0))
    prev = pl.BlockSpec((1, 8, D), lambda b, i: (b, jnp.maximum(i * (tm // 8) - 1, 0), 0))
    out = lambda dt: jax.ShapeDtypeStruct((B, T, D), dt)
    return pl.pallas_call(
        _rwkv_prep_kernel,
        grid=(B, T // tm),
        in_specs=[tile, prev] + [_const_spec(c.shape) for c in consts],
        out_specs=[tile] * 7,
        out_shape=[out(BF16), out(F32)] + [out(BF16)] * 5,
        compiler_params=_params(("parallel", "parallel")),
        name="rwkv_prep",
    )(x, x, *consts)


def _blockdiag(z, lo):
    z = z.astype(BF16)
    zero = jnp.zeros_like(z)
    return jnp.concatenate([jnp.where(lo, z, zero), jnp.where(lo, zero, z)], axis=0)


def _rwkv_scan_kernel(r_ref, lw_ref, k_ref, v_ref, kk_ref, b_ref, y_out, s_scr, *,
                      nchunk, nbatch, npairs):
    c = pl.program_id(2)

    @pl.when(c == 0)
    def _():
        s_scr[...] = jnp.zeros_like(s_scr)

    C = CHUNK
    t_i = lax.broadcasted_iota(jnp.int32, (C, C), 0)
    j_i = lax.broadcasted_iota(jnp.int32, (C, C), 1)
    ltri = (j_i <= t_i).astype(BF16)
    row = lax.broadcasted_iota(jnp.int32, (C, LANES), 0)
    lane = lax.broadcasted_iota(jnp.int32, (C, LANES), 1)
    lo = lane < HEAD_DIM
    col = jnp.bitwise_and(lane, HEAD_DIM - 1)
    strict = col < row
    incl = col <= row
    rr = lax.broadcasted_iota(jnp.int32, (LANES, LANES), 0)
    cc = lax.broadcasted_iota(jnp.int32, (LANES, LANES), 1)
    same_head = (rr < HEAD_DIM) == (cc < HEAD_DIM)
    bd = functools.partial(_blockdiag, lo=lo)
    cat0 = lambda *xs: jnp.concatenate([x.astype(BF16) for x in xs], axis=0)
    cat1 = lambda *xs: jnp.concatenate([x.astype(BF16) for x in xs], axis=1)

    nstate = nbatch * npairs
    eye = jnp.where(col == row, 1.0, 0.0)

    def local(ci):
        units = [(bi, slice(ci * C, (ci + 1) * C), slice(p * LANES, (p + 1) * LANES))
                 for bi in range(nbatch) for p in range(npairs)]
        P = range(nstate)
        r = [r_ref[u].astype(F32) for u in units]
        lw = [lw_ref[u] for u in units]
        k = [k_ref[u].astype(F32) for u in units]
        v = [v_ref[u].astype(F32) for u in units]
        kk = [kk_ref[u].astype(F32) for u in units]
        b = [b_ref[u].astype(F32) for u in units]

        cw2 = [_dot(ltri, cat1(*_split(lw[p]))) for p in P]
        cw = [cw2[p][:, :LANES] + cw2[p][:, LANES:] for p in P]
        cwl = [cw[p][C - 1:C, :] for p in P]
        at = [-kk[p] * jnp.exp(cw[p] - lw[p]) for p in P]
        dinv = [jnp.exp(-cw[p]) for p in P]
        rt = [r[p] * jnp.exp(cw[p]) for p in P]
        dend = [jnp.exp(cwl[p] - cw[p]) for p in P]

        x = [_dot_nt(cat0(at[p], rt[p]), cat0(bd(b[p] * dinv[p]), bd(k[p] * dinv[p]))) for p in P]
        aab = [jnp.where(strict, x[p][:C, :LANES], 0.0) for p in P]
        arb = [jnp.where(incl, x[p][C:, :LANES], 0.0) for p in P]
        aak = [jnp.where(strict, x[p][:C, LANES:], 0.0) for p in P]
        ark = [jnp.where(incl, x[p][C:, LANES:], 0.0) for p in P]

        bdv = [bd(v[p]) for p in P]
        av = [_dot(aak[p].astype(BF16), bdv[p]) for p in P]
        tinv = [eye + aab[p] for p in P]
        n = [_dot(aab[p].astype(BF16), bd(aab[p])) for p in P]
        for it in range(5):
            last = it == 4
            res = [_dot(n[p].astype(BF16),
                        jnp.concatenate([bd(tinv[p])] + ([] if last else [bd(n[p])]), axis=1)) for p in P]
            tinv = [tinv[p] + res[p][:, :LANES] for p in P]
            if not last:
                n = [res[p][:, LANES:] for p in P]
        z = [_dot(tinv[p].astype(BF16), jnp.concatenate([bd(at[p]), bd(av[p])], axis=1)) for p in P]
        return dict(units=units, z=z, rt=rt, arb=arb, ark=ark, bdv=bdv, v=v, cwl=cwl,
                    bh=[b[p] * dend[p] for p in P], kh=[k[p] * dend[p] for p in P])

    def advance(L, s):
        P = range(nstate)
        ws = [_dot_nt(cat0(L["z"][p][:, :LANES], L["rt"][p]), s[p].astype(BF16)) for p in P]
        u = [ws[p][:C] + L["z"][p][:, LANES:] for p in P]
        for p in P:
            y_out[L["units"][p]] = ws[p][C:] + _dot(cat1(L["arb"][p], L["ark"][p]),
                                                    cat0(bd(u[p]), L["bdv"][p]))
        upd = [_dot_tn(cat0(u[p], L["v"][p]), cat0(L["bh"][p], L["kh"][p])) for p in P]
        return [s[p] * jnp.exp(L["cwl"][p]) + jnp.where(same_head, upd[p], 0.0) for p in P]

    s = [s_scr[q] for q in range(nstate)]
    pending = {}
    for t in range(nchunk + LOOKAHEAD):
        if t < nchunk:
            pending[t] = local(t)
        if t >= LOOKAHEAD:
            s = advance(pending.pop(t - LOOKAHEAD), s)
    for q in range(nstate):
        s_scr[q] = s[q]


def _rwkv_scan(r, lw, k, v, kk, b, *, nchunk, nbatch, npairs):
    B, T, D = r.shape
    W = npairs * LANES
    tile = pl.BlockSpec((nbatch, nchunk * CHUNK, W), lambda bi, p, c: (bi, c, p))
    return pl.pallas_call(
        functools.partial(_rwkv_scan_kernel, nchunk=nchunk, nbatch=nbatch, npairs=npairs),
        grid=(B // nbatch, D // W, T // (nchunk * CHUNK)),
        in_specs=[tile] * 6,
        out_specs=tile,
        out_shape=jax.ShapeDtypeStruct((B, T, D), F32),
        scratch_shapes=[pltpu.VMEM((nbatch * npairs, LANES, LANES), F32)],
        compiler_params=_params(("parallel", "parallel", "arbitrary")),
        name="rwkv_scan",
    )(r, lw, k, v, kk, b)


def _mlp_tail(x, g_ref, win_ref, wout_ref, o_ref, tf):
    xn = _rms(x, g_ref[...]).astype(BF16)
    acc = x
    for f in range(win_ref.shape[-1] // tf):
        hid = jnp.maximum(_dot(xn, win_ref[:, f * tf:(f + 1) * tf]), 0.0)
        acc = acc + _dot((hid * hid).astype(BF16), wout_ref[f * tf:(f + 1) * tf, :])
    o_ref[...] = acc


def _proj_mlp_kernel(res_ref, a_ref, wo_ref, g_ref, win_ref, wout_ref, o_ref, *, tf):
    x = res_ref[...] + _dot(a_ref[...], wo_ref[...])
    _mlp_tail(x, g_ref, win_ref, wout_ref, o_ref, tf)


def _rwkv_out_mlp_kernel(res_ref, y_ref, r_ref, k_ref, v_ref, gate_ref, rk_ref, lnw_ref, lnb_ref,
                         wo_ref, g_ref, win_ref, wout_ref, o_ref, a_scr, *, tf):
    i = pl.program_id(0)

    def output_stage():
        y = y_ref[...]
        d = y - _head_sum(y) * (1.0 / HEAD_DIM)
        var = _head_sum(d * d) * (1.0 / HEAD_DIM)
        yn = d * lax.rsqrt(var + GN_EPS)
        r = r_ref[...].astype(F32)
        k = k_ref[...].astype(F32)
        bonus = _head_sum(r * k * rk_ref[...]) * v_ref[...].astype(F32)
        return ((yn * lnw_ref[...] + lnb_ref[...] + bonus) * gate_ref[...].astype(F32)).astype(BF16)

    @pl.when(i == 0)
    def _():
        a_scr[...] = output_stage()

    @pl.when(i > 0)
    def _():
        x = res_ref[...] + _dot(a_scr[...], wo_ref[...])
        _mlp_tail(x, g_ref, win_ref, wout_ref, o_ref, tf)
        a_scr[...] = output_stage()


def _proj_mlp(res, mix, w_o, g, w_in, w_out, *, layer, tm, tf, rwkv=None):
    M, D = res.shape
    tile = pl.BlockSpec((tm, D), lambda i: (i, 0))
    once = lambda shape: pl.BlockSpec(shape, lambda i: (0,) * len(shape), pipeline_mode=pl.Buffered(1))
    of_layer = lambda w: pl.BlockSpec((None,) + w.shape[1:], lambda i: (layer, 0, 0),
                                      pipeline_mode=pl.Buffered(1))
    vec = once((1, D))
    weights = [once(w_o.shape), vec, of_layer(w_in), of_layer(w_out)]
    wargs = (w_o, g.reshape(1, D), w_in, w_out)
    n = M // tm
    if rwkv is None:
        body, grid, specs, args, out_spec, scratch = _proj_mlp_kernel, n, [tile, tile], (res, mix), tile, []
    else:
        r, k, v, gate, rk, lnw, lnb = rwkv
        prev = pl.BlockSpec((tm, D), lambda i: (jnp.maximum(i - 1, 0), 0))
        cur = pl.BlockSpec((tm, D), lambda i: (jnp.minimum(i, n - 1), 0))
        body, grid, specs, out_spec = _rwkv_out_mlp_kernel, n + 1, [prev] + [cur] * 5 + [vec] * 3, prev
        args = (res, mix, r, k, v, gate, rk.reshape(1, D), lnw.reshape(1, D), lnb.reshape(1, D))
        scratch = [pltpu.VMEM((tm, D), BF16)]
    return pl.pallas_call(
        functools.partial(body, tf=tf),
        grid=(grid,),
        in_specs=specs + weights,
        out_specs=out_spec,
        out_shape=jax.ShapeDtypeStruct((M, D), F32),
        scratch_shapes=scratch,
        compiler_params=_params(("arbitrary",)),
        name="proj_mlp",
    )(*args, *wargs)


def _store_q_tiles(qt, c_rows, qg_ref, q_out, *, scale, tq):
    tm = qt.shape[1]
    row = lax.broadcasted_iota(jnp.int32, (HEAD_DIM, tm), 0)
    for h in range(qt.shape[0] // HEAD_DIM):
        hs = slice(h * HEAD_DIM, (h + 1) * HEAD_DIM)
        qh = qt[hs, :]
        ms = jnp.mean(qh * qh, axis=0, keepdims=True)
        qn = qh * lax.rsqrt(ms + NORM_EPS) * (qg_ref[hs, :] * scale)
        hi, mid, lo = _split3(c_rows[h:h + 1, :] * LOG2E)
        aug = jnp.where(row == 0, hi, jnp.where(row == 1, mid, jnp.where(
            row == 2, lo, jnp.where(row < 6, 1.0, 0.0))))
        tile = jnp.concatenate([qn, aug], axis=0).astype(q_out.dtype)
        for sb in range(tm // tq):
            q_out[0, h, sb] = tile[:, sb * tq:(sb + 1) * tq]


def _q_proj_kernel(x_ref, g_ref, wt_ref, qg_ref, c_ref, q_out, *, scale, tq):
    hn = _rms(x_ref[0], g_ref[...])
    qt = _dot_nt(wt_ref[...], hn.astype(BF16))
    _store_q_tiles(qt, c_ref[0], qg_ref, q_out, scale=scale, tq=tq)


def _q_proj(x, g, wt, qg, c_row, *, tm, tq, scale):
    B, T, D = x.shape
    H = D // HEAD_DIM
    return pl.pallas_call(
        functools.partial(_q_proj_kernel, scale=scale, tq=tq),
        grid=(B, T // tm),
        in_specs=[pl.BlockSpec((1, tm, D), lambda b, i: (b, i, 0)), _const_spec((1, D)),
                  _const_spec(wt.shape), _const_spec((D, 1)),
                  pl.BlockSpec((1, H, tm), lambda b, i: (b, 0, i))],
        out_specs=pl.BlockSpec((1, H, tm // tq, LANES, tq), lambda b, i: (b, 0, i, 0, 0)),
        out_shape=jax.ShapeDtypeStruct((B, H, T // tq, LANES, tq), BF16),
        compiler_params=_params(("parallel", "parallel")),
        name="q_proj",
    )(x, g.reshape(1, D), wt, qg.reshape(D, 1), c_row)


def _shared_kv_kernel(x_ref, g_ref, wk_ref, wvt_ref, wf_ref, fb_ref, kg_ref, sel_ref,
                      gq_ref, wqt_ref, qg_ref,
                      k_out, vt_out, c_out, q_out, carry_scr, *, tq, scale):
    i = pl.program_id(1)

    @pl.when(i == 0)
    def _():
        carry_scr[...] = jnp.zeros_like(carry_scr)

    x = x_ref[0]
    xn = x * lax.rsqrt(jnp.mean(x * x, axis=-1, keepdims=True) + NORM_EPS)
    hn = xn * g_ref[...]
    hb = hn.astype(BF16)
    tm, D = hn.shape
    H = D // HEAD_DIM
    k = _head_rms(_dot(hb, wk_ref[...]), kg_ref[...])
    vt = _dot_nt(wvt_ref[...], hb)

    f = _dot(hb, wf_ref[...]) + fb_ref[...]
    logf = jnp.minimum(f, 0.0) - jnp.log(1.0 + jnp.exp(-jnp.abs(f)))
    t_i = lax.broadcasted_iota(jnp.int32, (tm, tm), 0)
    j_i = lax.broadcasted_iota(jnp.int32, (tm, tm), 1)
    ltri = (j_i <= t_i).astype(BF16)
    c3 = _dot(ltri, jnp.concatenate([t.astype(BF16) for t in _split3(logf)], axis=1))
    c = (c3[:, :LANES] + c3[:, LANES:2 * LANES]) + c3[:, 2 * LANES:] + carry_scr[0:1, :]
    carry_scr[...] = jnp.broadcast_to(c[tm - 1:tm, :], carry_scr.shape)
    c_rows = c.T[:H, :]
    c_out[0] = c_rows
    qt = _dot_nt(wqt_ref[...], (xn * gq_ref[...]).astype(BF16))
    _store_q_tiles(qt, c_rows, qg_ref, q_out, scale=scale, tq=tq)

    lane = lax.broadcasted_iota(jnp.int32, (tm, LANES), 1)
    hi, mid, lo = (jnp.where(lane < H, t, 0.0) for t in _split3(c * (-LOG2E)))
    packed = (hi + pltpu.roll(mid, H, 1)) + (pltpu.roll(lo, 2 * H, 1) + jnp.where(lane == 3 * H, 1.0, 0.0))
    aug = _dot(packed.astype(BF16), sel_ref[...])
    vrow = lax.broadcasted_iota(jnp.int32, (V_ROWS - HEAD_DIM, tq), 0)
    ones_row = jnp.where(vrow == 0, 1.0, 0.0)
    for h in range(H):
        base = k[:, (h // 2) * LANES:(h // 2 + 1) * LANES]
        if h % 2:
            base = pltpu.roll(base, HEAD_DIM, 1)
        tile = jnp.where(lane < HEAD_DIM, base, aug[:, h * LANES:(h + 1) * LANES])
        k_out[0, h] = tile.astype(k_out.dtype)
        for sb in range(tm // tq):
            vt_out[0, h, sb] = jnp.concatenate(
                [vt[h * HEAD_DIM:(h + 1) * HEAD_DIM, sb * tq:(sb + 1) * tq], ones_row],
                axis=0).astype(vt_out.dtype)


def _bias_selector(H):
    assert 3 * H + 1 <= LANES
    sel = np.zeros((LANES, H * LANES), np.float32)
    for h in range(H):
        sel[3 * H, h * LANES + HEAD_DIM:h * LANES + HEAD_DIM + 3] = 1.0
        for t in range(3):
            sel[t * H + h, h * LANES + HEAD_DIM + 3 + t] = 1.0
    return jnp.asarray(sel, BF16)


def _shared_kv(x, g, wk, wvt, wf, fb, kg, gq, wqt, qg, *, tm, tq, scale):
    B, T, D = x.shape
    H = D // HEAD_DIM
    sel = _bias_selector(H)
    slabs = lambda rows: pl.BlockSpec((1, H, tm // tq, rows, tq), lambda b, i: (b, 0, i, 0, 0))
    return pl.pallas_call(
        functools.partial(_shared_kv_kernel, tq=tq, scale=scale),
        grid=(B, T // tm),
        in_specs=[pl.BlockSpec((1, tm, D), lambda b, i: (b, i, 0)), _const_spec((1, D)),
                  _const_spec(wk.shape), _const_spec(wvt.shape), _const_spec(wf.shape),
                  _const_spec((1, LANES)), _const_spec((1, D)), _const_spec(sel.shape),
                  _const_spec((1, D)), _const_spec(wqt.shape), _const_spec((D, 1))],
        out_specs=[pl.BlockSpec((1, H, tm, LANES), lambda b, i: (b, 0, i, 0)), slabs(V_ROWS),
                   pl.BlockSpec((1, H, tm), lambda b, i: (b, 0, i)), slabs(LANES)],
        out_shape=[jax.ShapeDtypeStruct((B, H, T, LANES), BF16),
                   jax.ShapeDtypeStruct((B, H, T // tq, V_ROWS, tq), BF16),
                   jax.ShapeDtypeStruct((B, H, T), F32),
                   jax.ShapeDtypeStruct((B, H, T // tq, LANES, tq), BF16)],
        scratch_shapes=[pltpu.VMEM((8, LANES), F32)],
        compiler_params=_params(("parallel", "arbitrary")),
        name="shared_kv",
    )(x, g.reshape(1, D), wk, wvt, wf, fb, kg.reshape(1, D), sel,
      gq.reshape(1, D), wqt, qg.reshape(D, 1))


def _fox_attn_kernel(q_ref, k_ref, vt_ref, o_ref, acc_scr, *, tq, nh):
    i = pl.program_id(2)
    qt = [q_ref[0, h, 0, :QK_DEPTH, :] for h in range(nh)]
    acc_scr[...] = jnp.zeros_like(acc_scr)
    key_i = lax.broadcasted_iota(jnp.int32, (tq, tq), 0)
    qry_i = lax.broadcasted_iota(jnp.int32, (tq, tq), 1)
    causal = key_i <= qry_i

    def step(j, m, masked):
        off = pl.multiple_of(j * tq, tq)
        s, p, alpha, m_new = {}, {}, {}, [None] * nh
        for t in range(nh + 2 * SKEW):
            if t < nh:
                s[t] = _dot(k_ref[0, t, pl.ds(off, tq), :QK_DEPTH], qt[t])
                if masked:
                    s[t] = jnp.where(causal, s[t], NEG_BIG)
            h = t - SKEW
            if 0 <= h < nh:
                m_new[h] = jnp.maximum(m[h], jnp.max(s[h], axis=0, keepdims=True))
                p[h] = jnp.exp2(s.pop(h) - m_new[h]).astype(BF16)
                alpha[h] = jnp.exp2(m[h] - m_new[h])
            h = t - 2 * SKEW
            if 0 <= h < nh:
                acc_scr[h] = alpha[h] * acc_scr[h] + _dot(vt_ref[0, h, j], p.pop(h))
        return tuple(m_new)

    m0 = tuple(jnp.full((1, tq), NEG_BIG, F32) for _ in range(nh))
    m = lax.fori_loop(0, i, functools.partial(step, masked=False), m0)
    step(i, m, True)
    ot = [acc_scr[h, :HEAD_DIM, :] * (1.0 / acc_scr[h, HEAD_DIM:HEAD_DIM + 1, :]) for h in range(nh)]
    o_ref[0] = jnp.concatenate(ot, axis=0).T.astype(o_ref.dtype)


def _fox_attn(qt, ka, vt, *, nh):
    B, H, nb, _, tq = qt.shape
    T = nb * tq
    return pl.pallas_call(
        functools.partial(_fox_attn_kernel, tq=tq, nh=nh),
        grid=(B, H // nh, nb),
        in_specs=[pl.BlockSpec((1, nh, 1, LANES, tq), lambda b, p, i: (b, p, i, 0, 0)),
                  pl.BlockSpec((1, nh, T, LANES), lambda b, p, i: (b, p, 0, 0)),
                  pl.BlockSpec((1, nh, nb, V_ROWS, tq), lambda b, p, i: (b, p, 0, 0, 0))],
        out_specs=pl.BlockSpec((1, tq, nh * HEAD_DIM), lambda b, p, i: (b, i, p)),
        out_shape=jax.ShapeDtypeStruct((B, T, H * HEAD_DIM), BF16),
        scratch_shapes=[pltpu.VMEM((nh, V_ROWS, tq), F32)],
        compiler_params=_params(("parallel", "parallel", "arbitrary")),
        name="fox_attn",
    )(qt, ka, vt)


def kernel(x, rwkv_norm_g, rwkv_mu, rwkv_w_rkv, rwkv_w0, rwkv_w1, rwkv_w2, rwkv_a0, rwkv_a1, rwkv_a2, rwkv_g1, rwkv_g2, rwkv_k_k, rwkv_k_a, rwkv_r_k, rwkv_lnx_w, rwkv_lnx_b, rwkv_w_o, kv_norm_g, kv_w, kv_f_bias, k_norm_g, attn_norm_g, attn_w_q, q_norm_g, attn_w_o, mlp_norm_g, mlp_w_in, mlp_w_out):
    B, T, D = x.shape
    M = B * T
    n_a = rwkv_norm_g.shape[0]
    depth = mlp_norm_g.shape[0]
    bf = lambda w: w.astype(BF16)
    tm = min(512, T)
    tq = min(256, T)
    tf = min(1024, mlp_w_in.shape[-1])
    npairs = D // LANES

    w_in_all, w_out_all = bf(mlp_w_in), bf(mlp_w_out)
    q_scale = HEAD_DIM ** -0.5 * LOG2E
    k_sh = v_sh = c_sh = q_first = None
    for layer in range(depth):
        if layer < n_a:
            i = layer
            r, lw, k, v, kk, b, g = _rwkv_prep(
                x, rwkv_norm_g[i], rwkv_mu[i], bf(rwkv_w_rkv[i, 0]), bf(rwkv_w_rkv[i, 1]),
                bf(rwkv_w_rkv[i, 2]), rwkv_w0[i], bf(rwkv_w1[i]), bf(rwkv_w2[i]), rwkv_a0[i],
                bf(rwkv_a1[i]), bf(rwkv_a2[i]), bf(rwkv_g1[i]), bf(rwkv_g2[i]),
                rwkv_k_k[i], rwkv_k_a[i], tm=tm)
            mix = _rwkv_scan(r, lw, k, v, kk, b, nchunk=4 if T % (4 * CHUNK) == 0 else 1,
                             nbatch=2 if B % 2 == 0 else 1, npairs=npairs)
            flat = lambda t: t.reshape(M, D)
            w_o, extra = rwkv_w_o[i], (flat(r), flat(k), flat(v), flat(g), rwkv_r_k[i].reshape(D),
                                       rwkv_lnx_w[i], rwkv_lnx_b[i])
        else:
            j = layer - n_a
            qt = q_first if j == 0 else _q_proj(x, attn_norm_g[j], bf(attn_w_q[j].T), q_norm_g[j],
                                                 c_sh, tm=tm, tq=tq, scale=q_scale)
            mix = _fox_attn(qt, k_sh, v_sh, nh=min(16, D // HEAD_DIM))
            w_o, extra = attn_w_o[j], None
        x = _proj_mlp(x.reshape(M, D), mix.reshape(M, D), bf(w_o), mlp_norm_g[layer],
                      w_in_all, w_out_all, layer=layer, tm=tm, tf=tf, rwkv=extra).reshape(B, T, D)
        if layer == n_a - 1:
            wf = jnp.pad(kv_w[:, 2 * D:], ((0, 0), (0, LANES - (kv_w.shape[1] - 2 * D))))
            fb = jnp.pad(kv_f_bias, (0, LANES - kv_f_bias.shape[0])).reshape(1, LANES)
            k_sh, v_sh, c_sh, q_first = _shared_kv(
                x, kv_norm_g, bf(kv_w[:, :D]), bf(kv_w[:, D:2 * D].T), bf(wf), fb, k_norm_g,
                attn_norm_g[0], bf(attn_w_q[0].T), q_norm_g[0], tm=tm, tq=tq, scale=q_scale)
    return x
```

```python
import functools

import jax
import jax.numpy as jnp
import numpy as np
from jax import lax
from jax.experimental import pallas as pl
from jax.experimental.pallas import tpu as pltpu

HEAD_DIM = 64
LANES = 128
NORM_EPS = 1e-6
GN_EPS = 64e-5
CHUNK = 64
NEG_BIG = -1e30
LOG2E = 1.4426950408889634
EXP_M_HALF = 0.6065306597126334
V_ROWS = 80
QK_DEPTH = 80
SKEW = 6
VMEM_LIMIT = 56 * 1024 * 1024

BF16 = jnp.bfloat16
F32 = jnp.float32

_NT = (((1,), (1,)), ((), ()))
_TN = (((0,), (0,)), ((), ()))


def _dot(a, b):
    return jnp.dot(a, b, preferred_element_type=F32)


def _dot_nt(a, b):
    return lax.dot_general(a, b, _NT, preferred_element_type=F32)


def _dot_tn(a, b):
    return lax.dot_general(a, b, _TN, preferred_element_type=F32)


def _split(a):
    hi = a.astype(BF16)
    return hi, (a - hi.astype(F32)).astype(BF16)


def _split3(a):
    hi = a.astype(BF16).astype(F32)
    r1 = a - hi
    mid = r1.astype(BF16).astype(F32)
    return hi, mid, r1 - mid


def _rms(x, g):
    return x * lax.rsqrt(jnp.mean(x * x, axis=-1, keepdims=True) + NORM_EPS) * g


def _head_sum(x):
    outs = []
    for c in range(x.shape[1] // LANES):
        xc = x[:, c * LANES:(c + 1) * LANES]
        lo = lax.broadcasted_iota(jnp.int32, xc.shape, 1) < HEAD_DIM
        s0 = jnp.sum(jnp.where(lo, xc, 0.0), axis=1, keepdims=True)
        s1 = jnp.sum(jnp.where(lo, 0.0, xc), axis=1, keepdims=True)
        outs.append(jnp.where(lo, s0, s1))
    return outs[0] if len(outs) == 1 else jnp.concatenate(outs, axis=1)


def _head_rms(t, g):
    ms = _head_sum(t * t) * (1.0 / HEAD_DIM)
    return t * lax.rsqrt(ms + NORM_EPS) * g


def _sigmoid(z):
    return 1.0 / (1.0 + jnp.exp(-z))


def _const_spec(shape):
    nd = len(shape)
    return pl.BlockSpec(shape, lambda *_: (0,) * nd)


def _params(sem):
    return pltpu.CompilerParams(dimension_semantics=sem, vmem_limit_bytes=VMEM_LIMIT)


def _rwkv_prep_kernel(x_ref, xp_ref, ng_ref, mu_ref, wr_ref, wk_ref, wv_ref,
                      w0_ref, w1_ref, w2_ref, a0_ref, a1_ref, a2_ref, g1_ref, g2_ref,
                      kkw_ref, kaw_ref,
                      r_out, lw_out, k_out, v_out, kk_out, b_out, g_out):
    i = pl.program_id(1)
    ng = ng_ref[...]
    h = _rms(x_ref[0], ng)
    hp = _rms(xp_ref[0][7:8, :], ng)
    hp = jnp.where(i > 0, hp, 0.0)
    rolled = pltpu.roll(h, 1, 0)
    first = jnp.where(lax.broadcasted_iota(jnp.int32, (8, h.shape[1]), 0) == 0, hp, rolled[:8])
    hs = jnp.concatenate([first, rolled[8:]], axis=0)
    hb = h.astype(BF16)
    xxb = (hs - h).astype(BF16)
    mub = mu_ref[...].astype(BF16)

    def mix(j):
        return hb + xxb * mub[j:j + 1, :]

    tw = _dot(mix(1), w1_ref[...])
    ta = _dot(mix(4), a1_ref[...])
    tg = _dot(mix(5), g1_ref[...])
    k = _dot(mix(2), wk_ref[...])
    wl = _dot(jnp.tanh(tw).astype(BF16), w2_ref[...])
    al = _dot(ta.astype(BF16), a2_ref[...])
    g_out[0] = _dot(_sigmoid(tg).astype(BF16), g2_ref[...]).astype(g_out.dtype)

    lw_out[0] = -EXP_M_HALF * _sigmoid(w0_ref[...] + wl)
    a = _sigmoid(a0_ref[...] + al)
    kk = k * kkw_ref[...]
    kk = kk * lax.rsqrt(jnp.maximum(_head_sum(kk * kk), 1e-24))
    k_out[0] = (k * (1.0 + (a - 1.0) * kaw_ref[...])).astype(k_out.dtype)
    kk_out[0] = kk.astype(kk_out.dtype)
    b_out[0] = (kk * a).astype(b_out.dtype)

    r_out[0] = _dot(mix(0), wr_ref[...]).astype(r_out.dtype)
    v_out[0] = _dot(mix(3), wv_ref[...]).astype(v_out.dtype)


def _rwkv_prep(x, ng, mu, wr, wk, wv, w0, w1, w2, a0, a1, a2, g1, g2, kkw, kaw, *, tm):
    B, T, D = x.shape
    row = lambda a: a.reshape(1, D)
    consts = [row(ng), mu, wr, wk, wv, row(w0), w1, w2, row(a0), a1, a2, g1, g2, row(kkw), row(kaw)]
    tile = pl.BlockSpec((1, tm, D), lambda b, i: (b, i, 0))
    prev = pl.BlockSpec((1, 8, D), lambda b, i: (b, jnp.maximum(i * (tm // 8) - 1, 0), 0))
    out = lambda dt: jax.ShapeDtypeStruct((B, T, D), dt)
    return pl.pallas_call(
        _rwkv_prep_kernel,
        grid=(B, T // tm),
        in_specs=[tile, prev] + [_const_spec(c.shape) for c in consts],
        out_specs=[tile] * 7,
        out_shape=[out(BF16), out(F32)] + [out(BF16)] * 5,
        compiler_params=_params(("parallel", "parallel")),
        name="rwkv_prep",
    )(x, x, *consts)


def _blockdiag(z, lo):
    z = z.astype(BF16)
    zero = jnp.zeros_like(z)
    return jnp.concatenate([jnp.where(lo, z, zero), jnp.where(lo, zero, z)], axis=0)


def _rwkv_scan_kernel(r_ref, lw_ref, k_ref, v_ref, kk_ref, b_ref, y_out, s_scr, *,
                      nchunk, nbatch, npairs):
    c = pl.program_id(2)

    @pl.when(c == 0)
    def _():
        s_scr[...] = jnp.zeros_like(s_scr)

    C = CHUNK
    t_i = lax.broadcasted_iota(jnp.int32, (C, C), 0)
    j_i = lax.broadcasted_iota(jnp.int32, (C, C), 1)
    ltri = (j_i <= t_i).astype(BF16)
    row = lax.broadcasted_iota(jnp.int32, (C, LANES), 0)
    lane = lax.broadcasted_iota(jnp.int32, (C, LANES), 1)
    lo = lane < HEAD_DIM
    col = jnp.bitwise_and(lane, HEAD_DIM - 1)
    strict = col < row
    incl = col <= row
    rr = lax.broadcasted_iota(jnp.int32, (LANES, LANES), 0)
    cc = lax.broadcasted_iota(jnp.int32, (LANES, LANES), 1)
    same_head = (rr < HEAD_DIM) == (cc < HEAD_DIM)
    bd = functools.partial(_blockdiag, lo=lo)
    cat0 = lambda *xs: jnp.concatenate([x.astype(BF16) for x in xs], axis=0)
    cat1 = lambda *xs: jnp.concatenate([x.astype(BF16) for x in xs], axis=1)

    nstate = nbatch * npairs
    units = [(bi, slice(ci * C, (ci + 1) * C), slice(p * LANES, (p + 1) * LANES))
             for ci in range(nchunk) for bi in range(nbatch) for p in range(npairs)]
    P = range(len(units))
    r = [r_ref[u].astype(F32) for u in units]
    lw = [lw_ref[u] for u in units]
    k = [k_ref[u].astype(F32) for u in units]
    v = [v_ref[u].astype(F32) for u in units]
    kk = [kk_ref[u].astype(F32) for u in units]
    b = [b_ref[u].astype(F32) for u in units]

    cw2 = [_dot(ltri, cat1(*_split(lw[p]))) for p in P]
    cw = [cw2[p][:, :LANES] + cw2[p][:, LANES:] for p in P]
    cwl = [cw[p][C - 1:C, :] for p in P]
    at = [-kk[p] * jnp.exp(cw[p] - lw[p]) for p in P]
    dinv = [jnp.exp(-cw[p]) for p in P]
    rt = [r[p] * jnp.exp(cw[p]) for p in P]
    dend = [jnp.exp(cwl[p] - cw[p]) for p in P]

    x = [_dot_nt(cat0(at[p], rt[p]), cat0(bd(b[p] * dinv[p]), bd(k[p] * dinv[p]))) for p in P]
    aab = [jnp.where(strict, x[p][:C, :LANES], 0.0) for p in P]
    arb = [jnp.where(incl, x[p][C:, :LANES], 0.0) for p in P]
    aak = [jnp.where(strict, x[p][:C, LANES:], 0.0) for p in P]
    ark = [jnp.where(incl, x[p][C:, LANES:], 0.0) for p in P]

    bdv = [bd(v[p]) for p in P]
    av = [_dot(aak[p].astype(BF16), bdv[p]) for p in P]
    eye = jnp.where(col == row, 1.0, 0.0)
    tinv = [eye + aab[p] for p in P]
    n = [_dot(aab[p].astype(BF16), bd(aab[p])) for p in P]
    for it in range(5):
        last = it == 4
        res = [_dot(n[p].astype(BF16),
                    jnp.concatenate([bd(tinv[p])] + ([] if last else [bd(n[p])]), axis=1)) for p in P]
        tinv = [tinv[p] + res[p][:, :LANES] for p in P]
        if not last:
            n = [res[p][:, LANES:] for p in P]
    z = [_dot(tinv[p].astype(BF16), jnp.concatenate([bd(at[p]), bd(av[p])], axis=1)) for p in P]
    z1 = [z[p][:, :LANES] for p in P]
    z2 = [z[p][:, LANES:] for p in P]

    s = [s_scr[q] for q in range(nstate)]
    for ci in range(nchunk):
        Q = range(ci * nstate, (ci + 1) * nstate)
        ws = [_dot_nt(cat0(z1[p], rt[p]), s[p - Q[0]].astype(BF16)) for p in Q]
        u = [ws[p - Q[0]][:C] + z2[p] for p in Q]
        for p in Q:
            y_out[units[p]] = ws[p - Q[0]][C:] + _dot(cat1(arb[p], ark[p]),
                                                     cat0(bd(u[p - Q[0]]), bdv[p]))
        upd = [_dot_tn(cat0(u[p - Q[0]], v[p]), cat0(b[p] * dend[p], k[p] * dend[p])) for p in Q]
        s = [s[p - Q[0]] * jnp.exp(cwl[p]) + jnp.where(same_head, upd[p - Q[0]], 0.0) for p in Q]
    for q in range(nstate):
        s_scr[q] = s[q]


def _rwkv_scan(r, lw, k, v, kk, b, *, nchunk, nbatch, npairs):
    B, T, D = r.shape
    W = npairs * LANES
    tile = pl.BlockSpec((nbatch, nchunk * CHUNK, W), lambda bi, p, c: (bi, c, p))
    return pl.pallas_call(
        functools.partial(_rwkv_scan_kernel, nchunk=nchunk, nbatch=nbatch, npairs=npairs),
        grid=(B // nbatch, D // W, T // (nchunk * CHUNK)),
        in_specs=[tile] * 6,
        out_specs=tile,
        out_shape=jax.ShapeDtypeStruct((B, T, D), F32),
        scratch_shapes=[pltpu.VMEM((nbatch * npairs, LANES, LANES), F32)],
        compiler_params=_params(("parallel", "parallel", "arbitrary")),
        name="rwkv_scan",
    )(r, lw, k, v, kk, b)


def _mlp_tail(x, g_ref, win_ref, wout_ref, o_ref, tf, side_work=None):
    xn = _rms(x, g_ref[...]).astype(BF16)
    acc = x
    nf = win_ref.shape[-1] // tf
    for f in range(nf):
        hid = jnp.maximum(_dot(xn, win_ref[:, f * tf:(f + 1) * tf]), 0.0)
        if side_work is not None:
            side_work(f, nf)
        acc = acc + _dot((hid * hid).astype(BF16), wout_ref[f * tf:(f + 1) * tf, :])
    o_ref[...] = acc


def _proj_mlp_kernel(res_ref, a_ref, wo_ref, g_ref, win_ref, wout_ref, o_ref, *, tf):
    x = res_ref[...] + _dot(a_ref[...], wo_ref[...])
    _mlp_tail(x, g_ref, win_ref, wout_ref, o_ref, tf)


def _rwkv_out_mlp_kernel(res_ref, y_ref, r_ref, k_ref, v_ref, gate_ref, rk_ref, lnw_ref, lnb_ref,
                         wo_ref, g_ref, win_ref, wout_ref, o_ref, a_scr, *, tf):
    i = pl.program_id(0)

    tm = y_ref.shape[0]

    def output_stage(f=0, n=1):
        rows = slice(f * (tm // n), (f + 1) * (tm // n))
        y = y_ref[rows, :]
        d = y - _head_sum(y) * (1.0 / HEAD_DIM)
        var = _head_sum(d * d) * (1.0 / HEAD_DIM)
        yn = d * lax.rsqrt(var + GN_EPS)
        r = r_ref[rows, :].astype(F32)
        k = k_ref[rows, :].astype(F32)
        bonus = _head_sum(r * k * rk_ref[...]) * v_ref[rows, :].astype(F32)
        a = (yn * lnw_ref[...] + lnb_ref[...] + bonus) * gate_ref[rows, :].astype(F32)
        a_scr[rows, :] = a.astype(a_scr.dtype)

    @pl.when(i == 0)
    def _():
        output_stage()

    @pl.when(i > 0)
    def _():
        x = res_ref[...] + _dot(a_scr[...], wo_ref[...])
        _mlp_tail(x, g_ref, win_ref, wout_ref, o_ref, tf, side_work=output_stage)


def _proj_mlp(res, mix, w_o, g, w_in, w_out, *, layer, tm, tf, rwkv=None):
    M, D = res.shape
    tile = pl.BlockSpec((tm, D), lambda i: (i, 0))
    once = lambda shape: pl.BlockSpec(shape, lambda i: (0,) * len(shape), pipeline_mode=pl.Buffered(1))
    of_layer = lambda w: pl.BlockSpec((None,) + w.shape[1:], lambda i: (layer, 0, 0),
                                      pipeline_mode=pl.Buffered(1))
    vec = once((1, D))
    weights = [once(w_o.shape), vec, of_layer(w_in), of_layer(w_out)]
    wargs = (w_o, g.reshape(1, D), w_in, w_out)
    n = M // tm
    if rwkv is None:
        body, grid, specs, args, out_spec, scratch = _proj_mlp_kernel, n, [tile, tile], (res, mix), tile, []
    else:
        r, k, v, gate, rk, lnw, lnb = rwkv
        prev = pl.BlockSpec((tm, D), lambda i: (jnp.maximum(i - 1, 0), 0))
        cur = pl.BlockSpec((tm, D), lambda i: (jnp.minimum(i, n - 1), 0))
        body, grid, specs, out_spec = _rwkv_out_mlp_kernel, n + 1, [prev] + [cur] * 5 + [vec] * 3, prev
        args = (res, mix, r, k, v, gate, rk.reshape(1, D), lnw.reshape(1, D), lnb.reshape(1, D))
        scratch = [pltpu.VMEM((tm, D), BF16)]
    return pl.pallas_call(
        functools.partial(body, tf=tf),
        grid=(grid,),
        in_specs=specs + weights,
        out_specs=out_spec,
        out_shape=jax.ShapeDtypeStruct((M, D), F32),
        scratch_shapes=scratch,
        compiler_params=_params(("arbitrary",)),
        name="proj_mlp",
    )(*args, *wargs)


def _store_q_tiles(qt, c_rows, qg_ref, q_out, *, scale, tq):
    tm = qt.shape[1]
    row = lax.broadcasted_iota(jnp.int32, (HEAD_DIM, tm), 0)
    for h in range(qt.shape[0] // HEAD_DIM):
        hs = slice(h * HEAD_DIM, (h + 1) * HEAD_DIM)
        qh = qt[hs, :]
        ms = jnp.mean(qh * qh, axis=0, keepdims=True)
        qn = qh * lax.rsqrt(ms + NORM_EPS) * (qg_ref[hs, :] * scale)
        hi, mid, lo = _split3(c_rows[h:h + 1, :] * LOG2E)
        aug = jnp.where(row == 0, hi, jnp.where(row == 1, mid, jnp.where(
            row == 2, lo, jnp.where(row < 6, 1.0, 0.0))))
        tile = jnp.concatenate([qn, aug], axis=0).astype(q_out.dtype)
        for sb in range(tm // tq):
            q_out[0, h, sb] = tile[:, sb * tq:(sb + 1) * tq]


def _q_proj_kernel(x_ref, g_ref, wt_ref, qg_ref, c_ref, q_out, *, scale, tq):
    hn = _rms(x_ref[0], g_ref[...])
    qt = _dot_nt(wt_ref[...], hn.astype(BF16))
    _store_q_tiles(qt, c_ref[0], qg_ref, q_out, scale=scale, tq=tq)


def _q_proj(x, g, wt, qg, c_row, *, tm, tq, scale):
    B, T, D = x.shape
    H = D // HEAD_DIM
    return pl.pallas_call(
        functools.partial(_q_proj_kernel, scale=scale, tq=tq),
        grid=(B, T // tm),
        in_specs=[pl.BlockSpec((1, tm, D), lambda b, i: (b, i, 0)), _const_spec((1, D)),
                  _const_spec(wt.shape), _const_spec((D, 1)),
                  pl.BlockSpec((1, H, tm), lambda b, i: (b, 0, i))],
        out_specs=pl.BlockSpec((1, H, tm // tq, LANES, tq), lambda b, i: (b, 0, i, 0, 0)),
        out_shape=jax.ShapeDtypeStruct((B, H, T // tq, LANES, tq), BF16),
        compiler_params=_params(("parallel", "parallel")),
        name="q_proj",
    )(x, g.reshape(1, D), wt, qg.reshape(D, 1), c_row)


def _shared_kv_kernel(x_ref, g_ref, wk_ref, wvt_ref, wf_ref, fb_ref, kg_ref, sel_ref,
                      gq_ref, wqt_ref, qg_ref,
                      k_out, vt_out, c_out, q_out, carry_scr, *, tq, scale):
    i = pl.program_id(1)

    @pl.when(i == 0)
    def _():
        carry_scr[...] = jnp.zeros_like(carry_scr)

    x = x_ref[0]
    xn = x * lax.rsqrt(jnp.mean(x * x, axis=-1, keepdims=True) + NORM_EPS)
    hn = xn * g_ref[...]
    hb = hn.astype(BF16)
    tm, D = hn.shape
    H = D // HEAD_DIM
    k = _head_rms(_dot(hb, wk_ref[...]), kg_ref[...])
    vt = _dot_nt(wvt_ref[...], hb)

    f = _dot(hb, wf_ref[...]) + fb_ref[...]
    logf = jnp.minimum(f, 0.0) - jnp.log(1.0 + jnp.exp(-jnp.abs(f)))
    t_i = lax.broadcasted_iota(jnp.int32, (tm, tm), 0)
    j_i = lax.broadcasted_iota(jnp.int32, (tm, tm), 1)
    ltri = (j_i <= t_i).astype(BF16)
    c3 = _dot(ltri, jnp.concatenate([t.astype(BF16) for t in _split3(logf)], axis=1))
    c = (c3[:, :LANES] + c3[:, LANES:2 * LANES]) + c3[:, 2 * LANES:] + carry_scr[0:1, :]
    carry_scr[...] = jnp.broadcast_to(c[tm - 1:tm, :], carry_scr.shape)
    c_rows = c.T[:H, :]
    c_out[0] = c_rows
    qt = _dot_nt(wqt_ref[...], (xn * gq_ref[...]).astype(BF16))
    _store_q_tiles(qt, c_rows, qg_ref, q_out, scale=scale, tq=tq)

    lane = lax.broadcasted_iota(jnp.int32, (tm, LANES), 1)
    hi, mid, lo = (jnp.where(lane < H, t, 0.0) for t in _split3(c * (-LOG2E)))
    packed = (hi + pltpu.roll(mid, H, 1)) + (pltpu.roll(lo, 2 * H, 1) + jnp.where(lane == 3 * H, 1.0, 0.0))
    aug = _dot(packed.astype(BF16), sel_ref[...])
    vrow = lax.broadcasted_iota(jnp.int32, (V_ROWS - HEAD_DIM, tq), 0)
    ones_row = jnp.where(vrow == 0, 1.0, 0.0)
    for h in range(H):
        base = k[:, (h // 2) * LANES:(h // 2 + 1) * LANES]
        if h % 2:
            base = pltpu.roll(base, HEAD_DIM, 1)
        tile = jnp.where(lane < HEAD_DIM, base, aug[:, h * LANES:(h + 1) * LANES])
        k_out[0, h] = tile.astype(k_out.dtype)
        for sb in range(tm // tq):
            vt_out[0, h, sb] = jnp.concatenate(
                [vt[h * HEAD_DIM:(h + 1) * HEAD_DIM, sb * tq:(sb + 1) * tq], ones_row],
                axis=0).astype(vt_out.dtype)


def _bias_selector(H):
    assert 3 * H + 1 <= LANES
    sel = np.zeros((LANES, H * LANES), np.float32)
    for h in range(H):
        sel[3 * H, h * LANES + HEAD_DIM:h * LANES + HEAD_DIM + 3] = 1.0
        for t in range(3):
            sel[t * H + h, h * LANES + HEAD_DIM + 3 + t] = 1.0
    return jnp.asarray(sel, BF16)


def _shared_kv(x, g, wk, wvt, wf, fb, kg, gq, wqt, qg, *, tm, tq, scale):
    B, T, D = x.shape
    H = D // HEAD_DIM
    sel = _bias_selector(H)
    slabs = lambda rows: pl.BlockSpec((1, H, tm // tq, rows, tq), lambda b, i: (b, 0, i, 0, 0))
    return pl.pallas_call(
        functools.partial(_shared_kv_kernel, tq=tq, scale=scale),
        grid=(B, T // tm),
        in_specs=[pl.BlockSpec((1, tm, D), lambda b, i: (b, i, 0)), _const_spec((1, D)),
                  _const_spec(wk.shape), _const_spec(wvt.shape), _const_spec(wf.shape),
                  _const_spec((1, LANES)), _const_spec((1, D)), _const_spec(sel.shape),
                  _const_spec((1, D)), _const_spec(wqt.shape), _const_spec((D, 1))],
        out_specs=[pl.BlockSpec((1, H, tm, LANES), lambda b, i: (b, 0, i, 0)), slabs(V_ROWS),
                   pl.BlockSpec((1, H, tm), lambda b, i: (b, 0, i)), slabs(LANES)],
        out_shape=[jax.ShapeDtypeStruct((B, H, T, LANES), BF16),
                   jax.ShapeDtypeStruct((B, H, T // tq, V_ROWS, tq), BF16),
                   jax.ShapeDtypeStruct((B, H, T), F32),
                   jax.ShapeDtypeStruct((B, H, T // tq, LANES, tq), BF16)],
        scratch_shapes=[pltpu.VMEM((8, LANES), F32)],
        compiler_params=_params(("parallel", "arbitrary")),
        name="shared_kv",
    )(x, g.reshape(1, D), wk, wvt, wf, fb, kg.reshape(1, D), sel,
      gq.reshape(1, D), wqt, qg.reshape(D, 1))


def _fox_attn_kernel(q_ref, k_ref, vt_ref, o_ref, acc_scr, *, tq, nh):
    i = pl.program_id(2)
    qt = [q_ref[0, h, 0, :QK_DEPTH, :] for h in range(nh)]
    acc_scr[...] = jnp.zeros_like(acc_scr)
    key_i = lax.broadcasted_iota(jnp.int32, (tq, tq), 0)
    qry_i = lax.broadcasted_iota(jnp.int32, (tq, tq), 1)
    causal = key_i <= qry_i

    def step(j, m, masked):
        off = pl.multiple_of(j * tq, tq)
        s, p, alpha, m_new = {}, {}, {}, [None] * nh
        for t in range(nh + 2 * SKEW):
            if t < nh:
                s[t] = _dot(k_ref[0, t, pl.ds(off, tq), :QK_DEPTH], qt[t])
                if masked:
                    s[t] = jnp.where(causal, s[t], NEG_BIG)
            h = t - SKEW
            if 0 <= h < nh:
                m_new[h] = jnp.maximum(m[h], jnp.max(s[h], axis=0, keepdims=True))
                p[h] = jnp.exp2(s.pop(h) - m_new[h]).astype(BF16)
                alpha[h] = jnp.exp2(m[h] - m_new[h])
            h = t - 2 * SKEW
            if 0 <= h < nh:
                acc_scr[h] = alpha[h] * acc_scr[h] + _dot(vt_ref[0, h, j], p.pop(h))
        return tuple(m_new)

    m0 = tuple(jnp.full((1, tq), NEG_BIG, F32) for _ in range(nh))
    m = lax.fori_loop(0, i, functools.partial(step, masked=False), m0)
    step(i, m, True)
    ot = [acc_scr[h, :HEAD_DIM, :] * (1.0 / acc_scr[h, HEAD_DIM:HEAD_DIM + 1, :]) for h in range(nh)]
    o_ref[0] = jnp.concatenate(ot, axis=0).T.astype(o_ref.dtype)


def _fox_attn(qt, ka, vt, *, nh):
    B, H, nb, _, tq = qt.shape
    T = nb * tq
    return pl.pallas_call(
        functools.partial(_fox_attn_kernel, tq=tq, nh=nh),
        grid=(B, H // nh, nb),
        in_specs=[pl.BlockSpec((1, nh, 1, LANES, tq), lambda b, p, i: (b, p, i, 0, 0)),
                  pl.BlockSpec((1, nh, T, LANES), lambda b, p, i: (b, p, 0, 0)),
                  pl.BlockSpec((1, nh, nb, V_ROWS, tq), lambda b, p, i: (b, p, 0, 0, 0))],
        out_specs=pl.BlockSpec((1, tq, nh * HEAD_DIM), lambda b, p, i: (b, i, p)),
        out_shape=jax.ShapeDtypeStruct((B, T, H * HEAD_DIM), BF16),
        scratch_shapes=[pltpu.VMEM((nh, V_ROWS, tq), F32)],
        compiler_params=_params(("parallel", "parallel", "arbitrary")),
        name="fox_attn",
    )(qt, ka, vt)


def kernel(x, rwkv_norm_g, rwkv_mu, rwkv_w_rkv, rwkv_w0, rwkv_w1, rwkv_w2, rwkv_a0, rwkv_a1, rwkv_a2, rwkv_g1, rwkv_g2, rwkv_k_k, rwkv_k_a, rwkv_r_k, rwkv_lnx_w, rwkv_lnx_b, rwkv_w_o, kv_norm_g, kv_w, kv_f_bias, k_norm_g, attn_norm_g, attn_w_q, q_norm_g, attn_w_o, mlp_norm_g, mlp_w_in, mlp_w_out):
    B, T, D = x.shape
    M = B * T
    n_a = rwkv_norm_g.shape[0]
    depth = mlp_norm_g.shape[0]
    bf = lambda w: w.astype(BF16)
    tm = min(512, T)
    tq = min(256, T)
    tf = min(1024, mlp_w_in.shape[-1])
    npairs = D // LANES

    w_in_all, w_out_all = bf(mlp_w_in), bf(mlp_w_out)
    q_scale = HEAD_DIM ** -0.5 * LOG2E
    k_sh = v_sh = c_sh = q_first = None
    for layer in range(depth):
        if layer < n_a:
            i = layer
            r, lw, k, v, kk, b, g = _rwkv_prep(
                x, rwkv_norm_g[i], rwkv_mu[i], bf(rwkv_w_rkv[i, 0]), bf(rwkv_w_rkv[i, 1]),
                bf(rwkv_w_rkv[i, 2]), rwkv_w0[i], bf(rwkv_w1[i]), bf(rwkv_w2[i]), rwkv_a0[i],
                bf(rwkv_a1[i]), bf(rwkv_a2[i]), bf(rwkv_g1[i]), bf(rwkv_g2[i]),
                rwkv_k_k[i], rwkv_k_a[i], tm=tm)
            mix = _rwkv_scan(r, lw, k, v, kk, b, nchunk=4 if T % (4 * CHUNK) == 0 else 1,
                             nbatch=2 if B % 2 == 0 else 1, npairs=npairs)
            flat = lambda t: t.reshape(M, D)
            w_o, extra = rwkv_w_o[i], (flat(r), flat(k), flat(v), flat(g), rwkv_r_k[i].reshape(D),
                                       rwkv_lnx_w[i], rwkv_lnx_b[i])
        else:
            j = layer - n_a
            qt = q_first if j == 0 else _q_proj(x, attn_norm_g[j], bf(attn_w_q[j].T), q_norm_g[j],
                                                 c_sh, tm=tm, tq=tq, scale=q_scale)
            mix = _fox_attn(qt, k_sh, v_sh, nh=min(16, D // HEAD_DIM))
            w_o, extra = attn_w_o[j], None
        x = _proj_mlp(x.reshape(M, D), mix.reshape(M, D), bf(w_o), mlp_norm_g[layer],
                      w_in_all, w_out_all, layer=layer, tm=tm, tf=tf, rwkv=extra).reshape(B, T, D)
        if layer == n_a - 1:
            wf = jnp.pad(kv_w[:, 2 * D:], ((0, 0), (0, LANES - (kv_w.shape[1] - 2 * D))))
            fb = jnp.pad(kv_f_bias, (0, LANES - kv_f_bias.shape[0])).reshape(1, LANES)
            k_sh, v_sh, c_sh, q_first = _shared_kv(
                x, kv_norm_g, bf(kv_w[:, :D]), bf(kv_w[:, D:2 * D].T), bf(wf), fb, k_norm_g,
                attn_norm_g[0], bf(attn_w_q[0].T), q_norm_g[0], tm=tm, tq=tq, scale=q_scale)
    return x
```

```python
import functools

import jax
import jax.numpy as jnp
import numpy as np
from jax import lax
from jax.experimental import pallas as pl
from jax.experimental.pallas import tpu as pltpu

HEAD_DIM = 64
LANES = 128
NORM_EPS = 1e-6
GN_EPS = 64e-5
CHUNK = 64
NEG_BIG = -1e30
LOG2E = 1.4426950408889634
EXP_M_HALF = 0.6065306597126334
V_ROWS = 80
QK_DEPTH = 80
SKEW = 6
VMEM_LIMIT = 56 * 1024 * 1024

BF16 = jnp.bfloat16
F32 = jnp.float32

_NT = (((1,), (1,)), ((), ()))
_TN = (((0,), (0,)), ((), ()))


def _dot(a, b):
    return jnp.dot(a, b, preferred_element_type=F32)


def _dot_nt(a, b):
    return lax.dot_general(a, b, _NT, preferred_element_type=F32)


def _dot_tn(a, b):
    return lax.dot_general(a, b, _TN, preferred_element_type=F32)


def _split(a):
    hi = a.astype(BF16)
    return hi, (a - hi.astype(F32)).astype(BF16)


def _split3(a):
    hi = a.astype(BF16).astype(F32)
    r1 = a - hi
    mid = r1.astype(BF16).astype(F32)
    return hi, mid, r1 - mid


def _rms(x, g):
    return x * lax.rsqrt(jnp.mean(x * x, axis=-1, keepdims=True) + NORM_EPS) * g


def _head_sum(x):
    outs = []
    for c in range(x.shape[1] // LANES):
        xc = x[:, c * LANES:(c + 1) * LANES]
        lo = lax.broadcasted_iota(jnp.int32, xc.shape, 1) < HEAD_DIM
        s0 = jnp.sum(jnp.where(lo, xc, 0.0), axis=1, keepdims=True)
        s1 = jnp.sum(jnp.where(lo, 0.0, xc), axis=1, keepdims=True)
        outs.append(jnp.where(lo, s0, s1))
    return outs[0] if len(outs) == 1 else jnp.concatenate(outs, axis=1)


def _head_rms(t, g):
    ms = _head_sum(t * t) * (1.0 / HEAD_DIM)
    return t * lax.rsqrt(ms + NORM_EPS) * g


def _sigmoid(z):
    return 1.0 / (1.0 + jnp.exp(-z))


def _const_spec(shape):
    nd = len(shape)
    return pl.BlockSpec(shape, lambda *_: (0,) * nd)


def _params(sem):
    return pltpu.CompilerParams(dimension_semantics=sem, vmem_limit_bytes=VMEM_LIMIT)


def _rwkv_prep_kernel(x_ref, xp_ref, ng_ref, mu_ref, wr_ref, wk_ref, wv_ref,
                      w0_ref, w1_ref, w2_ref, a0_ref, a1_ref, a2_ref, g1_ref, g2_ref,
                      kkw_ref, kaw_ref,
                      r_out, lw_out, k_out, v_out, kk_out, b_out, g_out):
    i = pl.program_id(1)
    ng = ng_ref[...]
    h = _rms(x_ref[0], ng)
    hp = _rms(xp_ref[0][7:8, :], ng)
    hp = jnp.where(i > 0, hp, 0.0)
    rolled = pltpu.roll(h, 1, 0)
    first = jnp.where(lax.broadcasted_iota(jnp.int32, (8, h.shape[1]), 0) == 0, hp, rolled[:8])
    hs = jnp.concatenate([first, rolled[8:]], axis=0)
    hb = h.astype(BF16)
    xxb = (hs - h).astype(BF16)
    mub = mu_ref[...].astype(BF16)

    def mix(j):
        return hb + xxb * mub[j:j + 1, :]

    tw = _dot(mix(1), w1_ref[...])
    ta = _dot(mix(4), a1_ref[...])
    tg = _dot(mix(5), g1_ref[...])
    k = _dot(mix(2), wk_ref[...])
    wl = _dot(jnp.tanh(tw).astype(BF16), w2_ref[...])
    al = _dot(ta.astype(BF16), a2_ref[...])
    g_out[0] = _dot(_sigmoid(tg).astype(BF16), g2_ref[...]).astype(g_out.dtype)

    lw_out[0] = -EXP_M_HALF * _sigmoid(w0_ref[...] + wl)
    a = _sigmoid(a0_ref[...] + al)
    kk = k * kkw_ref[...]
    kk = kk * lax.rsqrt(jnp.maximum(_head_sum(kk * kk), 1e-24))
    k_out[0] = (k * (1.0 + (a - 1.0) * kaw_ref[...])).astype(k_out.dtype)
    kk_out[0] = kk.astype(kk_out.dtype)
    b_out[0] = (kk * a).astype(b_out.dtype)

    r_out[0] = _dot(mix(0), wr_ref[...]).astype(r_out.dtype)
    v_out[0] = _dot(mix(3), wv_ref[...]).astype(v_out.dtype)


def _rwkv_prep(x, ng, mu, wr, wk, wv, w0, w1, w2, a0, a1, a2, g1, g2, kkw, kaw, *, tm):
    B, T, D = x.shape
    row = lambda a: a.reshape(1, D)
    consts = [row(ng), mu, wr, wk, wv, row(w0), w1, w2, row(a0), a1, a2, g1, g2, row(kkw), row(kaw)]
    tile = pl.BlockSpec((1, tm, D), lambda b, i: (b, i, 0))
    prev = pl.BlockSpec((1, 8, D), lambda b, i: (b, jnp.maximum(i * (tm // 8) - 1, 0), 0))
    out = lambda dt: jax.ShapeDtypeStruct((B, T, D), dt)
    return pl.pallas_call(
        _rwkv_prep_kernel,
        grid=(B, T // tm),
        in_specs=[tile, prev] + [_const_spec(c.shape) for c in consts],
        out_specs=[tile] * 7,
        out_shape=[out(BF16), out(F32)] + [out(BF16)] * 5,
        compiler_params=_params(("parallel", "parallel")),
        name="rwkv_prep",
    )(x, x, *consts)


def _blockdiag(z, lo):
    z = z.astype(BF16)
    zero = jnp.zeros_like(z)
    return jnp.concatenate([jnp.where(lo, z, zero), jnp.where(lo, zero, z)], axis=0)


def _rwkv_scan_kernel(r_ref, lw_ref, k_ref, v_ref, kk_ref, b_ref, y_out, s_scr, *,
                      nchunk, nbatch, npairs):
    c = pl.program_id(2)

    @pl.when(c == 0)
    def _():
        s_scr[...] = jnp.zeros_like(s_scr)

    C = CHUNK
    t_i = lax.broadcasted_iota(jnp.int32, (C, C), 0)
    j_i = lax.broadcasted_iota(jnp.int32, (C, C), 1)
    ltri = (j_i <= t_i).astype(BF16)
    row = lax.broadcasted_iota(jnp.int32, (C, LANES), 0)
    lane = lax.broadcasted_iota(jnp.int32, (C, LANES), 1)
    lo = lane < HEAD_DIM
    col = jnp.bitwise_and(lane, HEAD_DIM - 1)
    strict = col < row
    incl = col <= row
    rr = lax.broadcasted_iota(jnp.int32, (LANES, LANES), 0)
    cc = lax.broadcasted_iota(jnp.int32, (LANES, LANES), 1)
    same_head = (rr < HEAD_DIM) == (cc < HEAD_DIM)
    bd = functools.partial(_blockdiag, lo=lo)
    cat0 = lambda *xs: jnp.concatenate([x.astype(BF16) for x in xs], axis=0)
    cat1 = lambda *xs: jnp.concatenate([x.astype(BF16) for x in xs], axis=1)

    nstate = nbatch * npairs
    units = [(bi, slice(ci * C, (ci + 1) * C), slice(p * LANES, (p + 1) * LANES))
             for ci in range(nchunk) for bi in range(nbatch) for p in range(npairs)]
    P = range(len(units))
    r = [r_ref[u].astype(F32) for u in units]
    lw = [lw_ref[u] for u in units]
    k = [k_ref[u].astype(F32) for u in units]
    v = [v_ref[u].astype(F32) for u in units]
    kk = [kk_ref[u].astype(F32) for u in units]
    b = [b_ref[u].astype(F32) for u in units]

    cw2 = [_dot(ltri, cat1(*_split(lw[p]))) for p in P]
    cw = [cw2[p][:, :LANES] + cw2[p][:, LANES:] for p in P]
    cwl = [cw[p][C - 1:C, :] for p in P]
    at = [-kk[p] * jnp.exp(cw[p] - lw[p]) for p in P]
    dinv = [jnp.exp(-cw[p]) for p in P]
    rt = [r[p] * jnp.exp(cw[p]) for p in P]
    dend = [jnp.exp(cwl[p] - cw[p]) for p in P]

    x = [_dot_nt(cat0(at[p], rt[p]), cat0(bd(b[p] * dinv[p]), bd(k[p] * dinv[p]))) for p in P]
    aab = [jnp.where(strict, x[p][:C, :LANES], 0.0) for p in P]
    arb = [jnp.where(incl, x[p][C:, :LANES], 0.0) for p in P]
    aak = [jnp.where(strict, x[p][:C, LANES:], 0.0) for p in P]
    ark = [jnp.where(incl, x[p][C:, LANES:], 0.0) for p in P]

    bdv = [bd(v[p]) for p in P]
    av = [_dot(aak[p].astype(BF16), bdv[p]) for p in P]
    eye = jnp.where(col == row, 1.0, 0.0)
    tinv = [eye + aab[p] for p in P]
    n = [_dot(aab[p].astype(BF16), bd(aab[p])) for p in P]
    for it in range(5):
        last = it == 4
        res = [_dot(n[p].astype(BF16),
                    jnp.concatenate([bd(tinv[p])] + ([] if last else [bd(n[p])]), axis=1)) for p in P]
        tinv = [tinv[p] + res[p][:, :LANES] for p in P]
        if not last:
            n = [res[p][:, LANES:] for p in P]
    z = [_dot(tinv[p].astype(BF16), jnp.concatenate([bd(at[p]), bd(av[p])], axis=1)) for p in P]
    z1 = [z[p][:, :LANES] for p in P]
    z2 = [z[p][:, LANES:] for p in P]

    s = [s_scr[q] for q in range(nstate)]
    for ci in range(nchunk):
        Q = range(ci * nstate, (ci + 1) * nstate)
        ws = [_dot_nt(cat0(z1[p], rt[p]), s[p - Q[0]].astype(BF16)) for p in Q]
        u = [ws[p - Q[0]][:C] + z2[p] for p in Q]
        for p in Q:
            y_out[units[p]] = ws[p - Q[0]][C:] + _dot(cat1(arb[p], ark[p]),
                                                     cat0(bd(u[p - Q[0]]), bdv[p]))
        upd = [_dot_tn(cat0(u[p - Q[0]], v[p]), cat0(b[p] * dend[p], k[p] * dend[p])) for p in Q]
        s = [s[p - Q[0]] * jnp.exp(cwl[p]) + jnp.where(same_head, upd[p - Q[0]], 0.0) for p in Q]
    for q in range(nstate):
        s_scr[q] = s[q]


def _rwkv_scan(r, lw, k, v, kk, b, *, nchunk, nbatch, npairs):
    B, T, D = r.shape
    W = npairs * LANES
    tile = pl.BlockSpec((nbatch, nchunk * CHUNK, W), lambda bi, p, c: (bi, c, p))
    return pl.pallas_call(
        functools.partial(_rwkv_scan_kernel, nchunk=nchunk, nbatch=nbatch, npairs=npairs),
        grid=(B // nbatch, D // W, T // (nchunk * CHUNK)),
        in_specs=[tile] * 6,
        out_specs=tile,
        out_shape=jax.ShapeDtypeStruct((B, T, D), F32),
        scratch_shapes=[pltpu.VMEM((nbatch * npairs, LANES, LANES), F32)],
        compiler_params=_params(("parallel", "parallel", "arbitrary")),
        name="rwkv_scan",
    )(r, lw, k, v, kk, b)


def _mlp_tail(x, g_ref, win_ref, wout_ref, o_ref, tf, side_work=None):
    xn = _rms(x, g_ref[...]).astype(BF16)
    acc = x
    nf = win_ref.shape[-1] // tf
    for f in range(nf):
        if side_work is not None:
            side_work(2 * f, 2 * nf)
        hid = jnp.maximum(_dot(xn, win_ref[:, f * tf:(f + 1) * tf]), 0.0)
        if side_work is not None:
            side_work(2 * f + 1, 2 * nf)
        acc = acc + _dot((hid * hid).astype(BF16), wout_ref[f * tf:(f + 1) * tf, :])
    o_ref[...] = acc


def _proj_mlp_kernel(res_ref, a_ref, wo_ref, g_ref, win_ref, wout_ref, o_ref, *, tf):
    x = res_ref[...] + _dot(a_ref[...], wo_ref[...])
    _mlp_tail(x, g_ref, win_ref, wout_ref, o_ref, tf)


def _rwkv_out_mlp_kernel(res_ref, y_ref, r_ref, k_ref, v_ref, gate_ref, rk_ref, lnw_ref, lnb_ref,
                         wo_ref, g_ref, win_ref, wout_ref, o_ref, a_scr, *, tf):
    i = pl.program_id(0)

    tm = y_ref.shape[0]

    def output_stage(f=0, n=1):
        rows = slice(f * (tm // n), (f + 1) * (tm // n))
        y = y_ref[rows, :]
        d = y - _head_sum(y) * (1.0 / HEAD_DIM)
        var = _head_sum(d * d) * (1.0 / HEAD_DIM)
        yn = d * lax.rsqrt(var + GN_EPS)
        r = r_ref[rows, :].astype(F32)
        k = k_ref[rows, :].astype(F32)
        bonus = _head_sum(r * k * rk_ref[...]) * v_ref[rows, :].astype(F32)
        a = (yn * lnw_ref[...] + lnb_ref[...] + bonus) * gate_ref[rows, :].astype(F32)
        a_scr[rows, :] = a.astype(a_scr.dtype)

    @pl.when(i == 0)
    def _():
        output_stage()

    @pl.when(i > 0)
    def _():
        x = res_ref[...] + _dot(a_scr[...], wo_ref[...])
        _mlp_tail(x, g_ref, win_ref, wout_ref, o_ref, tf, side_work=output_stage)


def _proj_mlp(res, mix, w_o, g, w_in, w_out, *, layer, tm, tf, rwkv=None):
    M, D = res.shape
    tile = pl.BlockSpec((tm, D), lambda i: (i, 0))
    once = lambda shape: pl.BlockSpec(shape, lambda i: (0,) * len(shape), pipeline_mode=pl.Buffered(1))
    of_layer = lambda w: pl.BlockSpec((None,) + w.shape[1:], lambda i: (layer, 0, 0),
                                      pipeline_mode=pl.Buffered(1))
    vec = once((1, D))
    weights = [once(w_o.shape), vec, of_layer(w_in), of_layer(w_out)]
    wargs = (w_o, g.reshape(1, D), w_in, w_out)
    n = M // tm
    if rwkv is None:
        body, grid, specs, args, out_spec, scratch = _proj_mlp_kernel, n, [tile, tile], (res, mix), tile, []
    else:
        r, k, v, gate, rk, lnw, lnb = rwkv
        prev = pl.BlockSpec((tm, D), lambda i: (jnp.maximum(i - 1, 0), 0))
        cur = pl.BlockSpec((tm, D), lambda i: (jnp.minimum(i, n - 1), 0))
        body, grid, specs, out_spec = _rwkv_out_mlp_kernel, n + 1, [prev] + [cur] * 5 + [vec] * 3, prev
        args = (res, mix, r, k, v, gate, rk.reshape(1, D), lnw.reshape(1, D), lnb.reshape(1, D))
        scratch = [pltpu.VMEM((tm, D), BF16)]
    return pl.pallas_call(
        functools.partial(body, tf=tf),
        grid=(grid,),
        in_specs=specs + weights,
        out_specs=out_spec,
        out_shape=jax.ShapeDtypeStruct((M, D), F32),
        scratch_shapes=scratch,
        compiler_params=_params(("arbitrary",)),
        name="proj_mlp",
    )(*args, *wargs)


def _store_q_tiles(qt, c_rows, qg_ref, q_out, *, scale, tq):
    tm = qt.shape[1]
    row = lax.broadcasted_iota(jnp.int32, (HEAD_DIM, tm), 0)
    for h in range(qt.shape[0] // HEAD_DIM):
        hs = slice(h * HEAD_DIM, (h + 1) * HEAD_DIM)
        qh = qt[hs, :]
        ms = jnp.mean(qh * qh, axis=0, keepdims=True)
        qn = qh * lax.rsqrt(ms + NORM_EPS) * (qg_ref[hs, :] * scale)
        hi, mid, lo = _split3(c_rows[h:h + 1, :] * LOG2E)
        aug = jnp.where(row == 0, hi, jnp.where(row == 1, mid, jnp.where(
            row == 2, lo, jnp.where(row < 6, 1.0, 0.0))))
        tile = jnp.concatenate([qn, aug], axis=0).astype(q_out.dtype)
        for sb in range(tm // tq):
            q_out[0, h, sb] = tile[:, sb * tq:(sb + 1) * tq]


def _q_proj_kernel(x_ref, g_ref, wt_ref, qg_ref, c_ref, q_out, *, scale, tq):
    hn = _rms(x_ref[0], g_ref[...])
    qt = _dot_nt(wt_ref[...], hn.astype(BF16))
    _store_q_tiles(qt, c_ref[0], qg_ref, q_out, scale=scale, tq=tq)


def _q_proj(x, g, wt, qg, c_row, *, tm, tq, scale):
    B, T, D = x.shape
    H = D // HEAD_DIM
    return pl.pallas_call(
        functools.partial(_q_proj_kernel, scale=scale, tq=tq),
        grid=(B, T // tm),
        in_specs=[pl.BlockSpec((1, tm, D), lambda b, i: (b, i, 0)), _const_spec((1, D)),
                  _const_spec(wt.shape), _const_spec((D, 1)),
                  pl.BlockSpec((1, H, tm), lambda b, i: (b, 0, i))],
        out_specs=pl.BlockSpec((1, H, tm // tq, LANES, tq), lambda b, i: (b, 0, i, 0, 0)),
        out_shape=jax.ShapeDtypeStruct((B, H, T // tq, LANES, tq), BF16),
        compiler_params=_params(("parallel", "parallel")),
        name="q_proj",
    )(x, g.reshape(1, D), wt, qg.reshape(D, 1), c_row)


def _shared_kv_kernel(x_ref, g_ref, wk_ref, wvt_ref, wf_ref, fb_ref, kg_ref, sel_ref,
                      gq_ref, wqt_ref, qg_ref,
                      k_out, vt_out, c_out, q_out, carry_scr, *, tq, scale):
    i = pl.program_id(1)

    @pl.when(i == 0)
    def _():
        carry_scr[...] = jnp.zeros_like(carry_scr)

    x = x_ref[0]
    xn = x * lax.rsqrt(jnp.mean(x * x, axis=-1, keepdims=True) + NORM_EPS)
    hn = xn * g_ref[...]
    hb = hn.astype(BF16)
    tm, D = hn.shape
    H = D // HEAD_DIM
    k = _head_rms(_dot(hb, wk_ref[...]), kg_ref[...])
    vt = _dot_nt(wvt_ref[...], hb)

    f = _dot(hb, wf_ref[...]) + fb_ref[...]
    logf = jnp.minimum(f, 0.0) - jnp.log(1.0 + jnp.exp(-jnp.abs(f)))
    t_i = lax.broadcasted_iota(jnp.int32, (tm, tm), 0)
    j_i = lax.broadcasted_iota(jnp.int32, (tm, tm), 1)
    ltri = (j_i <= t_i).astype(BF16)
    c3 = _dot(ltri, jnp.concatenate([t.astype(BF16) for t in _split3(logf)], axis=1))
    c = (c3[:, :LANES] + c3[:, LANES:2 * LANES]) + c3[:, 2 * LANES:] + carry_scr[0:1, :]
    carry_scr[...] = jnp.broadcast_to(c[tm - 1:tm, :], carry_scr.shape)
    c_rows = c.T[:H, :]
    c_out[0] = c_rows
    qt = _dot_nt(wqt_ref[...], (xn * gq_ref[...]).astype(BF16))
    _store_q_tiles(qt, c_rows, qg_ref, q_out, scale=scale, tq=tq)

    lane = lax.broadcasted_iota(jnp.int32, (tm, LANES), 1)
    hi, mid, lo = (jnp.where(lane < H, t, 0.0) for t in _split3(c * (-LOG2E)))
    packed = (hi + pltpu.roll(mid, H, 1)) + (pltpu.roll(lo, 2 * H, 1) + jnp.where(lane == 3 * H, 1.0, 0.0))
    aug = _dot(packed.astype(BF16), sel_ref[...])
    vrow = lax.broadcasted_iota(jnp.int32, (V_ROWS - HEAD_DIM, tq), 0)
    ones_row = jnp.where(vrow == 0, 1.0, 0.0)
    for h in range(H):
        base = k[:, (h // 2) * LANES:(h // 2 + 1) * LANES]
        if h % 2:
            base = pltpu.roll(base, HEAD_DIM, 1)
        tile = jnp.where(lane < HEAD_DIM, base, aug[:, h * LANES:(h + 1) * LANES])
        k_out[0, h] = tile.astype(k_out.dtype)
        for sb in range(tm // tq):
            vt_out[0, h, sb] = jnp.concatenate(
                [vt[h * HEAD_DIM:(h + 1) * HEAD_DIM, sb * tq:(sb + 1) * tq], ones_row],
                axis=0).astype(vt_out.dtype)


def _bias_selector(H):
    assert 3 * H + 1 <= LANES
    sel = np.zeros((LANES, H * LANES), np.float32)
    for h in range(H):
        sel[3 * H, h * LANES + HEAD_DIM:h * LANES + HEAD_DIM + 3] = 1.0
        for t in range(3):
            sel[t * H + h, h * LANES + HEAD_DIM + 3 + t] = 1.0
    return jnp.asarray(sel, BF16)


def _shared_kv(x, g, wk, wvt, wf, fb, kg, gq, wqt, qg, *, tm, tq, scale):
    B, T, D = x.shape
    H = D // HEAD_DIM
    sel = _bias_selector(H)
    slabs = lambda rows: pl.BlockSpec((1, H, tm // tq, rows, tq), lambda b, i: (b, 0, i, 0, 0))
    return pl.pallas_call(
        functools.partial(_shared_kv_kernel, tq=tq, scale=scale),
        grid=(B, T // tm),
        in_specs=[pl.BlockSpec((1, tm, D), lambda b, i: (b, i, 0)), _const_spec((1, D)),
                  _const_spec(wk.shape), _const_spec(wvt.shape), _const_spec(wf.shape),
                  _const_spec((1, LANES)), _const_spec((1, D)), _const_spec(sel.shape),
                  _const_spec((1, D)), _const_spec(wqt.shape), _const_spec((D, 1))],
        out_specs=[pl.BlockSpec((1, H, tm, LANES), lambda b, i: (b, 0, i, 0)), slabs(V_ROWS),
                   pl.BlockSpec((1, H, tm), lambda b, i: (b, 0, i)), slabs(LANES)],
        out_shape=[jax.ShapeDtypeStruct((B, H, T, LANES), BF16),
                   jax.ShapeDtypeStruct((B, H, T // tq, V_ROWS, tq), BF16),
                   jax.ShapeDtypeStruct((B, H, T), F32),
                   jax.ShapeDtypeStruct((B, H, T // tq, LANES, tq), BF16)],
        scratch_shapes=[pltpu.VMEM((8, LANES), F32)],
        compiler_params=_params(("parallel", "arbitrary")),
        name="shared_kv",
    )(x, g.reshape(1, D), wk, wvt, wf, fb, kg.reshape(1, D), sel,
      gq.reshape(1, D), wqt, qg.reshape(D, 1))


def _fox_attn_kernel(q_ref, k_ref, vt_ref, o_ref, acc_scr, *, tq, nh):
    i = pl.program_id(2)
    qt = [q_ref[0, h, 0, :QK_DEPTH, :] for h in range(nh)]
    acc_scr[...] = jnp.zeros_like(acc_scr)
    key_i = lax.broadcasted_iota(jnp.int32, (tq, tq), 0)
    qry_i = lax.broadcasted_iota(jnp.int32, (tq, tq), 1)
    causal = key_i <= qry_i

    def step(j, m, masked):
        off = pl.multiple_of(j * tq, tq)
        s, p, alpha, m_new = {}, {}, {}, [None] * nh
        for t in range(nh + 2 * SKEW):
            if t < nh:
                s[t] = _dot(k_ref[0, t, pl.ds(off, tq), :QK_DEPTH], qt[t])
                if masked:
                    s[t] = jnp.where(causal, s[t], NEG_BIG)
            h = t - SKEW
            if 0 <= h < nh:
                m_new[h] = jnp.maximum(m[h], jnp.max(s[h], axis=0, keepdims=True))
                p[h] = jnp.exp2(s.pop(h) - m_new[h]).astype(BF16)
                alpha[h] = jnp.exp2(m[h] - m_new[h])
            h = t - 2 * SKEW
            if 0 <= h < nh:
                acc_scr[h] = alpha[h] * acc_scr[h] + _dot(vt_ref[0, h, j], p.pop(h))
        return tuple(m_new)

    m0 = tuple(jnp.full((1, tq), NEG_BIG, F32) for _ in range(nh))
    m = lax.fori_loop(0, i, functools.partial(step, masked=False), m0)
    step(i, m, True)
    ot = [acc_scr[h, :HEAD_DIM, :] * (1.0 / acc_scr[h, HEAD_DIM:HEAD_DIM + 1, :]) for h in range(nh)]
    o_ref[0] = jnp.concatenate(ot, axis=0).T.astype(o_ref.dtype)


def _fox_attn(qt, ka, vt, *, nh):
    B, H, nb, _, tq = qt.shape
    T = nb * tq
    return pl.pallas_call(
        functools.partial(_fox_attn_kernel, tq=tq, nh=nh),
        grid=(B, H // nh, nb),
        in_specs=[pl.BlockSpec((1, nh, 1, LANES, tq), lambda b, p, i: (b, p, i, 0, 0)),
                  pl.BlockSpec((1, nh, T, LANES), lambda b, p, i: (b, p, 0, 0)),
                  pl.BlockSpec((1, nh, nb, V_ROWS, tq), lambda b, p, i: (b, p, 0, 0, 0))],
        out_specs=pl.BlockSpec((1, tq, nh * HEAD_DIM), lambda b, p, i: (b, i, p)),
        out_shape=jax.ShapeDtypeStruct((B, T, H * HEAD_DIM), BF16),
        scratch_shapes=[pltpu.VMEM((nh, V_ROWS, tq), F32)],
        compiler_params=_params(("parallel", "parallel", "arbitrary")),
        name="fox_attn",
    )(qt, ka, vt)


def kernel(x, rwkv_norm_g, rwkv_mu, rwkv_w_rkv, rwkv_w0, rwkv_w1, rwkv_w2, rwkv_a0, rwkv_a1, rwkv_a2, rwkv_g1, rwkv_g2, rwkv_k_k, rwkv_k_a, rwkv_r_k, rwkv_lnx_w, rwkv_lnx_b, rwkv_w_o, kv_norm_g, kv_w, kv_f_bias, k_norm_g, attn_norm_g, attn_w_q, q_norm_g, attn_w_o, mlp_norm_g, mlp_w_in, mlp_w_out):
    B, T, D = x.shape
    M = B * T
    n_a = rwkv_norm_g.shape[0]
    depth = mlp_norm_g.shape[0]
    bf = lambda w: w.astype(BF16)
    tm = min(512, T)
    tq = min(256, T)
    tf = min(1024, mlp_w_in.shape[-1])
    npairs = D // LANES

    w_in_all, w_out_all = bf(mlp_w_in), bf(mlp_w_out)
    q_scale = HEAD_DIM ** -0.5 * LOG2E
    k_sh = v_sh = c_sh = q_first = None
    for layer in range(depth):
        if layer < n_a:
            i = layer
            r, lw, k, v, kk, b, g = _rwkv_prep(
                x, rwkv_norm_g[i], rwkv_mu[i], bf(rwkv_w_rkv[i, 0]), bf(rwkv_w_rkv[i, 1]),
                bf(rwkv_w_rkv[i, 2]), rwkv_w0[i], bf(rwkv_w1[i]), bf(rwkv_w2[i]), rwkv_a0[i],
                bf(rwkv_a1[i]), bf(rwkv_a2[i]), bf(rwkv_g1[i]), bf(rwkv_g2[i]),
                rwkv_k_k[i], rwkv_k_a[i], tm=tm)
            mix = _rwkv_scan(r, lw, k, v, kk, b, nchunk=4 if T % (4 * CHUNK) == 0 else 1,
                             nbatch=2 if B % 2 == 0 else 1, npairs=npairs)
            flat = lambda t: t.reshape(M, D)
            w_o, extra = rwkv_w_o[i], (flat(r), flat(k), flat(v), flat(g), rwkv_r_k[i].reshape(D),
                                       rwkv_lnx_w[i], rwkv_lnx_b[i])
        else:
            j = layer - n_a
            qt = q_first if j == 0 else _q_proj(x, attn_norm_g[j], bf(attn_w_q[j].T), q_norm_g[j],
                                                 c_sh, tm=tm, tq=tq, scale=q_scale)
            mix = _fox_attn(qt, k_sh, v_sh, nh=min(16, D // HEAD_DIM))
            w_o, extra = attn_w_o[j], None
        x = _proj_mlp(x.reshape(M, D), mix.reshape(M, D), bf(w_o), mlp_norm_g[layer],
                      w_in_all, w_out_all, layer=layer, tm=tm, tf=tf, rwkv=extra).reshape(B, T, D)
        if layer == n_a - 1:
            wf = jnp.pad(kv_w[:, 2 * D:], ((0, 0), (0, LANES - (kv_w.shape[1] - 2 * D))))
            fb = jnp.pad(kv_f_bias, (0, LANES - kv_f_bias.shape[0])).reshape(1, LANES)
            k_sh, v_sh, c_sh, q_first = _shared_kv(
                x, kv_norm_g, bf(kv_w[:, :D]), bf(kv_w[:, D:2 * D].T), bf(wf), fb, k_norm_g,
                attn_norm_g[0], bf(attn_w_q[0].T), q_norm_g[0], tm=tm, tq=tq, scale=q_scale)
    return x
```
